```python
import jax, jax.numpy as jnp
from jax import lax
import numpy as np

D_MODEL = 1024
BATCH = 4
SEQ = 4096
DEPTH = 4
DEC_BATCH = 32
DEC_SEQ = 1
PAST_LEN = 8192
PAGE_SIZE = 128

N_EVEN = (DEPTH + 1) // 2
N_ODD = DEPTH // 2
C_CONV = D_MODEL // 2
CONV_W = 3
H_FOX = 8
HD_FOX = (D_MODEL // 2) // H_FOX
D_FOX = H_FOX * HD_FOX
H_GLA = 4
DK_GLA = (D_MODEL // 2) // H_GLA
DV_GLA = D_MODEL // H_GLA
GATE_RANK = 16
GLA_TAU = 16.0
GLA_CHUNK = 64
D_FF = 4 * D_MODEL
Q_BLOCK = 128
EPS = 1e-6

kernel_name = "hybrid_conv_fox_gla_decoder_step"


def rms_norm(x, gain):
    xf = x.astype(jnp.float32)
    y = xf * lax.rsqrt(jnp.mean(xf * xf, axis=-1, keepdims=True) + EPS)
    return (y * gain.astype(jnp.float32)).astype(x.dtype)


def adaln(c, w_ada, b_ada):
    mod = jax.nn.silu(c) @ w_ada + b_ada
    return jnp.split(mod[:, None, :], 6, axis=-1)


def modulate(h, shift, scale):
    return h * (1 + scale) + shift


def short_conv(u, prev, w_conv):
    L = u.shape[1]
    full = jnp.concatenate([prev.astype(u.dtype), u], axis=1)
    out = sum(full[:, j:j + L] * w_conv[j] for j in range(CONV_W))
    return out, full[:, -(CONV_W - 1):]


def fox_attend(q, k, v, f_q, f_k, q_pos, k_pos):
    s = jnp.einsum('bqhd,bkhd->bhqk', q, k, preferred_element_type=jnp.float32)
    s = s + jnp.swapaxes(f_q, 1, 2)[..., :, None] - jnp.swapaxes(f_k, 1, 2)[..., None, :]
    s = jnp.where(k_pos[None, :] <= q_pos[:, None], s, -1e30)
    p = jax.nn.softmax(s, axis=-1)
    return jnp.einsum('bhqk,bkhd->bqhd', p.astype(v.dtype), v)


def fox_prompt(q, k, v, f):
    B, L, H, Dh = q.shape
    nb = L // Q_BLOCK
    qb = q.reshape(B, nb, Q_BLOCK, H, Dh).transpose(1, 0, 2, 3, 4)
    fb = f.reshape(B, nb, Q_BLOCK, H).transpose(1, 0, 2, 3)
    k_pos = jnp.arange(L)

    def block(args):
        i, q_i, f_i = args
        q_pos = i * Q_BLOCK + jnp.arange(Q_BLOCK)
        return fox_attend(q_i, k, v, f_i, f, q_pos, k_pos)

    out = lax.map(block, (jnp.arange(nb), qb, fb))
    return out.transpose(1, 0, 2, 3, 4).reshape(B, L, H, Dh)


def even_mixer(h, w_in, b_f, g_q, g_k, w_conv, w_out, conv_prev, past):
    B, L, _ = h.shape
    idx = [C_CONV, 2 * C_CONV, 3 * C_CONV, 3 * C_CONV + D_FOX, 3 * C_CONV + 2 * D_FOX, 3 * C_CONV + 3 * D_FOX]
    gb, gc, xin, q, k, v, fz = jnp.split(h @ w_in, idx, axis=-1)
    conv_out, conv_new = short_conv(gc * xin, conv_prev, w_conv)
    y_a = gb * conv_out
    q = rms_norm(q.reshape(B, L, H_FOX, HD_FOX), g_q) * (HD_FOX ** -0.5)
    k = rms_norm(k.reshape(B, L, H_FOX, HD_FOX), g_k)
    v = v.reshape(B, L, H_FOX, HD_FOX)
    logf = jax.nn.log_sigmoid((fz + b_f).astype(jnp.float32))
    if past is None:
        o = fox_prompt(q, k, v, jnp.cumsum(logf, axis=1))
    else:
        k_past, v_past, lf_past = past
        P = k_past.shape[1]
        k_all = jnp.concatenate([k_past.astype(k.dtype), k], axis=1)
        v_all = jnp.concatenate([v_past.astype(v.dtype), v], axis=1)
        f_all = jnp.cumsum(jnp.concatenate([lf_past.astype(jnp.float32), logf], axis=1), axis=1)
        o = fox_attend(q, k_all, v_all, f_all[:, P:], f_all, P + jnp.arange(L), jnp.arange(P + L))
    y = jnp.concatenate([y_a, o.reshape(B, L, D_FOX)], axis=-1) @ w_out
    return y, conv_new, k, v, logf.astype(h.dtype)


def gla_recurrence(q, k, v, g, s0):
    B, L, H, _ = q.shape
    DV = v.shape[-1]
    C = GLA_CHUNK
    n = -(-L // C)
    padw = ((0, 0), (0, n * C - L), (0, 0), (0, 0))
    q, k, v, g = [jnp.pad(a, padw).reshape(B, n, C, H, a.shape[-1]) for a in (q, k, v, g)]
    b = jnp.cumsum(g, axis=2)
    b_last = b[:, :, -1:]
    q_e = q * jnp.exp(b)
    k_e = k * jnp.exp(-b)
    k_d = k * jnp.exp(b_last - b)
    causal = jnp.tril(jnp.ones((C, C), bool))
    a = jnp.where(causal, jnp.einsum('bnihk,bnjhk->bnhij', q_e, k_e), 0.0)
    o_intra = jnp.einsum('bnhij,bnjhv->bnihv', a, v)
    decay = jnp.exp(b_last[:, :, 0])

    def step(s, xs):
        q_c, k_c, v_c, d_c = xs
        o_c = jnp.einsum('bihk,bhkv->bihv', q_c, s)
        s = d_c[..., None] * s + jnp.einsum('bjhk,bjhv->bhkv', k_c, v_c)
        return s, o_c

    mv = lambda a: jnp.moveaxis(a, 1, 0)
    s_fin, o_inter = lax.scan(step, s0, (mv(q_e), mv(k_d), mv(v), mv(decay)))
    o = o_intra + jnp.moveaxis(o_inter, 0, 1)
    return o.reshape(B, n * C, H, DV)[:, :L], s_fin


def odd_mixer(h, w_in, w_gk, b_gk, g_o, w_out, s0):
    B, L, _ = h.shape
    QK = H_GLA * DK_GLA
    VV = H_GLA * DV_GLA
    q, k, v, r, gd = jnp.split(h @ w_in, [QK, 2 * QK, 2 * QK + VV, 2 * QK + 2 * VV], axis=-1)
    q = q.reshape(B, L, H_GLA, DK_GLA).astype(jnp.float32) * (DK_GLA ** -0.5)
    k = k.reshape(B, L, H_GLA, DK_GLA).astype(jnp.float32)
    v = v.reshape(B, L, H_GLA, DV_GLA).astype(jnp.float32)
    g = jax.nn.log_sigmoid((gd @ w_gk + b_gk).astype(jnp.float32)) / GLA_TAU
    g = g.reshape(B, L, H_GLA, DK_GLA)
    o, s_fin = gla_recurrence(q, k, v, g, s0.astype(jnp.float32))
    o = rms_norm(o, g_o).astype(h.dtype) * jax.nn.silu(r).reshape(B, L, H_GLA, DV_GLA)
    return o.reshape(B, L, VV) @ w_out, s_fin.astype(h.dtype)


def sq_relu_mlp(h, w1, w2):
    return jnp.square(jax.nn.relu(h @ w1)) @ w2


def setup_inputs(seed: int = 0) -> dict:
    key = jax.random.key(seed)
    ks = iter(jax.random.split(key, 32))
    nrm = lambda shape, s: jax.random.normal(next(ks), shape, jnp.float32) * s
    n_pages = PAST_LEN // PAGE_SIZE
    pool = (DEC_BATCH * n_pages * 5) // 4
    e_in = 3 * C_CONV + 3 * D_FOX + H_FOX
    o_in = 2 * H_GLA * DK_GLA + 2 * H_GLA * DV_GLA + GATE_RANK
    x_prompt = nrm((BATCH, SEQ, D_MODEL), 1.0)
    x_sample = nrm((DEC_BATCH, DEC_SEQ, D_MODEL), 1.0)
    c_prompt = nrm((BATCH, D_MODEL), 1.0)
    c_sample = nrm((DEC_BATCH, D_MODEL), 1.0)
    cache_k = nrm((pool, N_EVEN, PAGE_SIZE, H_FOX, HD_FOX), 1.0)
    cache_v = nrm((pool, N_EVEN, PAGE_SIZE, H_FOX, HD_FOX), 1.0)
    cache_logf = jax.nn.log_sigmoid(3.0 + nrm((pool, N_EVEN, PAGE_SIZE, H_FOX), 1.0))
    page_table = jax.random.permutation(next(ks), pool)[:DEC_BATCH * n_pages].reshape(DEC_BATCH, n_pages).astype(jnp.int32)
    state_conv = nrm((DEC_BATCH, N_EVEN, CONV_W - 1, C_CONV), 1.0)
    state_gla = nrm((DEC_BATCH, N_ODD, H_GLA, DK_GLA, DV_GLA), 1.0)
    g_mix = 1.0 + nrm((DEPTH, D_MODEL), 0.05)
    g_ffn = 1.0 + nrm((DEPTH, D_MODEL), 0.05)
    w_ada = nrm((DEPTH, D_MODEL, 6 * D_MODEL), 0.5 * D_MODEL ** -0.5)
    b_ada = nrm((DEPTH, 6 * D_MODEL), 0.01)
    w_in_even = nrm((N_EVEN, D_MODEL, e_in), D_MODEL ** -0.5)
    b_f = 1.0 + 4.0 * jax.random.uniform(next(ks), (N_EVEN, H_FOX), jnp.float32)
    g_q = 1.0 + nrm((N_EVEN, HD_FOX), 0.05)
    g_k = 1.0 + nrm((N_EVEN, HD_FOX), 0.05)
    w_conv = nrm((N_EVEN, CONV_W, C_CONV), CONV_W ** -0.5)
    w_out_even = nrm((N_EVEN, C_CONV + D_FOX, D_MODEL), (C_CONV + D_FOX) ** -0.5)
    w_in_odd = nrm((N_ODD, D_MODEL, o_in), D_MODEL ** -0.5)
    w_gk = nrm((N_ODD, GATE_RANK, H_GLA * DK_GLA), GATE_RANK ** -0.5)
    b_gk = nrm((N_ODD, H_GLA * DK_GLA), 0.01)
    g_o = 1.0 + nrm((N_ODD, DV_GLA), 0.05)
    w_out_odd = nrm((N_ODD, H_GLA * DV_GLA, D_MODEL), (H_GLA * DV_GLA) ** -0.5)
    w_ff1 = nrm((DEPTH, D_MODEL, D_FF), D_MODEL ** -0.5)
    w_ff2 = nrm((DEPTH, D_FF, D_MODEL), D_FF ** -0.5)
    return {"x_prompt": x_prompt, "x_sample": x_sample, "c_prompt": c_prompt, "c_sample": c_sample,
            "cache_k": cache_k, "cache_v": cache_v, "cache_logf": cache_logf, "page_table": page_table,
            "state_conv": state_conv, "state_gla": state_gla, "g_mix": g_mix, "g_ffn": g_ffn,
            "w_ada": w_ada, "b_ada": b_ada, "w_in_even": w_in_even, "b_f": b_f, "g_q": g_q, "g_k": g_k,
            "w_conv": w_conv, "w_out_even": w_out_even, "w_in_odd": w_in_odd, "w_gk": w_gk, "b_gk": b_gk,
            "g_o": g_o, "w_out_odd": w_out_odd, "w_ff1": w_ff1, "w_ff2": w_ff2}


def reference(x_prompt, x_sample, c_prompt, c_sample, cache_k, cache_v, cache_logf, page_table,
              state_conv, state_gla, g_mix, g_ffn, w_ada, b_ada, w_in_even, b_f, g_q, g_k,
              w_conv, w_out_even, w_in_odd, w_gk, b_gk, g_o, w_out_odd, w_ff1, w_ff2):
    n_seq, n_pages = page_table.shape
    past_len = n_pages * cache_k.shape[2]

    def gather(cache, e):
        rows = cache[page_table, e]
        return rows.reshape((n_seq, past_len) + rows.shape[3:])

    def run_group(x, c, conv_prev, gla_prev, fox_past):
        fox_rows, conv_new, gla_new = [], [], []
        for li in range(DEPTH):
            sh_m, sc_m, gt_m, sh_f, sc_f, gt_f = adaln(c, w_ada[li], b_ada[li])
            h = modulate(rms_norm(x, g_mix[li]), sh_m, sc_m)
            if li % 2 == 0:
                e = li // 2
                m, buf, k_rows, v_rows, lf_rows = even_mixer(
                    h, w_in_even[e], b_f[e], g_q[e], g_k[e], w_conv[e], w_out_even[e],
                    conv_prev(e), fox_past(e))
                fox_rows.append((k_rows, v_rows, lf_rows))
                conv_new.append(buf)
            else:
                o = li // 2
                m, s_fin = odd_mixer(h, w_in_odd[o], w_gk[o], b_gk[o], g_o[o], w_out_odd[o], gla_prev(o))
                gla_new.append(s_fin)
            x = x + gt_m * m
            h = modulate(rms_norm(x, g_ffn[li]), sh_f, sc_f)
            x = x + gt_f * sq_relu_mlp(h, w_ff1[li], w_ff2[li])
        k_new = jnp.stack([r[0] for r in fox_rows], axis=1)
        v_new = jnp.stack([r[1] for r in fox_rows], axis=1)
        lf_new = jnp.stack([r[2] for r in fox_rows], axis=1)
        return x, k_new, v_new, lf_new, jnp.stack(conv_new, axis=1), jnp.stack(gla_new, axis=1)

    bp = x_prompt.shape[0]
    y_p, k_p, v_p, lf_p, conv_p, gla_p = run_group(
        x_prompt, c_prompt,
        lambda e: jnp.zeros((bp, CONV_W - 1, C_CONV), x_prompt.dtype),
        lambda o: jnp.zeros((bp, H_GLA, DK_GLA, DV_GLA), jnp.float32),
        lambda e: None)
    y_s, k_s, v_s, lf_s, conv_s, gla_s = run_group(
        x_sample, c_sample,
        lambda e: state_conv[:, e],
        lambda o: state_gla[:, o],
        lambda e: (gather(cache_k, e), gather(cache_v, e), gather(cache_logf, e)))
    return (y_p, y_s, k_p, v_p, lf_p, k_s, v_s, lf_s, conv_p, conv_s, gla_p, gla_s)
```

```python
import functools

import numpy as np
import jax
import jax.numpy as jnp
from jax import lax
from jax.experimental import pallas as pl
from jax.experimental.pallas import tpu as pltpu

F32 = jnp.float32
BF16 = jnp.bfloat16

EPS = 1e-6
GLA_TAU = 16.0
GLA_CHUNK = 64
LANES = 128
VMEM_LIMIT = 56 * 1024 * 1024
NEG = -1e30
PAGES_PER_STEP = 8
HEAD_ROWS = 16


def _cparams(*sem):
    return pltpu.CompilerParams(dimension_semantics=sem, vmem_limit_bytes=VMEM_LIMIT)


def _silu(x):
    return x / (1.0 + jnp.exp(-x))


def _log_sigmoid(z):
    return jnp.minimum(z, 0.0) - jnp.log1p(jnp.exp(-jnp.abs(z)))


def _norm_mod(x, gain, shift, scale):
    y = x * lax.rsqrt(jnp.mean(x * x, axis=-1, keepdims=True) + EPS)
    return (y * gain) * (1.0 + scale) + shift


def _split3(a):
    hi = a.astype(BF16)
    r = a - hi.astype(F32)
    mid = r.astype(BF16)
    lo = (r - mid.astype(F32)).astype(BF16)
    return hi, mid, lo


def _dot(a, b):
    return jnp.dot(a, b, preferred_element_type=F32)


def _dot_nt(a, b):
    return lax.dot_general(a, b, (((1,), (1,)), ((), ())), preferred_element_type=F32)


def _dot_tn(a, b):
    return lax.dot_general(a, b, (((0,), (0,)), ((), ())), preferred_element_type=F32)


def _sum3(c, w):
    return c[:, 0:w] + c[:, w:2 * w] + c[:, 2 * w:3 * w]


def _adaln_kernel(c_ref, w_ref, b_ref, o_ref):
    a = _silu(c_ref[...]).astype(BF16)
    o_ref[0] = _dot(a, w_ref[0].astype(BF16)) + b_ref[0]


def _adaln(c_all, w_ada, b_ada):
    depth, d, n = w_ada.shape
    r = c_all.shape[0]
    tn = n // 4
    return pl.pallas_call(
        _adaln_kernel,
        grid=(depth, n // tn),
        in_specs=[pl.BlockSpec((r, d), lambda l, j: (0, 0)),
                  pl.BlockSpec((1, d, tn), lambda l, j: (l, 0, j)),
                  pl.BlockSpec((1, 1, tn), lambda l, j: (l, 0, j))],
        out_specs=pl.BlockSpec((1, r, tn), lambda l, j: (l, 0, j)),
        out_shape=jax.ShapeDtypeStruct((depth, r, n), F32),
        compiler_params=_cparams("arbitrary", "arbitrary"),
        name="adaln",
    )(c_all, w_ada, b_ada.reshape(depth, 1, n))


def _even_project(x_ref, g_ref, sh_ref, sc_ref, w_ref, wf_ref, bf_ref, gq_ref, gk_ref, p_ref, cc, nh):
    hb = _norm_mod(x_ref[0], g_ref[...], sh_ref[0], sc_ref[0]).astype(BF16)
    gb = _dot(hb, w_ref[:, 0:cc])
    u = _dot(hb, w_ref[:, cc:2 * cc]) * _dot(hb, w_ref[:, 2 * cc:3 * cc])
    q = _dot(hb, w_ref[:, 3 * cc:4 * cc])
    k = _dot(hb, w_ref[:, 4 * cc:5 * cc])
    v = _dot(hb, w_ref[:, 5 * cc:6 * cc])
    hd = cc // nh
    qms = _dot((q * q).astype(BF16), p_ref[...]) * (1.0 / hd)
    kms = _dot((k * k).astype(BF16), p_ref[...]) * (1.0 / hd)
    qn = (q * lax.rsqrt(qms + EPS)) * gq_ref[...]
    kn = (k * lax.rsqrt(kms + EPS)) * gk_ref[...]
    fz = _dot(hb, wf_ref[...]) + bf_ref[...]
    lane = lax.broadcasted_iota(jnp.int32, fz.shape, 1)
    lf = jnp.where(lane < nh, _log_sigmoid(fz), 0.0)
    return gb, u, qn, kn, v, lf


def _even_seq_kernel(x_ref, g_ref, sh_ref, sc_ref, w_ref, wf_ref, bf_ref, gq_ref, gk_ref, wc_ref, p_ref,
                     tri_ref, eq_ref, ek_ref, oq_ref, ok_ref,
                     ya_ref, ut_ref, qp_ref, kp_ref, vb_ref, ko_ref, vo_ref, lf_ref,
                     ubuf, fcarry, *, nh, cc, tm):
    i = pl.program_id(1)
    gb, u, qn, kn, v, lf = _even_project(x_ref, g_ref, sh_ref, sc_ref, w_ref, wf_ref, bf_ref,
                                         gq_ref, gk_ref, p_ref, cc, nh)

    @pl.when(i == 0)
    def _():
        ubuf[0:8, :] = jnp.zeros((8, cc), F32)
        fcarry[...] = jnp.zeros_like(fcarry)

    ubuf[8:8 + tm, :] = u
    conv = wc_ref[0:1, :] * ubuf[6:6 + tm, :] + wc_ref[1:2, :] * ubuf[7:7 + tm, :] + wc_ref[2:3, :] * u
    ya_ref[0] = (gb * conv).astype(BF16)
    tail = ubuf[tm:tm + 8, :]
    ubuf[0:8, :] = tail
    ut_ref[0] = tail

    ko_ref[0] = kn
    vo_ref[0] = v
    vb_ref[0] = v.astype(BF16)
    lf_ref[0] = lf[:, 0:nh]

    c3 = _dot(tri_ref[...], jnp.concatenate(_split3(lf), axis=1))
    f = _sum3(c3, LANES) + fcarry[...]
    fcarry[...] = f[tm - 1:tm, :]
    f3 = jnp.concatenate(_split3(f), axis=1)

    qp_ref[0] = (_dot(jnp.concatenate([qn.astype(BF16), f3], axis=1), eq_ref[...]) + oq_ref[...]).astype(BF16)
    kp_ref[0] = (_dot(jnp.concatenate([kn.astype(BF16), f3], axis=1), ek_ref[...]) + ok_ref[...]).astype(BF16)


def _even_dec_kernel(x_ref, g_ref, sh_ref, sc_ref, w_ref, wf_ref, bf_ref, gq_ref, gk_ref, wc_ref, p_ref,
                     p0_ref, p1_ref,
                     ya_ref, u_ref, q_ref, k_ref, v_ref, lf_ref, *, nh, cc):
    gb, u, qn, kn, v, lf = _even_project(x_ref, g_ref, sh_ref, sc_ref, w_ref, wf_ref, bf_ref,
                                         gq_ref, gk_ref, p_ref, cc, nh)
    conv = wc_ref[0:1, :] * p0_ref[0] + wc_ref[1:2, :] * p1_ref[0] + wc_ref[2:3, :] * u
    ya_ref[0] = (gb * conv).astype(BF16)
    u_ref[0] = u
    q_ref[0] = qn
    k_ref[0] = kn
    v_ref[0] = v
    lf_ref[0] = lf


def _even_consts(nh, hd, tm):
    cc = nh * hd
    p = np.kron(np.eye(nh), np.ones((hd, hd)))
    tri = np.tril(np.ones((tm, tm)))
    eq = np.zeros((cc + 3 * LANES, nh * LANES))
    ek = np.zeros((cc + 3 * LANES, nh * LANES))
    oq = np.zeros((1, nh * LANES))
    ok = np.zeros((1, nh * LANES))
    for h in range(nh):
        for d in range(hd):
            eq[h * hd + d, h * LANES + d] = 1.0
            ek[h * hd + d, h * LANES + d] = 1.0
        for part in range(3):
            eq[cc + part * LANES + h, h * LANES + hd + part] = 1.0
            ek[cc + part * LANES + h, h * LANES + hd + 3 + part] = -1.0
            oq[0, h * LANES + hd + 3 + part] = 1.0
            ok[0, h * LANES + hd + part] = 1.0
    bf = lambda a: jnp.asarray(a, BF16)
    return bf(p), bf(tri), bf(eq), bf(ek), jnp.asarray(oq, F32), jnp.asarray(ok, F32)


def _even_weights(w_in, b_f, g_q, g_k, nh, hd):
    d = w_in.shape[0]
    cc = nh * hd
    w_main = w_in[:, :6 * cc].astype(BF16)
    w_f = jnp.zeros((d, LANES), BF16).at[:, :nh].set(w_in[:, 6 * cc:].astype(BF16))
    bf = jnp.zeros((1, LANES), F32).at[0, :nh].set(b_f)
    gq = (jnp.tile(g_q, nh) * (hd ** -0.5)).reshape(1, cc)
    gk = jnp.tile(g_k, nh).reshape(1, cc)
    return w_main, w_f, bf, gq, gk


def _full(shape):
    nd = len(shape)
    return pl.BlockSpec(shape, lambda *_: (0,) * nd)


def _even_in_seq(x, gain, mod, wts, w_conv, nh, hd, tm):
    g, l, d = x.shape
    cc = nh * hd
    w_main, w_f, bf, gq, gk = wts
    p, tri, eq, ek, oq, ok = _even_consts(nh, hd, tm)
    r = mod.shape[1]
    row = lambda b, i: (b, i, 0)
    in_specs = [pl.BlockSpec((1, tm, d), row), _full((1, d)),
                pl.BlockSpec((1, r, d), lambda b, i: (b, 0, 0)), pl.BlockSpec((1, r, d), lambda b, i: (b, 0, 1)),
                _full(w_main.shape), _full(w_f.shape), _full(bf.shape), _full(gq.shape), _full(gk.shape),
                _full(w_conv.shape), _full(p.shape), _full(tri.shape), _full(eq.shape), _full(ek.shape),
                _full(oq.shape), _full(ok.shape)]
    out_shape = [jax.ShapeDtypeStruct((g, l, cc), BF16),
                 jax.ShapeDtypeStruct((g, 8, cc), F32),
                 jax.ShapeDtypeStruct((g, l, nh * LANES), BF16),
                 jax.ShapeDtypeStruct((g, l, nh * LANES), BF16),
                 jax.ShapeDtypeStruct((g, l, cc), BF16),
                 jax.ShapeDtypeStruct((g, l, cc), F32),
                 jax.ShapeDtypeStruct((g, l, cc), F32),
                 jax.ShapeDtypeStruct((g, l, nh), F32)]
    out_specs = [pl.BlockSpec((1, tm, cc), row), pl.BlockSpec((1, 8, cc), lambda b, i: (b, 0, 0)),
                 pl.BlockSpec((1, tm, nh * LANES), row), pl.BlockSpec((1, tm, nh * LANES), row),
                 pl.BlockSpec((1, tm, cc), row), pl.BlockSpec((1, tm, cc), row), pl.BlockSpec((1, tm, cc), row),
                 pl.BlockSpec((1, tm, nh), row)]
    return pl.pallas_call(
        functools.partial(_even_seq_kernel, nh=nh, cc=cc, tm=tm),
        grid=(g, l // tm), in_specs=in_specs, out_specs=out_specs, out_shape=out_shape,
        scratch_shapes=[pltpu.VMEM((tm + 8, cc), F32), pltpu.VMEM((1, LANES), F32)],
        compiler_params=_cparams("arbitrary", "arbitrary"),
        name="even_in_seq",
    )(x, gain, mod, mod, w_main, w_f, bf, gq, gk, w_conv, p, tri, eq, ek, oq, ok)


def _even_in_dec(x, gain, mod, wts, w_conv, prev0, prev1, nh, hd):
    g, l, d = x.shape
    cc = nh * hd
    w_main, w_f, bf, gq, gk = wts
    p = _even_consts(nh, hd, 8)[0]
    whole = lambda s: pl.BlockSpec(s, lambda i: (0, 0, 0))
    in_specs = [whole((1, l, d)), pl.BlockSpec((1, d), lambda i: (0, 0)),
                pl.BlockSpec((1, l, d), lambda i: (0, 0, 0)), pl.BlockSpec((1, l, d), lambda i: (0, 0, 1)),
                _full(w_main.shape), _full(w_f.shape), _full(bf.shape), _full(gq.shape), _full(gk.shape),
                _full(w_conv.shape), _full(p.shape), whole((1, l, cc)), whole((1, l, cc))]
    out_shape = [jax.ShapeDtypeStruct((1, l, cc), BF16)] + [jax.ShapeDtypeStruct((1, l, cc), F32)] * 4 \
        + [jax.ShapeDtypeStruct((1, l, LANES), F32)]
    out_specs = [whole((1, l, cc))] * 5 + [whole((1, l, LANES))]
    return pl.pallas_call(
        functools.partial(_even_dec_kernel, nh=nh, cc=cc),
        grid=(1,), in_specs=in_specs, out_specs=out_specs, out_shape=out_shape,
        compiler_params=_cparams("arbitrary"),
        name="even_in_dec",
    )(x, gain, mod, mod, w_main, w_f, bf, gq, gk, w_conv, p, prev0, prev1)


def _attn_kernel(qp_ref, kp_ref, vb_ref, o_ref, m_sc, l_sc, acc_sc, *, nh, hd, tq):
    i = pl.program_id(1)
    j = pl.program_id(2)

    @pl.when(j == 0)
    def _():
        m_sc[...] = jnp.full(m_sc.shape, NEG, F32)
        l_sc[...] = jnp.zeros_like(l_sc)
        acc_sc[...] = jnp.zeros_like(acc_sc)

    def step(diagonal):
        for h in range(nh):
            q = qp_ref[0, :, h * LANES:(h + 1) * LANES]
            k = kp_ref[0, :, h * LANES:(h + 1) * LANES]
            s = _dot_nt(q, k)
            if diagonal:
                row = lax.broadcasted_iota(jnp.int32, s.shape, 0)
                col = lax.broadcasted_iota(jnp.int32, s.shape, 1)
                s = jnp.where(col <= row, s, NEG)
            m_old = m_sc[h]
            m_new = jnp.maximum(m_old, jnp.max(s, axis=1, keepdims=True))
            alpha = jnp.exp(m_old - m_new)
            p = jnp.exp(s - m_new)
            l_sc[h] = alpha * l_sc[h] + jnp.sum(p, axis=1, keepdims=True)
            pair = h // 2
            pv = _dot(p.astype(BF16), vb_ref[0, :, pair * LANES:(pair + 1) * LANES])
            acc_sc[h] = alpha * acc_sc[h] + pv
            m_sc[h] = m_new

    @pl.when(j < i)
    def _():
        step(False)

    @pl.when(j == i)
    def _():
        step(True)
        lane = lax.broadcasted_iota(jnp.int32, (tq, LANES), 1)
        for pair in range(nh // 2):
            o0 = acc_sc[2 * pair] / l_sc[2 * pair]
            o1 = acc_sc[2 * pair + 1] / l_sc[2 * pair + 1]
            o_ref[0, :, pair * LANES:(pair + 1) * LANES] = jnp.where(lane < hd, o0, o1).astype(BF16)


def _attn_prompt(qp, kp, vb, nh, hd, tq):
    g, l, _ = qp.shape
    cc = nh * hd
    nq = l // tq
    return pl.pallas_call(
        functools.partial(_attn_kernel, nh=nh, hd=hd, tq=tq),
        grid=(g, nq, nq),
        in_specs=[pl.BlockSpec((1, tq, nh * LANES), lambda b, i, j: (b, i, 0)),
                  pl.BlockSpec((1, tq, nh * LANES), lambda b, i, j: (b, jnp.minimum(i, j), 0)),
                  pl.BlockSpec((1, tq, cc), lambda b, i, j: (b, jnp.minimum(i, j), 0))],
        out_specs=pl.BlockSpec((1, tq, cc), lambda b, i, j: (b, i, 0)),
        out_shape=jax.ShapeDtypeStruct((g, l, cc), BF16),
        scratch_shapes=[pltpu.VMEM((nh, tq, 1), F32), pltpu.VMEM((nh, tq, 1), F32),
                        pltpu.VMEM((nh, tq, LANES), F32)],
        compiler_params=_cparams("arbitrary", "arbitrary", "arbitrary"),
        name="attn_prompt",
    )(qp, kp, vb)


def _attn_dec_kernel(pt_ref, qbd_ref, knew_ref, vnew_ref, lfnew_ref, uincl_ref, hm_ref, *rest, npg, nh):
    k_refs = rest[:npg]
    v_refs = rest[npg:2 * npg]
    lf_refs = rest[2 * npg:3 * npg]
    o_ref = rest[3 * npg]
    m_sc, l_sc, acc_sc, c_sc = rest[3 * npg + 1:]
    g = pl.program_id(1)

    @pl.when(g == 0)
    def _():
        m_sc[...] = jnp.full(m_sc.shape, NEG, F32)
        l_sc[...] = jnp.zeros_like(l_sc)
        acc_sc[...] = jnp.zeros_like(acc_sc)
        c_sc[...] = jnp.zeros_like(c_sc)

    qbd = qbd_ref[0]
    zpad = jnp.zeros((HEAD_ROWS - nh, LANES), F32)
    for j in range(npg):
        kb = k_refs[j][0, 0].astype(BF16)
        s = _dot_nt(qbd, kb)
        hi, mid, lo = _split3(lf_refs[j][0, 0])
        parts = jnp.concatenate([hi.astype(F32), mid.astype(F32), lo.astype(F32),
                                 jnp.zeros((nh, LANES), F32)], axis=0).astype(BF16)
        c3 = _dot(parts, uincl_ref[...])
        f = c_sc[...] + (c3[0:nh] + c3[nh:2 * nh] + c3[2 * nh:3 * nh])
        c_sc[...] = f[:, LANES - 1:LANES]
        s = s - jnp.concatenate([f, zpad], axis=0)
        m_old = m_sc[...]
        m_new = jnp.maximum(m_old, jnp.max(s, axis=1, keepdims=True))
        alpha = jnp.exp(m_old - m_new)
        p = jnp.exp(s - m_new)
        l_sc[...] = alpha * l_sc[...] + jnp.sum(p, axis=1, keepdims=True)
        acc_sc[...] = alpha * acc_sc[...] + _dot(p.astype(BF16), v_refs[j][0, 0].astype(BF16))
        m_sc[...] = m_new

    @pl.when(g == pl.num_programs(1) - 1)
    def _():
        f_new = jnp.concatenate([c_sc[...] + lfnew_ref[0], jnp.zeros((HEAD_ROWS - nh, 1), F32)], axis=0)
        s_new = jnp.sum(qbd.astype(F32) * knew_ref[0], axis=1, keepdims=True) - f_new
        m_old = m_sc[...]
        m_new = jnp.maximum(m_old, s_new)
        alpha = jnp.exp(m_old - m_new)
        p_new = jnp.exp(s_new - m_new)
        l_fin = alpha * l_sc[...] + p_new
        acc = alpha * acc_sc[...] + p_new * vnew_ref[0]
        o_ref[0] = jnp.sum((acc / l_fin) * hm_ref[...], axis=0, keepdims=True)


def _attn_decode(page_table, q, knew, vnew, lfnew, cache_k4, cache_v4, cache_lft, e, nh, hd):
    n_seq, n_pages = page_table.shape
    cc = nh * hd
    page = cache_k4.shape[2]
    assert page == LANES and n_pages % PAGES_PER_STEP == 0
    npg = PAGES_PER_STEP
    headmask = np.zeros((HEAD_ROWS, cc), np.float32)
    for h in range(nh):
        headmask[h, h * hd:(h + 1) * hd] = 1.0
    hm = jnp.asarray(headmask)
    qbd = (q[:, None, :] * hm[None]).astype(BF16)
    uincl = jnp.asarray(np.triu(np.ones((page, page))), BF16)

    def pg(j):
        return lambda b, g, pt: (pt[b, g * npg + j], e, 0, 0)

    per_seq = lambda b, g, pt: (b, 0, 0)
    const2 = lambda b, g, pt: (0, 0)
    in_specs = [pl.BlockSpec((1, HEAD_ROWS, cc), per_seq), pl.BlockSpec((1, 1, cc), per_seq),
                pl.BlockSpec((1, 1, cc), per_seq), pl.BlockSpec((1, nh, 1), per_seq),
                pl.BlockSpec((page, page), const2), pl.BlockSpec((HEAD_ROWS, cc), const2)]
    in_specs += [pl.BlockSpec((1, 1, page, cc), pg(j)) for j in range(npg)]
    in_specs += [pl.BlockSpec((1, 1, page, cc), pg(j)) for j in range(npg)]
    in_specs += [pl.BlockSpec((1, 1, nh, page), pg(j)) for j in range(npg)]
    grid_spec = pltpu.PrefetchScalarGridSpec(
        num_scalar_prefetch=1, grid=(n_seq, n_pages // npg), in_specs=in_specs,
        out_specs=pl.BlockSpec((1, 1, cc), per_seq),
        scratch_shapes=[pltpu.VMEM((HEAD_ROWS, 1), F32), pltpu.VMEM((HEAD_ROWS, 1), F32),
                        pltpu.VMEM((HEAD_ROWS, cc), F32), pltpu.VMEM((nh, 1), F32)])
    return pl.pallas_call(
        functools.partial(_attn_dec_kernel, npg=npg, nh=nh),
        grid_spec=grid_spec,
        out_shape=jax.ShapeDtypeStruct((n_seq, 1, cc), F32),
        compiler_params=_cparams("arbitrary", "arbitrary"),
        name="attn_decode",
    )(page_table, qbd, knew.reshape(n_seq, 1, cc), vnew.reshape(n_seq, 1, cc), lfnew.reshape(n_seq, nh, 1),
      uincl, hm, *([cache_k4] * npg), *([cache_v4] * npg), *([cache_lft] * npg))


def _odd_in_kernel(x_ref, g_ref, sh_ref, sc_ref, w_ref, wgd_ref, wgk_ref, bgk_ref,
                   q_ref, k_ref, v_ref, sr_ref, gl_ref, *, qk, vv, dk):
    hb = _norm_mod(x_ref[0], g_ref[...], sh_ref[0], sc_ref[0]).astype(BF16)
    q_ref[0] = _dot(hb, w_ref[:, 0:qk]) * (dk ** -0.5)
    k_ref[0] = _dot(hb, w_ref[:, qk:2 * qk])
    v_ref[0] = _dot(hb, w_ref[:, 2 * qk:2 * qk + vv]).astype(BF16)
    sr_ref[0] = _silu(_dot(hb, w_ref[:, 2 * qk + vv:2 * qk + 2 * vv]))
    gd = _dot(hb, wgd_ref[...]).astype(BF16)
    gl_ref[0] = _log_sigmoid(_dot(gd, wgk_ref[...]) + bgk_ref[...]) * (1.0 / GLA_TAU)


def _odd_weights(w_in, w_gk, b_gk, qk, vv):
    d = w_in.shape[0]
    rank = w_gk.shape[0]
    w_main = w_in[:, :2 * qk + 2 * vv].astype(BF16)
    w_gd = jnp.zeros((d, LANES), BF16).at[:, :rank].set(w_in[:, 2 * qk + 2 * vv:].astype(BF16))
    wgk = jnp.zeros((LANES, qk), BF16).at[:rank].set(w_gk.astype(BF16))
    return w_main, w_gd, wgk, b_gk.reshape(1, qk)


def _odd_in(x, gain, mod, wts, qk, vv, dk, tm):
    g, l, d = x.shape
    w_main, w_gd, wgk, bgk = wts
    r = mod.shape[1]
    row = lambda b, i: (b, i, 0)
    in_specs = [pl.BlockSpec((1, tm, d), row), _full((1, d)),
                pl.BlockSpec((1, r, d), lambda b, i: (b, 0, 0)), pl.BlockSpec((1, r, d), lambda b, i: (b, 0, 1)),
                _full(w_main.shape), _full(w_gd.shape), _full(wgk.shape), _full(bgk.shape)]
    out_shape = [jax.ShapeDtypeStruct((g, l, qk), F32), jax.ShapeDtypeStruct((g, l, qk), F32),
                 jax.ShapeDtypeStruct((g, l, vv), BF16), jax.ShapeDtypeStruct((g, l, vv), F32),
                 jax.ShapeDtypeStruct((g, l, qk), F32)]
    out_specs = [pl.BlockSpec((1, tm, qk), row), pl.BlockSpec((1, tm, qk), row), pl.BlockSpec((1, tm, vv), row),
                 pl.BlockSpec((1, tm, vv), row), pl.BlockSpec((1, tm, qk), row)]
    return pl.pallas_call(
        functools.partial(_odd_in_kernel, qk=qk, vv=vv, dk=dk),
        grid=(g, l // tm), in_specs=in_specs, out_specs=out_specs, out_shape=out_shape,
        compiler_params=_cparams("arbitrary", "arbitrary"),
        name="odd_in",
    )(x, gain, mod, mod, w_main, w_gd, wgk, bgk)


def _gla_seq_kernel(q_ref, k_ref, g_ref, v_ref, sr_ref, go_ref, tri_ref, o_ref, st_ref, s_sc, *, nh, dk, dv, tt):
    t = pl.program_id(1)
    c = GLA_CHUNK

    @pl.when(t == 0)
    def _():
        s_sc[...] = jnp.zeros_like(s_sc)

    qk = nh * dk
    b_all = _sum3(_dot(tri_ref[...], jnp.concatenate(_split3(g_ref[0]), axis=1)), qk)
    row = lax.broadcasted_iota(jnp.int32, (c, c), 0)
    col = lax.broadcasted_iota(jnp.int32, (c, c), 1)
    causal = col <= row
    for n in range(tt // c):
        rows = slice(n * c, (n + 1) * c)
        b = b_all[rows]
        b_last = b[c - 1:c, :]
        q = q_ref[0, rows, :]
        k = k_ref[0, rows, :]
        q_e = (q * jnp.exp(b)).astype(BF16)
        k_e = (k * jnp.exp(-b)).astype(BF16)
        k_d = (k * jnp.exp(b_last - b)).astype(BF16)
        decay = jnp.exp(b_last)
        for h in range(nh):
            kc = slice(h * dk, (h + 1) * dk)
            vc = slice(h * dv, (h + 1) * dv)
            v = v_ref[0, rows, vc]
            a = jnp.where(causal, _dot_nt(q_e[:, kc], k_e[:, kc]), 0.0)
            s_t = s_sc[h]
            o = _dot(a.astype(BF16), v) + _dot_nt(q_e[:, kc], s_t.astype(BF16))
            s_sc[h] = s_t * decay[:, kc] + _dot_tn(v, k_d[:, kc])
            y = o * lax.rsqrt(jnp.mean(o * o, axis=-1, keepdims=True) + EPS) * go_ref[...]
            o_ref[0, rows, vc] = (y * sr_ref[0, rows, vc]).astype(BF16)
    st_ref[0] = s_sc[...]


def _gla_seq(q, k, gl, v, sr, g_o, nh, dk, dv, tt):
    g, l, qk = q.shape
    vv = nh * dv
    tri = jnp.asarray(np.kron(np.eye(tt // GLA_CHUNK), np.tril(np.ones((GLA_CHUNK, GLA_CHUNK)))), BF16)
    row = lambda b, i: (b, i, 0)
    return pl.pallas_call(
        functools.partial(_gla_seq_kernel, nh=nh, dk=dk, dv=dv, tt=tt),
        grid=(g, l // tt),
        in_specs=[pl.BlockSpec((1, tt, qk), row), pl.BlockSpec((1, tt, qk), row), pl.BlockSpec((1, tt, qk), row),
                  pl.BlockSpec((1, tt, vv), row), pl.BlockSpec((1, tt, vv), row), _full((1, dv)), _full((tt, tt))],
        out_specs=[pl.BlockSpec((1, tt, vv), row), pl.BlockSpec((1, nh, dv, dk), lambda b, i: (b, 0, 0, 0))],
        out_shape=[jax.ShapeDtypeStruct((g, l, vv), BF16), jax.ShapeDtypeStruct((g, nh, dv, dk), F32)],
        scratch_shapes=[pltpu.VMEM((nh, dv, dk), F32)],
        compiler_params=_cparams("arbitrary", "arbitrary"),
        name="gla_seq",
    )(q, k, gl, v, sr, g_o.reshape(1, dv), tri)


def _gla_dec_kernel(q_ref, k_ref, g_ref, v_ref, sr_ref, go_ref, s0_ref, o_ref, s_ref, *, nh, dk, dv):
    def column(r):
        return jnp.broadcast_to(r, (dk, dk)).T

    for h in range(nh):
        kc = slice(h * dk, (h + 1) * dk)
        dcol = jnp.exp(column(g_ref[0, :, kc]))
        kcol = column(k_ref[0, :, kc])
        qcol = column(q_ref[0, :, kc])
        halves = []
        for half in range(dv // dk):
            lanes = slice(half * dk, (half + 1) * dk)
            v = v_ref[0, :, h * dv + half * dk:h * dv + (half + 1) * dk].astype(F32)
            s_new = dcol * s0_ref[0, 0, h, :, lanes] + kcol * v
            s_ref[0, h, :, lanes] = s_new
            halves.append(jnp.sum(qcol * s_new, axis=0, keepdims=True))
        o = jnp.concatenate(halves, axis=1)
        y = o * lax.rsqrt(jnp.mean(o * o, axis=-1, keepdims=True) + EPS) * go_ref[...]
        vc = slice(h * dv, (h + 1) * dv)
        o_ref[0, :, vc] = y * sr_ref[0, :, vc]


def _gla_dec(q, k, gl, v, sr, g_o, state_gla, o_idx, nh, dk, dv):
    n_seq = q.shape[0]
    qk, vv = nh * dk, nh * dv
    r3 = lambda a: a.reshape(n_seq, 1, a.shape[-1])
    per_seq = lambda b: (b, 0, 0)
    return pl.pallas_call(
        functools.partial(_gla_dec_kernel, nh=nh, dk=dk, dv=dv),
        grid=(n_seq,),
        in_specs=[pl.BlockSpec((1, 1, qk), per_seq), pl.BlockSpec((1, 1, qk), per_seq),
                  pl.BlockSpec((1, 1, qk), per_seq), pl.BlockSpec((1, 1, vv), per_seq),
                  pl.BlockSpec((1, 1, vv), per_seq), pl.BlockSpec((1, dv), lambda b: (0, 0)),
                  pl.BlockSpec((1, 1, nh, dk, dv), lambda b: (b, o_idx, 0, 0, 0))],
        out_specs=[pl.BlockSpec((1, 1, vv), per_seq), pl.BlockSpec((1, nh, dk, dv), lambda b: (b, 0, 0, 0))],
        out_shape=[jax.ShapeDtypeStruct((n_seq, 1, vv), F32), jax.ShapeDtypeStruct((n_seq, nh, dk, dv), F32)],
        compiler_params=_cparams("arbitrary"),
        name="gla_dec",
    )(r3(q), r3(k), r3(gl), r3(v), r3(sr), g_o.reshape(1, dv), state_gla)


def _post_kernel(x_ref, a1_ref, a2_ref, wo_ref, gtm_ref, gf_ref, shf_ref, scf_ref, gtf_ref, w1_ref, w2_ref,
                 o_ref, x1_sc, h_sc, acc_sc, *, ka):
    f = pl.program_id(2)

    @pl.when(f == 0)
    def _():
        m = _dot(a1_ref[0], wo_ref[0:ka, :]) + _dot(a2_ref[0], wo_ref[ka:2 * ka, :])
        x1 = x_ref[0] + gtm_ref[0] * m
        x1_sc[...] = x1
        h_sc[...] = _norm_mod(x1, gf_ref[...], shf_ref[0], scf_ref[0]).astype(BF16)
        acc_sc[...] = jnp.zeros_like(acc_sc)

    t = jnp.maximum(_dot(h_sc[...], w1_ref[...]), 0.0)
    acc_sc[...] += _dot((t * t).astype(BF16), w2_ref[...])

    @pl.when(f == pl.num_programs(2) - 1)
    def _():
        o_ref[0] = x1_sc[...] + gtf_ref[0] * acc_sc[...]


def _post(x, a1, a1_col, a2, a2_col, w_out, gain_f, mod, w1, w2, tm, tf):
    g, l, d = x.shape
    ka = w_out.shape[0] // 2
    dff = w1.shape[1]
    r = mod.shape[1]
    row = lambda b, i, f: (b, i, 0)
    modc = lambda c: pl.BlockSpec((1, r, d), lambda b, i, f: (b, 0, c))
    const = lambda s: pl.BlockSpec(s, lambda b, i, f: (0, 0))
    return pl.pallas_call(
        functools.partial(_post_kernel, ka=ka),
        grid=(g, l // tm, dff // tf),
        in_specs=[pl.BlockSpec((1, tm, d), row),
                  pl.BlockSpec((1, tm, ka), lambda b, i, f: (b, i, a1_col)),
                  pl.BlockSpec((1, tm, ka), lambda b, i, f: (b, i, a2_col)),
                  const(w_out.shape), modc(2), const((1, d)), modc(3), modc(4), modc(5),
                  pl.BlockSpec((d, tf), lambda b, i, f: (0, f)), pl.BlockSpec((tf, d), lambda b, i, f: (f, 0))],
        out_specs=pl.BlockSpec((1, tm, d), row),
        out_shape=jax.ShapeDtypeStruct((g, l, d), F32),
        scratch_shapes=[pltpu.VMEM((tm, d), F32), pltpu.VMEM((tm, d), BF16), pltpu.VMEM((tm, d), F32)],
        compiler_params=_cparams("arbitrary", "arbitrary", "arbitrary"),
        name="post_mlp",
    )(x, a1, a2, w_out, mod, gain_f, mod, mod, mod, w1, w2)


def kernel(x_prompt, x_sample, c_prompt, c_sample, cache_k, cache_v, cache_logf, page_table, state_conv, state_gla,
           g_mix, g_ffn, w_ada, b_ada, w_in_even, b_f, g_q, g_k, w_conv, w_out_even, w_in_odd, w_gk, b_gk, g_o,
           w_out_odd, w_ff1, w_ff2):
    bp, seq, d = x_prompt.shape
    n_seq = x_sample.shape[0]
    depth = w_ada.shape[0]
    n_even, nh_fox = b_f.shape
    hd_fox = g_q.shape[1]
    cc = nh_fox * hd_fox
    _, n_odd, nh_gla, dk, dv = state_gla.shape
    qk, vv = nh_gla * dk, nh_gla * dv
    pool, _, page, _, _ = cache_k.shape

    mod_all = _adaln(jnp.concatenate([c_prompt, c_sample], axis=0), w_ada, b_ada)

    cache_k4 = cache_k.reshape(pool, n_even, page, cc)
    cache_v4 = cache_v.reshape(pool, n_even, page, cc)
    cache_lft = jnp.swapaxes(cache_logf, 2, 3)

    xp = x_prompt
    xs = x_sample.reshape(1, n_seq, d)
    fox_p, fox_s, conv_p, conv_s, gla_p, gla_s = [], [], [], [], [], []
    for li in range(depth):
        mod_p = mod_all[li, :bp].reshape(bp, 1, 6 * d)
        mod_s = mod_all[li, bp:].reshape(1, n_seq, 6 * d)
        gm = g_mix[li].reshape(1, d)
        gf = g_ffn[li].reshape(1, d)
        w1 = w_ff1[li].astype(BF16)
        w2 = w_ff2[li].astype(BF16)
        if li % 2 == 0:
            e = li // 2
            wts = _even_weights(w_in_even[e], b_f[e], g_q[e], g_k[e], nh_fox, hd_fox)
            w_out = w_out_even[e].astype(BF16)
            ya, utail, qp, kp, vb, k_rows, v_rows, lf_rows = _even_in_seq(
                xp, gm, mod_p, wts, w_conv[e], nh_fox, hd_fox, tm=256)
            o = _attn_prompt(qp, kp, vb, nh_fox, hd_fox, tq=512)
            xp = _post(xp, ya, 0, o, 0, w_out, gf, mod_p, w1, w2, tm=512, tf=1024)
            fox_p.append((k_rows, v_rows, lf_rows))
            conv_p.append(utail[:, 6:8])
            prev0 = state_conv[:, e, 0].reshape(1, n_seq, cc)
            prev1 = state_conv[:, e, 1].reshape(1, n_seq, cc)
            ya, u, qn, kn, v, lfpad = _even_in_dec(xs, gm, mod_s, wts, w_conv[e], prev0, prev1, nh_fox, hd_fox)
            lf_new = lfpad[0, :, :nh_fox]
            o = _attn_decode(page_table, qn[0], kn[0], v[0], lf_new, cache_k4, cache_v4, cache_lft, e,
                             nh_fox, hd_fox)
            o = o.reshape(1, n_seq, cc).astype(BF16)
            xs = _post(xs, ya, 0, o, 0, w_out, gf, mod_s, w1, w2, tm=n_seq, tf=1024)
            fox_s.append((kn[0], v[0], lf_new))
            conv_s.append(jnp.stack([prev1[0], u[0]], axis=1))
        else:
            oi = li // 2
            wts = _odd_weights(w_in_odd[oi], w_gk[oi], b_gk[oi], qk, vv)
            w_out = w_out_odd[oi].astype(BF16)
            q, k, v, sr, gl = _odd_in(xp, gm, mod_p, wts, qk, vv, dk, tm=256)
            og, s_t = _gla_seq(q, k, gl, v, sr, g_o[oi], nh_gla, dk, dv, tt=256)
            xp = _post(xp, og, 0, og, 1, w_out, gf, mod_p, w1, w2, tm=512, tf=1024)
            gla_p.append(jnp.swapaxes(s_t, 2, 3))
            q, k, v, sr, gl = _odd_in(xs, gm, mod_s, wts, qk, vv, dk, tm=n_seq)
            og, s_new = _gla_dec(q[0], k[0], gl[0], v[0].astype(F32), sr[0], g_o[oi], state_gla, oi,
                                 nh_gla, dk, dv)
            og = og.reshape(1, n_seq, vv).astype(BF16)
            xs = _post(xs, og, 0, og, 1, w_out, gf, mod_s, w1, w2, tm=n_seq, tf=1024)
            gla_s.append(s_new)

    heads = lambda a, n: a.reshape(a.shape[0], n, nh_fox, hd_fox)
    k_p = jnp.stack([heads(r[0], seq) for r in fox_p], axis=1)
    v_p = jnp.stack([heads(r[1], seq) for r in fox_p], axis=1)
    lf_p = jnp.stack([r[2] for r in fox_p], axis=1)
    k_s = jnp.stack([heads(r[0], 1) for r in fox_s], axis=1)
    v_s = jnp.stack([heads(r[1], 1) for r in fox_s], axis=1)
    lf_s = jnp.stack([r[2].reshape(n_seq, 1, nh_fox) for r in fox_s], axis=1)
    return (xp, xs.reshape(n_seq, 1, d), k_p, v_p, lf_p, k_s, v_s, lf_s,
            jnp.stack(conv_p, axis=1), jnp.stack(conv_s, axis=1),
            jnp.stack(gla_p, axis=1), jnp.stack(gla_s, axis=1))
```

```python
import functools

import numpy as np
import jax
import jax.numpy as jnp
from jax import lax
from jax.experimental import pallas as pl
from jax.experimental.pallas import tpu as pltpu

F32 = jnp.float32
BF16 = jnp.bfloat16

EPS = 1e-6
GLA_TAU = 16.0
GLA_CHUNK = 64
LANES = 128
VMEM_LIMIT = 56 * 1024 * 1024
NEG = -1e30
LOG2E = 1.4426950408889634
PAGES_PER_STEP = 8
ONES_ROWS = 16


def _cparams(*sem):
    return pltpu.CompilerParams(dimension_semantics=sem, vmem_limit_bytes=VMEM_LIMIT)


def _silu(x):
    return x / (1.0 + jnp.exp(-x))


def _log_sigmoid(z):
    return jnp.minimum(z, 0.0) - jnp.log1p(jnp.exp(-jnp.abs(z)))


def _norm_mod(x, gain, shift, scale):
    y = x * lax.rsqrt(jnp.mean(x * x, axis=-1, keepdims=True) + EPS)
    return (y * gain) * (1.0 + scale) + shift


def _split3(a):
    hi = a.astype(BF16)
    r = a - hi.astype(F32)
    mid = r.astype(BF16)
    lo = (r - mid.astype(F32)).astype(BF16)
    return hi, mid, lo


def _dot(a, b):
    return jnp.dot(a, b, preferred_element_type=F32)


def _dot_nt(a, b):
    return lax.dot_general(a, b, (((1,), (1,)), ((), ())), preferred_element_type=F32)


def _dot_tn(a, b):
    return lax.dot_general(a, b, (((0,), (0,)), ((), ())), preferred_element_type=F32)


def _sum3(c, w):
    return c[:, 0:w] + c[:, w:2 * w] + c[:, 2 * w:3 * w]


def _adaln_kernel(c_ref, w_ref, b_ref, o_ref):
    a = _silu(c_ref[...]).astype(BF16)
    o_ref[0] = _dot(a, w_ref[0].astype(BF16)) + b_ref[0]


def _adaln(c_all, w_ada, b_ada):
    depth, d, n = w_ada.shape
    r = c_all.shape[0]
    tn = n // 4
    return pl.pallas_call(
        _adaln_kernel,
        grid=(depth, n // tn),
        in_specs=[pl.BlockSpec((r, d), lambda l, j: (0, 0)),
                  pl.BlockSpec((1, d, tn), lambda l, j: (l, 0, j)),
                  pl.BlockSpec((1, 1, tn), lambda l, j: (l, 0, j))],
        out_specs=pl.BlockSpec((1, r, tn), lambda l, j: (l, 0, j)),
        out_shape=jax.ShapeDtypeStruct((depth, r, n), F32),
        compiler_params=_cparams("arbitrary", "arbitrary"),
        name="adaln",
    )(c_all, w_ada, b_ada.reshape(depth, 1, n))


def _even_project(x_ref, g_ref, sh_ref, sc_ref, w_ref, wf_ref, bf_ref, gq_ref, gk_ref, p_ref, cc, nh):
    hb = _norm_mod(x_ref[0], g_ref[...], sh_ref[0], sc_ref[0]).astype(BF16)
    gb = _dot(hb, w_ref[:, 0:cc])
    u = _dot(hb, w_ref[:, cc:2 * cc]) * _dot(hb, w_ref[:, 2 * cc:3 * cc])
    q = _dot(hb, w_ref[:, 3 * cc:4 * cc])
    k = _dot(hb, w_ref[:, 4 * cc:5 * cc])
    v = _dot(hb, w_ref[:, 5 * cc:6 * cc])
    hd = cc // nh
    qms = _dot((q * q).astype(BF16), p_ref[...]) * (1.0 / hd)
    kms = _dot((k * k).astype(BF16), p_ref[...]) * (1.0 / hd)
    qn = (q * lax.rsqrt(qms + EPS)) * gq_ref[...]
    kn = (k * lax.rsqrt(kms + EPS)) * gk_ref[...]
    fz = _dot(hb, wf_ref[...]) + bf_ref[...]
    lane = lax.broadcasted_iota(jnp.int32, fz.shape, 1)
    lf = jnp.where(lane < nh, _log_sigmoid(fz), 0.0)
    return gb, u, qn, kn, v, lf


def _even_seq_kernel(x_ref, g_ref, sh_ref, sc_ref, w_ref, wf_ref, bf_ref, gq_ref, gk_ref, wc_ref, p_ref,
                     tri_ref, eq_ref, ek_ref, oq_ref, ok_ref,
                     ya_ref, ut_ref, qp_ref, kp_ref, va_ref, ko_ref, vo_ref, lf_ref,
                     ubuf, fcarry, *, nh, cc, tm):
    i = pl.program_id(1)
    gb, u, qn, kn, v, lf = _even_project(x_ref, g_ref, sh_ref, sc_ref, w_ref, wf_ref, bf_ref,
                                         gq_ref, gk_ref, p_ref, cc, nh)

    @pl.when(i == 0)
    def _():
        ubuf[0:8, :] = jnp.zeros((8, cc), F32)
        fcarry[...] = jnp.zeros_like(fcarry)

    ubuf[8:8 + tm, :] = u
    conv = wc_ref[0:1, :] * ubuf[6:6 + tm, :] + wc_ref[1:2, :] * ubuf[7:7 + tm, :] + wc_ref[2:3, :] * u
    ya_ref[0] = (gb * conv).astype(BF16)
    tail = ubuf[tm:tm + 8, :]
    ubuf[0:8, :] = tail
    ut_ref[0] = tail

    ko_ref[0] = kn
    vo_ref[0] = v
    lf_ref[0] = lf[:, 0:nh]
    vb = v.astype(BF16)
    for pair in range(nh // 2):
        va_ref[0, :, 2 * pair * LANES:(2 * pair + 1) * LANES] = vb[:, pair * LANES:(pair + 1) * LANES]
        va_ref[0, :, (2 * pair + 1) * LANES:(2 * pair + 2) * LANES] = jnp.ones((tm, LANES), BF16)

    c3 = _dot(tri_ref[...], jnp.concatenate(_split3(lf), axis=1))
    f = _sum3(c3, LANES) + fcarry[...]
    fcarry[...] = f[tm - 1:tm, :]
    f3 = jnp.concatenate(_split3(f * LOG2E), axis=1)

    qp_ref[0] = (_dot(jnp.concatenate([qn.astype(BF16), f3], axis=1), eq_ref[...]) + oq_ref[...]).astype(BF16)
    kp_ref[0] = (_dot(jnp.concatenate([kn.astype(BF16), f3], axis=1), ek_ref[...]) + ok_ref[...]).astype(BF16)


def _even_dec_kernel(x_ref, g_ref, sh_ref, sc_ref, w_ref, wf_ref, bf_ref, gq_ref, gk_ref, wc_ref, p_ref,
                     p0_ref, p1_ref,
                     ya_ref, u_ref, q_ref, k_ref, v_ref, lf_ref, *, nh, cc):
    gb, u, qn, kn, v, lf = _even_project(x_ref, g_ref, sh_ref, sc_ref, w_ref, wf_ref, bf_ref,
                                         gq_ref, gk_ref, p_ref, cc, nh)
    conv = wc_ref[0:1, :] * p0_ref[0] + wc_ref[1:2, :] * p1_ref[0] + wc_ref[2:3, :] * u
    ya_ref[0] = (gb * conv).astype(BF16)
    u_ref[0] = u
    q_ref[0] = qn
    k_ref[0] = kn
    v_ref[0] = v
    lf_ref[0] = lf


def _even_consts(nh, hd, tm):
    cc = nh * hd
    p = np.kron(np.eye(nh), np.ones((hd, hd)))
    tri = np.tril(np.ones((tm, tm)))
    eq = np.zeros((cc + 3 * LANES, nh * LANES))
    ek = np.zeros((cc + 3 * LANES, nh * LANES))
    oq = np.zeros((1, nh * LANES))
    ok = np.zeros((1, nh * LANES))
    for h in range(nh):
        for d in range(hd):
            eq[h * hd + d, h * LANES + d] = 1.0
            ek[h * hd + d, h * LANES + d] = 1.0
        for part in range(3):
            eq[cc + part * LANES + h, h * LANES + hd + part] = 1.0
            ek[cc + part * LANES + h, h * LANES + hd + 3 + part] = -1.0
            oq[0, h * LANES + hd + 3 + part] = 1.0
            ok[0, h * LANES + hd + part] = 1.0
    bf = lambda a: jnp.asarray(a, BF16)
    return bf(p), bf(tri), bf(eq), bf(ek), jnp.asarray(oq, F32), jnp.asarray(ok, F32)


def _even_weights(w_in, b_f, g_q, g_k, nh, hd):
    d = w_in.shape[0]
    cc = nh * hd
    w_main = w_in[:, :6 * cc].astype(BF16)
    w_f = jnp.zeros((d, LANES), BF16).at[:, :nh].set(w_in[:, 6 * cc:].astype(BF16))
    bf = jnp.zeros((1, LANES), F32).at[0, :nh].set(b_f)
    gq = (jnp.tile(g_q, nh) * (hd ** -0.5)).reshape(1, cc)
    gk = jnp.tile(g_k, nh).reshape(1, cc)
    return w_main, w_f, bf, gq, gk


def _full(shape):
    nd = len(shape)
    return pl.BlockSpec(shape, lambda *_: (0,) * nd)


def _even_in_seq(x, gain, mod, wts, w_conv, nh, hd, tm):
    g, l, d = x.shape
    cc = nh * hd
    w_main, w_f, bf, gq, gk = wts
    gq = gq * LOG2E
    p, tri, eq, ek, oq, ok = _even_consts(nh, hd, tm)
    r = mod.shape[1]
    row = lambda b, i: (b, i, 0)
    in_specs = [pl.BlockSpec((1, tm, d), row), _full((1, d)),
                pl.BlockSpec((1, r, d), lambda b, i: (b, 0, 0)), pl.BlockSpec((1, r, d), lambda b, i: (b, 0, 1)),
                _full(w_main.shape), _full(w_f.shape), _full(bf.shape), _full(gq.shape), _full(gk.shape),
                _full(w_conv.shape), _full(p.shape), _full(tri.shape), _full(eq.shape), _full(ek.shape),
                _full(oq.shape), _full(ok.shape)]
    out_shape = [jax.ShapeDtypeStruct((g, l, cc), BF16),
                 jax.ShapeDtypeStruct((g, 8, cc), F32),
                 jax.ShapeDtypeStruct((g, l, nh * LANES), BF16),
                 jax.ShapeDtypeStruct((g, l, nh * LANES), BF16),
                 jax.ShapeDtypeStruct((g, l, nh * LANES), BF16),
                 jax.ShapeDtypeStruct((g, l, cc), F32),
                 jax.ShapeDtypeStruct((g, l, cc), F32),
                 jax.ShapeDtypeStruct((g, l, nh), F32)]
    out_specs = [pl.BlockSpec((1, tm, cc), row), pl.BlockSpec((1, 8, cc), lambda b, i: (b, 0, 0)),
                 pl.BlockSpec((1, tm, nh * LANES), row), pl.BlockSpec((1, tm, nh * LANES), row),
                 pl.BlockSpec((1, tm, nh * LANES), row), pl.BlockSpec((1, tm, cc), row),
                 pl.BlockSpec((1, tm, cc), row), pl.BlockSpec((1, tm, nh), row)]
    return pl.pallas_call(
        functools.partial(_even_seq_kernel, nh=nh, cc=cc, tm=tm),
        grid=(g, l // tm), in_specs=in_specs, out_specs=out_specs, out_shape=out_shape,
        scratch_shapes=[pltpu.VMEM((tm + 8, cc), F32), pltpu.VMEM((1, LANES), F32)],
        compiler_params=_cparams("arbitrary", "arbitrary"),
        name="even_in_seq",
    )(x, gain, mod, mod, w_main, w_f, bf, gq, gk, w_conv, p, tri, eq, ek, oq, ok)


def _even_in_dec(x, gain, mod, wts, w_conv, prev0, prev1, nh, hd):
    g, l, d = x.shape
    cc = nh * hd
    w_main, w_f, bf, gq, gk = wts
    p = _even_consts(nh, hd, 8)[0]
    whole = lambda s: pl.BlockSpec(s, lambda i: (0, 0, 0))
    in_specs = [whole((1, l, d)), pl.BlockSpec((1, d), lambda i: (0, 0)),
                pl.BlockSpec((1, l, d), lambda i: (0, 0, 0)), pl.BlockSpec((1, l, d), lambda i: (0, 0, 1)),
                _full(w_main.shape), _full(w_f.shape), _full(bf.shape), _full(gq.shape), _full(gk.shape),
                _full(w_conv.shape), _full(p.shape), whole((1, l, cc)), whole((1, l, cc))]
    out_shape = [jax.ShapeDtypeStruct((1, l, cc), BF16)] + [jax.ShapeDtypeStruct((1, l, cc), F32)] * 4 \
        + [jax.ShapeDtypeStruct((1, l, LANES), F32)]
    out_specs = [whole((1, l, cc))] * 5 + [whole((1, l, LANES))]
    return pl.pallas_call(
        functools.partial(_even_dec_kernel, nh=nh, cc=cc),
        grid=(1,), in_specs=in_specs, out_specs=out_specs, out_shape=out_shape,
        compiler_params=_cparams("arbitrary"),
        name="even_in_dec",
    )(x, gain, mod, mod, w_main, w_f, bf, gq, gk, w_conv, p, prev0, prev1)


def _attn_kernel(qp_ref, kp_ref, va_ref, o_ref, m_sc, l_sc, acc_sc, s_sc, p_sc, *, nh, hd, tq):
    i = pl.program_id(1)
    j = pl.program_id(2)
    reps = tq // LANES

    @pl.when(j == 0)
    def _():
        m_sc[...] = jnp.full(m_sc.shape, NEG, F32)
        l_sc[...] = jnp.zeros_like(l_sc)
        acc_sc[...] = jnp.zeros_like(acc_sc)

    def scores(h):
        s_sc[h % 2] = _dot_nt(qp_ref[0, :, h * LANES:(h + 1) * LANES], kp_ref[0, :, h * LANES:(h + 1) * LANES])

    def step(diagonal):
        scores(0)
        for h in range(nh):
            if h + 1 < nh:
                scores(h + 1)
            s = s_sc[h % 2]
            if diagonal:
                row = lax.broadcasted_iota(jnp.int32, s.shape, 0)
                col = lax.broadcasted_iota(jnp.int32, s.shape, 1)
                s = jnp.where(col <= row, s, NEG)
            m_old = m_sc[h]
            m_new = jnp.maximum(m_old, jnp.max(s, axis=1, keepdims=True))
            alpha = jnp.exp2(m_old - m_new)
            p_sc[h % 2] = jnp.exp2(s - jnp.concatenate([m_new] * reps, axis=1)).astype(BF16)
            pair = h // 2
            pv = _dot(p_sc[h % 2], va_ref[0, :, pair * 2 * LANES:(pair + 1) * 2 * LANES])
            acc_sc[h] = alpha * acc_sc[h] + pv[:, 0:LANES]
            l_sc[h] = alpha * l_sc[h] + pv[:, LANES:2 * LANES]
            m_sc[h] = m_new

    @pl.when(j < i)
    def _():
        step(False)

    @pl.when(j == i)
    def _():
        step(True)
        lane = lax.broadcasted_iota(jnp.int32, (tq, LANES), 1)
        for pair in range(nh // 2):
            o0 = acc_sc[2 * pair] / l_sc[2 * pair]
            o1 = acc_sc[2 * pair + 1] / l_sc[2 * pair + 1]
            o_ref[0, :, pair * LANES:(pair + 1) * LANES] = jnp.where(lane < hd, o0, o1).astype(BF16)


def _attn_prompt(qp, kp, va, nh, hd, tq):
    g, l, _ = qp.shape
    cc = nh * hd
    nq = l // tq
    return pl.pallas_call(
        functools.partial(_attn_kernel, nh=nh, hd=hd, tq=tq),
        grid=(g, nq, nq),
        in_specs=[pl.BlockSpec((1, tq, nh * LANES), lambda b, i, j: (b, i, 0)),
                  pl.BlockSpec((1, tq, nh * LANES), lambda b, i, j: (b, jnp.minimum(i, j), 0)),
                  pl.BlockSpec((1, tq, nh * LANES), lambda b, i, j: (b, jnp.minimum(i, j), 0))],
        out_specs=pl.BlockSpec((1, tq, cc), lambda b, i, j: (b, i, 0)),
        out_shape=jax.ShapeDtypeStruct((g, l, cc), BF16),
        scratch_shapes=[pltpu.VMEM((nh, tq, LANES), F32), pltpu.VMEM((nh, tq, LANES), F32),
                        pltpu.VMEM((nh, tq, LANES), F32), pltpu.VMEM((2, tq, tq), F32),
                        pltpu.VMEM((2, tq, tq), BF16)],
        compiler_params=_cparams("arbitrary", "arbitrary", "arbitrary"),
        name="attn_prompt",
    )(qp, kp, va)


def _attn_dec_kernel(pt_ref, qcol_ref, kncol_ref, vncol_ref, lfnew_ref, uincl_ref, ones_ref, *rest, npg, nh):
    k_refs = rest[:npg]
    v_refs = rest[npg:2 * npg]
    lf_refs = rest[2 * npg:3 * npg]
    o_ref = rest[3 * npg]
    m_sc, l_sc, acc_sc, c_sc = rest[3 * npg + 1:]
    g = pl.program_id(1)

    @pl.when(g == 0)
    def _():
        m_sc[...] = jnp.full(m_sc.shape, NEG, F32)
        l_sc[...] = jnp.zeros_like(l_sc)
        acc_sc[...] = jnp.zeros_like(acc_sc)
        c_sc[...] = jnp.zeros_like(c_sc)

    fs = []
    c = c_sc[...]
    for j in range(npg):
        hi, mid, lo = _split3(lf_refs[j][0, 0])
        parts = jnp.concatenate([hi.astype(F32), mid.astype(F32), lo.astype(F32),
                                 jnp.zeros((nh, LANES), F32)], axis=0).astype(BF16)
        c3 = _dot(parts, uincl_ref[...])
        f = c + (c3[0:nh] + c3[nh:2 * nh] + c3[2 * nh:3 * nh])
        c = f[:, LANES - 1:LANES]
        fs.append(f)
    c_sc[...] = c

    for h in range(nh):
        qc = qcol_ref[0, h]
        rows = [jnp.sum(qc * k_refs[j][0, 0, h], axis=0, keepdims=True) - fs[j][h:h + 1, :] for j in range(npg)]
        m_old = m_sc[h]
        m_new = jnp.maximum(m_old, jnp.max(functools.reduce(jnp.maximum, rows), axis=1, keepdims=True))
        alpha = jnp.exp(m_old - m_new)
        ps = [jnp.exp(r - m_new) for r in rows]
        l_sc[h] = alpha * l_sc[h] + jnp.sum(functools.reduce(jnp.add, ps), axis=1, keepdims=True)
        acc = alpha * acc_sc[h]
        for j in range(npg):
            acc = acc + ps[j] * v_refs[j][0, 0, h]
        acc_sc[h] = acc
        m_sc[h] = m_new

    @pl.when(g == pl.num_programs(1) - 1)
    def _():
        lane = lax.broadcasted_iota(jnp.int32, (1, LANES), 1)
        for h in range(nh):
            f_new = c_sc[h:h + 1, :] + lfnew_ref[0, h:h + 1, :]
            s_new = jnp.sum(qcol_ref[0, h] * kncol_ref[0, h], axis=0, keepdims=True) - f_new
            s_new = jnp.where(lane == 0, s_new, NEG)
            m_old = m_sc[h]
            m_new = jnp.maximum(m_old, jnp.max(s_new, axis=1, keepdims=True))
            alpha = jnp.exp(m_old - m_new)
            p_new = jnp.exp(s_new - m_new)
            l_fin = alpha * l_sc[h] + jnp.sum(p_new, axis=1, keepdims=True)
            acc = (alpha * acc_sc[h] + p_new * vncol_ref[0, h]) / l_fin
            hi, mid, lo = _split3(acc)
            o = _dot_nt(ones_ref[...], hi) + _dot_nt(ones_ref[...], mid) + _dot_nt(ones_ref[...], lo)
            o_ref[0, h:h + 1, :] = o[0:1, :]


def _attn_decode(page_table, q, knew, vnew, lfnew, cache_kt, cache_vt, cache_lft, e, nh, hd):
    n_seq, n_pages = page_table.shape
    page = cache_kt.shape[-1]
    assert page == LANES and n_pages % PAGES_PER_STEP == 0
    npg = PAGES_PER_STEP
    col = lambda a: jnp.broadcast_to(a.reshape(n_seq, nh, hd, 1), (n_seq, nh, hd, LANES))
    uincl = jnp.asarray(np.triu(np.ones((page, page))), BF16)
    ones = jnp.ones((ONES_ROWS, LANES), BF16)

    def pg(j, nd):
        return lambda b, g, pt: (pt[b, g * npg + j], e) + (0,) * nd

    per_seq4 = lambda b, g, pt: (b, 0, 0, 0)
    per_seq3 = lambda b, g, pt: (b, 0, 0)
    const2 = lambda b, g, pt: (0, 0)
    in_specs = [pl.BlockSpec((1, nh, hd, LANES), per_seq4), pl.BlockSpec((1, nh, hd, LANES), per_seq4),
                pl.BlockSpec((1, nh, hd, LANES), per_seq4), pl.BlockSpec((1, nh, 1), per_seq3),
                pl.BlockSpec((page, page), const2), pl.BlockSpec((ONES_ROWS, LANES), const2)]
    in_specs += [pl.BlockSpec((1, 1, nh, hd, page), pg(j, 3)) for j in range(npg)]
    in_specs += [pl.BlockSpec((1, 1, nh, hd, page), pg(j, 3)) for j in range(npg)]
    in_specs += [pl.BlockSpec((1, 1, nh, page), pg(j, 2)) for j in range(npg)]
    grid_spec = pltpu.PrefetchScalarGridSpec(
        num_scalar_prefetch=1, grid=(n_seq, n_pages // npg), in_specs=in_specs,
        out_specs=pl.BlockSpec((1, nh, hd), per_seq3),
        scratch_shapes=[pltpu.VMEM((nh, 1, LANES), F32), pltpu.VMEM((nh, 1, LANES), F32),
                        pltpu.VMEM((nh, hd, LANES), F32), pltpu.VMEM((nh, 1), F32)])
    o = pl.pallas_call(
        functools.partial(_attn_dec_kernel, npg=npg, nh=nh),
        grid_spec=grid_spec,
        out_shape=jax.ShapeDtypeStruct((n_seq, nh, hd), F32),
        compiler_params=_cparams("arbitrary", "arbitrary"),
        name="attn_decode",
    )(page_table, col(q), col(knew), col(vnew), lfnew.reshape(n_seq, nh, 1), uincl, ones,
      *([cache_kt] * npg), *([cache_vt] * npg), *([cache_lft] * npg))
    return o.reshape(n_seq, 1, nh * hd)


def _odd_in_kernel(x_ref, g_ref, sh_ref, sc_ref, w_ref, wgd_ref, wgk_ref, bgk_ref,
                   q_ref, k_ref, v_ref, sr_ref, gl_ref, *, qk, vv, dk):
    hb = _norm_mod(x_ref[0], g_ref[...], sh_ref[0], sc_ref[0]).astype(BF16)
    q_ref[0] = _dot(hb, w_ref[:, 0:qk]) * (dk ** -0.5)
    k_ref[0] = _dot(hb, w_ref[:, qk:2 * qk])
    v_ref[0] = _dot(hb, w_ref[:, 2 * qk:2 * qk + vv]).astype(BF16)
    sr_ref[0] = _silu(_dot(hb, w_ref[:, 2 * qk + vv:2 * qk + 2 * vv]))
    gd = _dot(hb, wgd_ref[...]).astype(BF16)
    gl_ref[0] = _log_sigmoid(_dot(gd, wgk_ref[...]) + bgk_ref[...]) * (1.0 / GLA_TAU)


def _odd_weights(w_in, w_gk, b_gk, qk, vv):
    d = w_in.shape[0]
    rank = w_gk.shape[0]
    w_main = w_in[:, :2 * qk + 2 * vv].astype(BF16)
    w_gd = jnp.zeros((d, LANES), BF16).at[:, :rank].set(w_in[:, 2 * qk + 2 * vv:].astype(BF16))
    wgk = jnp.zeros((LANES, qk), BF16).at[:rank].set(w_gk.astype(BF16))
    return w_main, w_gd, wgk, b_gk.reshape(1, qk)


def _odd_in(x, gain, mod, wts, qk, vv, dk, tm):
    g, l, d = x.shape
    w_main, w_gd, wgk, bgk = wts
    r = mod.shape[1]
    row = lambda b, i: (b, i, 0)
    in_specs = [pl.BlockSpec((1, tm, d), row), _full((1, d)),
                pl.BlockSpec((1, r, d), lambda b, i: (b, 0, 0)), pl.BlockSpec((1, r, d), lambda b, i: (b, 0, 1)),
                _full(w_main.shape), _full(w_gd.shape), _full(wgk.shape), _full(bgk.shape)]
    out_shape = [jax.ShapeDtypeStruct((g, l, qk), F32), jax.ShapeDtypeStruct((g, l, qk), F32),
                 jax.ShapeDtypeStruct((g, l, vv), BF16), jax.ShapeDtypeStruct((g, l, vv), F32),
                 jax.ShapeDtypeStruct((g, l, qk), F32)]
    out_specs = [pl.BlockSpec((1, tm, qk), row), pl.BlockSpec((1, tm, qk), row), pl.BlockSpec((1, tm, vv), row),
                 pl.BlockSpec((1, tm, vv), row), pl.BlockSpec((1, tm, qk), row)]
    return pl.pallas_call(
        functools.partial(_odd_in_kernel, qk=qk, vv=vv, dk=dk),
        grid=(g, l // tm), in_specs=in_specs, out_specs=out_specs, out_shape=out_shape,
        compiler_params=_cparams("arbitrary", "arbitrary"),
        name="odd_in",
    )(x, gain, mod, mod, w_main, w_gd, wgk, bgk)


def _gla_seq_kernel(q_ref, k_ref, g_ref, v_ref, sr_ref, go_ref, tri_ref, o_ref, st_ref, s_sc, *, nh, dk, dv, tt):
    t = pl.program_id(1)
    c = GLA_CHUNK

    @pl.when(t == 0)
    def _():
        s_sc[...] = jnp.zeros_like(s_sc)

    qk = nh * dk
    b_all = _sum3(_dot(tri_ref[...], jnp.concatenate(_split3(g_ref[0]), axis=1)), qk)
    row = lax.broadcasted_iota(jnp.int32, (c, c), 0)
    col = lax.broadcasted_iota(jnp.int32, (c, c), 1)
    causal = col <= row
    for n in range(tt // c):
        rows = slice(n * c, (n + 1) * c)
        b = b_all[rows]
        b_last = b[c - 1:c, :]
        q = q_ref[0, rows, :]
        k = k_ref[0, rows, :]
        q_e = (q * jnp.exp(b)).astype(BF16)
        k_e = (k * jnp.exp(-b)).astype(BF16)
        k_d = (k * jnp.exp(b_last - b)).astype(BF16)
        decay = jnp.exp(b_last)
        for h in range(nh):
            kc = slice(h * dk, (h + 1) * dk)
            vc = slice(h * dv, (h + 1) * dv)
            v = v_ref[0, rows, vc]
            a = jnp.where(causal, _dot_nt(q_e[:, kc], k_e[:, kc]), 0.0)
            s_t = s_sc[h]
            o = _dot(a.astype(BF16), v) + _dot_nt(q_e[:, kc], s_t.astype(BF16))
            s_sc[h] = s_t * decay[:, kc] + _dot_tn(v, k_d[:, kc])
            y = o * lax.rsqrt(jnp.mean(o * o, axis=-1, keepdims=True) + EPS) * go_ref[...]
            o_ref[0, rows, vc] = (y * sr_ref[0, rows, vc]).astype(BF16)
    st_ref[0] = s_sc[...]


def _gla_seq(q, k, gl, v, sr, g_o, nh, dk, dv, tt):
    g, l, qk = q.shape
    vv = nh * dv
    tri = jnp.asarray(np.kron(np.eye(tt // GLA_CHUNK), np.tril(np.ones((GLA_CHUNK, GLA_CHUNK)))), BF16)
    row = lambda b, i: (b, i, 0)
    return pl.pallas_call(
        functools.partial(_gla_seq_kernel, nh=nh, dk=dk, dv=dv, tt=tt),
        grid=(g, l // tt),
        in_specs=[pl.BlockSpec((1, tt, qk), row), pl.BlockSpec((1, tt, qk), row), pl.BlockSpec((1, tt, qk), row),
                  pl.BlockSpec((1, tt, vv), row), pl.BlockSpec((1, tt, vv), row), _full((1, dv)), _full((tt, tt))],
        out_specs=[pl.BlockSpec((1, tt, vv), row), pl.BlockSpec((1, nh, dv, dk), lambda b, i: (b, 0, 0, 0))],
        out_shape=[jax.ShapeDtypeStruct((g, l, vv), BF16), jax.ShapeDtypeStruct((g, nh, dv, dk), F32)],
        scratch_shapes=[pltpu.VMEM((nh, dv, dk), F32)],
        compiler_params=_cparams("arbitrary", "arbitrary"),
        name="gla_seq",
    )(q, k, gl, v, sr, g_o.reshape(1, dv), tri)


def _gla_dec_kernel(q_ref, k_ref, g_ref, v_ref, sr_ref, go_ref, s0_ref, o_ref, s_ref, *, nh, dk, dv):
    def column(r):
        return jnp.broadcast_to(r, (dk, dk)).T

    for h in range(nh):
        kc = slice(h * dk, (h + 1) * dk)
        dcol = jnp.exp(column(g_ref[0, :, kc]))
        kcol = column(k_ref[0, :, kc])
        qcol = column(q_ref[0, :, kc])
        halves = []
        for half in range(dv // dk):
            lanes = slice(half * dk, (half + 1) * dk)
            v = v_ref[0, :, h * dv + half * dk:h * dv + (half + 1) * dk].astype(F32)
            s_new = dcol * s0_ref[0, 0, h, :, lanes] + kcol * v
            s_ref[0, h, :, lanes] = s_new
            halves.append(jnp.sum(qcol * s_new, axis=0, keepdims=True))
        o = jnp.concatenate(halves, axis=1)
        y = o * lax.rsqrt(jnp.mean(o * o, axis=-1, keepdims=True) + EPS) * go_ref[...]
        vc = slice(h * dv, (h + 1) * dv)
        o_ref[0, :, vc] = y * sr_ref[0, :, vc]


def _gla_dec(q, k, gl, v, sr, g_o, state_gla, o_idx, nh, dk, dv):
    n_seq = q.shape[0]
    qk, vv = nh * dk, nh * dv
    r3 = lambda a: a.reshape(n_seq, 1, a.shape[-1])
    per_seq = lambda b: (b, 0, 0)
    return pl.pallas_call(
        functools.partial(_gla_dec_kernel, nh=nh, dk=dk, dv=dv),
        grid=(n_seq,),
        in_specs=[pl.BlockSpec((1, 1, qk), per_seq), pl.BlockSpec((1, 1, qk), per_seq),
                  pl.BlockSpec((1, 1, qk), per_seq), pl.BlockSpec((1, 1, vv), per_seq),
                  pl.BlockSpec((1, 1, vv), per_seq), pl.BlockSpec((1, dv), lambda b: (0, 0)),
                  pl.BlockSpec((1, 1, nh, dk, dv), lambda b: (b, o_idx, 0, 0, 0))],
        out_specs=[pl.BlockSpec((1, 1, vv), per_seq), pl.BlockSpec((1, nh, dk, dv), lambda b: (b, 0, 0, 0))],
        out_shape=[jax.ShapeDtypeStruct((n_seq, 1, vv), F32), jax.ShapeDtypeStruct((n_seq, nh, dk, dv), F32)],
        compiler_params=_cparams("arbitrary"),
        name="gla_dec",
    )(r3(q), r3(k), r3(gl), r3(v), r3(sr), g_o.reshape(1, dv), state_gla)


def _post_kernel(x_ref, a1_ref, a2_ref, wo_ref, gtm_ref, gf_ref, shf_ref, scf_ref, gtf_ref, w1_ref, w2_ref,
                 o_ref, x1_sc, h_sc, acc_sc, *, ka):
    f = pl.program_id(2)

    @pl.when(f == 0)
    def _():
        m = _dot(a1_ref[0], wo_ref[0:ka, :]) + _dot(a2_ref[0], wo_ref[ka:2 * ka, :])
        x1 = x_ref[0] + gtm_ref[0] * m
        x1_sc[...] = x1
        h_sc[...] = _norm_mod(x1, gf_ref[...], shf_ref[0], scf_ref[0]).astype(BF16)
        acc_sc[...] = jnp.zeros_like(acc_sc)

    t = jnp.maximum(_dot(h_sc[...], w1_ref[...]), 0.0)
    acc_sc[...] += _dot((t * t).astype(BF16), w2_ref[...])

    @pl.when(f == pl.num_programs(2) - 1)
    def _():
        o_ref[0] = x1_sc[...] + gtf_ref[0] * acc_sc[...]


def _post(x, a1, a1_col, a2, a2_col, w_out, gain_f, mod, w1, w2, tm, tf):
    g, l, d = x.shape
    ka = w_out.shape[0] // 2
    dff = w1.shape[1]
    r = mod.shape[1]
    row = lambda b, i, f: (b, i, 0)
    modc = lambda c: pl.BlockSpec((1, r, d), lambda b, i, f: (b, 0, c))
    const = lambda s: pl.BlockSpec(s, lambda b, i, f: (0, 0))
    return pl.pallas_call(
        functools.partial(_post_kernel, ka=ka),
        grid=(g, l // tm, dff // tf),
        in_specs=[pl.BlockSpec((1, tm, d), row),
                  pl.BlockSpec((1, tm, ka), lambda b, i, f: (b, i, a1_col)),
                  pl.BlockSpec((1, tm, ka), lambda b, i, f: (b, i, a2_col)),
                  const(w_out.shape), modc(2), const((1, d)), modc(3), modc(4), modc(5),
                  pl.BlockSpec((d, tf), lambda b, i, f: (0, f)), pl.BlockSpec((tf, d), lambda b, i, f: (f, 0))],
        out_specs=pl.BlockSpec((1, tm, d), row),
        out_shape=jax.ShapeDtypeStruct((g, l, d), F32),
        scratch_shapes=[pltpu.VMEM((tm, d), F32), pltpu.VMEM((tm, d), BF16), pltpu.VMEM((tm, d), F32)],
        compiler_params=_cparams("arbitrary", "arbitrary", "arbitrary"),
        name="post_mlp",
    )(x, a1, a2, w_out, mod, gain_f, mod, mod, mod, w1, w2)


def kernel(x_prompt, x_sample, c_prompt, c_sample, cache_k, cache_v, cache_logf, page_table, state_conv, state_gla,
           g_mix, g_ffn, w_ada, b_ada, w_in_even, b_f, g_q, g_k, w_conv, w_out_even, w_in_odd, w_gk, b_gk, g_o,
           w_out_odd, w_ff1, w_ff2):
    bp, seq, d = x_prompt.shape
    n_seq = x_sample.shape[0]
    depth = w_ada.shape[0]
    n_even, nh_fox = b_f.shape
    hd_fox = g_q.shape[1]
    cc = nh_fox * hd_fox
    _, n_odd, nh_gla, dk, dv = state_gla.shape
    qk, vv = nh_gla * dk, nh_gla * dv

    mod_all = _adaln(jnp.concatenate([c_prompt, c_sample], axis=0), w_ada, b_ada)

    cache_kt = jnp.transpose(cache_k, (0, 1, 3, 4, 2))
    cache_vt = jnp.transpose(cache_v, (0, 1, 3, 4, 2))
    cache_lft = jnp.swapaxes(cache_logf, 2, 3)

    xp = x_prompt
    xs = x_sample.reshape(1, n_seq, d)
    fox_p, fox_s, conv_p, conv_s, gla_p, gla_s = [], [], [], [], [], []
    for li in range(depth):
        mod_p = mod_all[li, :bp].reshape(bp, 1, 6 * d)
        mod_s = mod_all[li, bp:].reshape(1, n_seq, 6 * d)
        gm = g_mix[li].reshape(1, d)
        gf = g_ffn[li].reshape(1, d)
        w1 = w_ff1[li].astype(BF16)
        w2 = w_ff2[li].astype(BF16)
        if li % 2 == 0:
            e = li // 2
            wts = _even_weights(w_in_even[e], b_f[e], g_q[e], g_k[e], nh_fox, hd_fox)
            w_out = w_out_even[e].astype(BF16)
            ya, utail, qp, kp, va, k_rows, v_rows, lf_rows = _even_in_seq(
                xp, gm, mod_p, wts, w_conv[e], nh_fox, hd_fox, tm=256)
            o = _attn_prompt(qp, kp, va, nh_fox, hd_fox, tq=512)
            xp = _post(xp, ya, 0, o, 0, w_out, gf, mod_p, w1, w2, tm=512, tf=1024)
            fox_p.append((k_rows, v_rows, lf_rows))
            conv_p.append(utail[:, 6:8])
            prev0 = state_conv[:, e, 0].reshape(1, n_seq, cc)
            prev1 = state_conv[:, e, 1].reshape(1, n_seq, cc)
            ya, u, qn, kn, v, lfpad = _even_in_dec(xs, gm, mod_s, wts, w_conv[e], prev0, prev1, nh_fox, hd_fox)
            lf_new = lfpad[0, :, :nh_fox]
            o = _attn_decode(page_table, qn[0], kn[0], v[0], lf_new, cache_kt, cache_vt, cache_lft, e,
                             nh_fox, hd_fox)
            o = o.reshape(1, n_seq, cc).astype(BF16)
            xs = _post(xs, ya, 0, o, 0, w_out, gf, mod_s, w1, w2, tm=n_seq, tf=1024)
            fox_s.append((kn[0], v[0], lf_new))
            conv_s.append(jnp.stack([prev1[0], u[0]], axis=1))
        else:
            oi = li // 2
            wts = _odd_weights(w_in_odd[oi], w_gk[oi], b_gk[oi], qk, vv)
            w_out = w_out_odd[oi].astype(BF16)
            q, k, v, sr, gl = _odd_in(xp, gm, mod_p, wts, qk, vv, dk, tm=256)
            og, s_t = _gla_seq(q, k, gl, v, sr, g_o[oi], nh_gla, dk, dv, tt=256)
            xp = _post(xp, og, 0, og, 1, w_out, gf, mod_p, w1, w2, tm=512, tf=1024)
            gla_p.append(jnp.swapaxes(s_t, 2, 3))
            q, k, v, sr, gl = _odd_in(xs, gm, mod_s, wts, qk, vv, dk, tm=n_seq)
            og, s_new = _gla_dec(q[0], k[0], gl[0], v[0].astype(F32), sr[0], g_o[oi], state_gla, oi,
                                 nh_gla, dk, dv)
            og = og.reshape(1, n_seq, vv).astype(BF16)
            xs = _post(xs, og, 0, og, 1, w_out, gf, mod_s, w1, w2, tm=n_seq, tf=1024)
            gla_s.append(s_new)

    heads = lambda a, n: a.reshape(a.shape[0], n, nh_fox, hd_fox)
    k_p = jnp.stack([heads(r[0], seq) for r in fox_p], axis=1)
    v_p = jnp.stack([heads(r[1], seq) for r in fox_p], axis=1)
    lf_p = jnp.stack([r[2] for r in fox_p], axis=1)
    k_s = jnp.stack([heads(r[0], 1) for r in fox_s], axis=1)
    v_s = jnp.stack([heads(r[1], 1) for r in fox_s], axis=1)
    lf_s = jnp.stack([r[2].reshape(n_seq, 1, nh_fox) for r in fox_s], axis=1)
    return (xp, xs.reshape(n_seq, 1, d), k_p, v_p, lf_p, k_s, v_s, lf_s,
            jnp.stack(conv_p, axis=1), jnp.stack(conv_s, axis=1),
            jnp.stack(gla_p, axis=1), jnp.stack(gla_s, axis=1))
```

```python
import functools

import numpy as np
import jax
import jax.numpy as jnp
from jax import lax
from jax.experimental import pallas as pl
from jax.experimental.pallas import tpu as pltpu

F32 = jnp.float32
BF16 = jnp.bfloat16

EPS = 1e-6
GLA_TAU = 16.0
GLA_CHUNK = 64
LANES = 128
VMEM_LIMIT = 56 * 1024 * 1024
NEG = -1e30
LOG2E = 1.4426950408889634
PAGES_PER_STEP = 16
ONES_ROWS = 16
TM_EVEN = 512
TM_ODD = 512
TM_POST = 1024
TF_POST = 1024
TQ_ATTN = 512
TT_GLA = 256


def _cparams(*sem):
    return pltpu.CompilerParams(dimension_semantics=sem, vmem_limit_bytes=VMEM_LIMIT)


def _silu(x):
    return x / (1.0 + jnp.exp(-x))


def _log_sigmoid(z):
    return jnp.minimum(z, 0.0) - jnp.log1p(jnp.exp(-jnp.abs(z)))


def _norm_mod(x, gain, shift, scale):
    y = x * lax.rsqrt(jnp.mean(x * x, axis=-1, keepdims=True) + EPS)
    return (y * gain) * (1.0 + scale) + shift


def _split3(a):
    hi = a.astype(BF16)
    r = a - hi.astype(F32)
    mid = r.astype(BF16)
    lo = (r - mid.astype(F32)).astype(BF16)
    return hi, mid, lo


def _dot(a, b):
    return jnp.dot(a, b, preferred_element_type=F32)


def _dot_nt(a, b):
    return lax.dot_general(a, b, (((1,), (1,)), ((), ())), preferred_element_type=F32)


def _dot_tn(a, b):
    return lax.dot_general(a, b, (((0,), (0,)), ((), ())), preferred_element_type=F32)


def _sum3(c, w):
    return c[:, 0:w] + c[:, w:2 * w] + c[:, 2 * w:3 * w]


def _adaln_kernel(c_ref, w_ref, b_ref, o_ref):
    a = _silu(c_ref[...]).astype(BF16)
    o_ref[0] = _dot(a, w_ref[0].astype(BF16)) + b_ref[0]


def _adaln(c_all, w_ada, b_ada):
    depth, d, n = w_ada.shape
    r = c_all.shape[0]
    tn = n // 4
    return pl.pallas_call(
        _adaln_kernel,
        grid=(depth, n // tn),
        in_specs=[pl.BlockSpec((r, d), lambda l, j: (0, 0)),
                  pl.BlockSpec((1, d, tn), lambda l, j: (l, 0, j)),
                  pl.BlockSpec((1, 1, tn), lambda l, j: (l, 0, j))],
        out_specs=pl.BlockSpec((1, r, tn), lambda l, j: (l, 0, j)),
        out_shape=jax.ShapeDtypeStruct((depth, r, n), F32),
        compiler_params=_cparams("arbitrary", "arbitrary"),
        name="adaln",
    )(c_all, w_ada, b_ada.reshape(depth, 1, n))


def _even_project(x_ref, g_ref, sh_ref, sc_ref, w_ref, wf_ref, bf_ref, gq_ref, gk_ref, p_ref, cc, nh):
    hb = _norm_mod(x_ref[0], g_ref[...], sh_ref[0], sc_ref[0]).astype(BF16)
    gb = _dot(hb, w_ref[:, 0:cc])
    u = _dot(hb, w_ref[:, cc:2 * cc]) * _dot(hb, w_ref[:, 2 * cc:3 * cc])
    q = _dot(hb, w_ref[:, 3 * cc:4 * cc])
    k = _dot(hb, w_ref[:, 4 * cc:5 * cc])
    v = _dot(hb, w_ref[:, 5 * cc:6 * cc])
    hd = cc // nh
    qms = _dot((q * q).astype(BF16), p_ref[...]) * (1.0 / hd)
    kms = _dot((k * k).astype(BF16), p_ref[...]) * (1.0 / hd)
    qn = (q * lax.rsqrt(qms + EPS)) * gq_ref[...]
    kn = (k * lax.rsqrt(kms + EPS)) * gk_ref[...]
    fz = _dot(hb, wf_ref[...]) + bf_ref[...]
    lane = lax.broadcasted_iota(jnp.int32, fz.shape, 1)
    lf = jnp.where(lane < nh, _log_sigmoid(fz), 0.0)
    return gb, u, qn, kn, v, lf


def _even_seq_kernel(x_ref, g_ref, sh_ref, sc_ref, w_ref, wf_ref, bf_ref, gq_ref, gk_ref, wc_ref, p_ref,
                     tri_ref, eq_ref, ek_ref, oq_ref, ok_ref,
                     ya_ref, ut_ref, qp_ref, kp_ref, va_ref, ko_ref, vo_ref, lf_ref,
                     ubuf, fcarry, *, nh, cc, tm):
    i = pl.program_id(1)
    gb, u, qn, kn, v, lf = _even_project(x_ref, g_ref, sh_ref, sc_ref, w_ref, wf_ref, bf_ref,
                                         gq_ref, gk_ref, p_ref, cc, nh)

    @pl.when(i == 0)
    def _():
        ubuf[0:8, :] = jnp.zeros((8, cc), F32)
        fcarry[...] = jnp.zeros_like(fcarry)

    ubuf[8:8 + tm, :] = u
    conv = wc_ref[0:1, :] * ubuf[6:6 + tm, :] + wc_ref[1:2, :] * ubuf[7:7 + tm, :] + wc_ref[2:3, :] * u
    ya_ref[0] = (gb * conv).astype(BF16)
    tail = ubuf[tm:tm + 8, :]
    ubuf[0:8, :] = tail
    ut_ref[0] = tail

    ko_ref[0] = kn
    vo_ref[0] = v
    lf_ref[0] = lf[:, 0:nh]
    vb = v.astype(BF16)
    for pair in range(nh // 2):
        va_ref[0, :, 2 * pair * LANES:(2 * pair + 1) * LANES] = vb[:, pair * LANES:(pair + 1) * LANES]
        va_ref[0, :, (2 * pair + 1) * LANES:(2 * pair + 2) * LANES] = jnp.ones((tm, LANES), BF16)

    c3 = _dot(tri_ref[...], jnp.concatenate(_split3(lf), axis=1))
    f = _sum3(c3, LANES) + fcarry[...]
    fcarry[...] = f[tm - 1:tm, :]
    f3 = jnp.concatenate(_split3(f * LOG2E), axis=1)

    qp_ref[0] = (_dot(f3, eq_ref[...]) + oq_ref[...]).astype(BF16)
    kp_ref[0] = (_dot(f3, ek_ref[...]) + ok_ref[...]).astype(BF16)
    qb = qn.astype(BF16)
    kb = kn.astype(BF16)
    hd = cc // nh
    for h in range(nh):
        qp_ref[0, :, h * LANES:h * LANES + hd] = qb[:, h * hd:(h + 1) * hd]
        kp_ref[0, :, h * LANES:h * LANES + hd] = kb[:, h * hd:(h + 1) * hd]


def _even_dec_kernel(x_ref, g_ref, sh_ref, sc_ref, w_ref, wf_ref, bf_ref, gq_ref, gk_ref, wc_ref, p_ref,
                     p0_ref, p1_ref,
                     ya_ref, u_ref, q_ref, k_ref, v_ref, lf_ref, *, nh, cc):
    gb, u, qn, kn, v, lf = _even_project(x_ref, g_ref, sh_ref, sc_ref, w_ref, wf_ref, bf_ref,
                                         gq_ref, gk_ref, p_ref, cc, nh)
    conv = wc_ref[0:1, :] * p0_ref[0] + wc_ref[1:2, :] * p1_ref[0] + wc_ref[2:3, :] * u
    ya_ref[0] = (gb * conv).astype(BF16)
    u_ref[0] = u
    q_ref[0] = qn
    k_ref[0] = kn
    v_ref[0] = v
    lf_ref[0] = lf


def _even_consts(nh, hd, tm):
    cc = nh * hd
    p = np.kron(np.eye(nh), np.ones((hd, hd)))
    tri = np.tril(np.ones((tm, tm)))
    eq = np.zeros((3 * LANES, nh * LANES))
    ek = np.zeros((3 * LANES, nh * LANES))
    oq = np.zeros((1, nh * LANES))
    ok = np.zeros((1, nh * LANES))
    for h in range(nh):
        for part in range(3):
            eq[part * LANES + h, h * LANES + hd + part] = 1.0
            ek[part * LANES + h, h * LANES + hd + 3 + part] = -1.0
            oq[0, h * LANES + hd + 3 + part] = 1.0
            ok[0, h * LANES + hd + part] = 1.0
    bf = lambda a: jnp.asarray(a, BF16)
    return bf(p), bf(tri), bf(eq), bf(ek), jnp.asarray(oq, F32), jnp.asarray(ok, F32)


def _even_weights(w_in, b_f, g_q, g_k, nh, hd):
    d = w_in.shape[0]
    cc = nh * hd
    w_main = w_in[:, :6 * cc].astype(BF16)
    w_f = jnp.zeros((d, LANES), BF16).at[:, :nh].set(w_in[:, 6 * cc:].astype(BF16))
    bf = jnp.zeros((1, LANES), F32).at[0, :nh].set(b_f)
    gq = (jnp.tile(g_q, nh) * (hd ** -0.5)).reshape(1, cc)
    gk = jnp.tile(g_k, nh).reshape(1, cc)
    return w_main, w_f, bf, gq, gk


def _full(shape):
    nd = len(shape)
    return pl.BlockSpec(shape, lambda *_: (0,) * nd)


def _even_in_seq(x, gain, mod, wts, w_conv, nh, hd, tm):
    g, l, d = x.shape
    cc = nh * hd
    w_main, w_f, bf, gq, gk = wts
    gq = gq * LOG2E
    p, tri, eq, ek, oq, ok = _even_consts(nh, hd, tm)
    r = mod.shape[1]
    row = lambda b, i: (b, i, 0)
    in_specs = [pl.BlockSpec((1, tm, d), row), _full((1, d)),
                pl.BlockSpec((1, r, d), lambda b, i: (b, 0, 0)), pl.BlockSpec((1, r, d), lambda b, i: (b, 0, 1)),
                _full(w_main.shape), _full(w_f.shape), _full(bf.shape), _full(gq.shape), _full(gk.shape),
                _full(w_conv.shape), _full(p.shape), _full(tri.shape), _full(eq.shape), _full(ek.shape),
                _full(oq.shape), _full(ok.shape)]
    out_shape = [jax.ShapeDtypeStruct((g, l, cc), BF16),
                 jax.ShapeDtypeStruct((g, 8, cc), F32),
                 jax.ShapeDtypeStruct((g, l, nh * LANES), BF16),
                 jax.ShapeDtypeStruct((g, l, nh * LANES), BF16),
                 jax.ShapeDtypeStruct((g, l, nh * LANES), BF16),
                 jax.ShapeDtypeStruct((g, l, cc), F32),
                 jax.ShapeDtypeStruct((g, l, cc), F32),
                 jax.ShapeDtypeStruct((g, l, nh), F32)]
    out_specs = [pl.BlockSpec((1, tm, cc), row), pl.BlockSpec((1, 8, cc), lambda b, i: (b, 0, 0)),
                 pl.BlockSpec((1, tm, nh * LANES), row), pl.BlockSpec((1, tm, nh * LANES), row),
                 pl.BlockSpec((1, tm, nh * LANES), row), pl.BlockSpec((1, tm, cc), row),
                 pl.BlockSpec((1, tm, cc), row), pl.BlockSpec((1, tm, nh), row)]
    return pl.pallas_call(
        functools.partial(_even_seq_kernel, nh=nh, cc=cc, tm=tm),
        grid=(g, l // tm), in_specs=in_specs, out_specs=out_specs, out_shape=out_shape,
        scratch_shapes=[pltpu.VMEM((tm + 8, cc), F32), pltpu.VMEM((1, LANES), F32)],
        compiler_params=_cparams("arbitrary", "arbitrary"),
        name="even_in_seq",
    )(x, gain, mod, mod, w_main, w_f, bf, gq, gk, w_conv, p, tri, eq, ek, oq, ok)


def _even_in_dec(x, gain, mod, wts, w_conv, prev0, prev1, nh, hd):
    g, l, d = x.shape
    cc = nh * hd
    w_main, w_f, bf, gq, gk = wts
    p = _even_consts(nh, hd, 8)[0]
    whole = lambda s: pl.BlockSpec(s, lambda i: (0, 0, 0))
    in_specs = [whole((1, l, d)), pl.BlockSpec((1, d), lambda i: (0, 0)),
                pl.BlockSpec((1, l, d), lambda i: (0, 0, 0)), pl.BlockSpec((1, l, d), lambda i: (0, 0, 1)),
                _full(w_main.shape), _full(w_f.shape), _full(bf.shape), _full(gq.shape), _full(gk.shape),
                _full(w_conv.shape), _full(p.shape), whole((1, l, cc)), whole((1, l, cc))]
    out_shape = [jax.ShapeDtypeStruct((1, l, cc), BF16)] + [jax.ShapeDtypeStruct((1, l, cc), F32)] * 4 \
        + [jax.ShapeDtypeStruct((1, l, LANES), F32)]
    out_specs = [whole((1, l, cc))] * 5 + [whole((1, l, LANES))]
    return pl.pallas_call(
        functools.partial(_even_dec_kernel, nh=nh, cc=cc),
        grid=(1,), in_specs=in_specs, out_specs=out_specs, out_shape=out_shape,
        compiler_params=_cparams("arbitrary"),
        name="even_in_dec",
    )(x, gain, mod, mod, w_main, w_f, bf, gq, gk, w_conv, p, prev0, prev1)


def _attn_kernel(qp_ref, kp_ref, va_ref, o_ref, m_sc, l_sc, acc_sc, s_sc, p_sc, *, nh, hd, tq):
    i = pl.program_id(1)
    j = pl.program_id(2)
    reps = tq // LANES

    @pl.when(j == 0)
    def _():
        m_sc[...] = jnp.full(m_sc.shape, NEG, F32)
        l_sc[...] = jnp.zeros_like(l_sc)
        acc_sc[...] = jnp.zeros_like(acc_sc)

    def scores(h):
        s_sc[h % 2] = _dot_nt(qp_ref[0, :, h * LANES:(h + 1) * LANES], kp_ref[0, :, h * LANES:(h + 1) * LANES])

    def step(diagonal):
        scores(0)
        for h in range(nh):
            if h + 1 < nh:
                scores(h + 1)
            s = s_sc[h % 2]
            if diagonal:
                row = lax.broadcasted_iota(jnp.int32, s.shape, 0)
                col = lax.broadcasted_iota(jnp.int32, s.shape, 1)
                s = jnp.where(col <= row, s, NEG)
            m_old = m_sc[h]
            m_new = jnp.maximum(m_old, jnp.max(s, axis=1, keepdims=True))
            alpha = jnp.exp2(m_old - m_new)
            p_sc[h % 2] = jnp.exp2(s - jnp.concatenate([m_new] * reps, axis=1)).astype(BF16)
            pair = h // 2
            pv = _dot(p_sc[h % 2], va_ref[0, :, pair * 2 * LANES:(pair + 1) * 2 * LANES])
            acc_sc[h] = alpha * acc_sc[h] + pv[:, 0:LANES]
            l_sc[h] = alpha * l_sc[h] + pv[:, LANES:2 * LANES]
            m_sc[h] = m_new

    @pl.when(j < i)
    def _():
        step(False)

    @pl.when(j == i)
    def _():
        step(True)
        lane = lax.broadcasted_iota(jnp.int32, (tq, LANES), 1)
        for pair in range(nh // 2):
            o0 = acc_sc[2 * pair] / l_sc[2 * pair]
            o1 = acc_sc[2 * pair + 1] / l_sc[2 * pair + 1]
            o_ref[0, :, pair * LANES:(pair + 1) * LANES] = jnp.where(lane < hd, o0, o1).astype(BF16)


def _attn_prompt(qp, kp, va, nh, hd, tq):
    g, l, _ = qp.shape
    cc = nh * hd
    nq = l // tq
    return pl.pallas_call(
        functools.partial(_attn_kernel, nh=nh, hd=hd, tq=tq),
        grid=(g, nq, nq),
        in_specs=[pl.BlockSpec((1, tq, nh * LANES), lambda b, i, j: (b, i, 0)),
                  pl.BlockSpec((1, tq, nh * LANES), lambda b, i, j: (b, jnp.minimum(i, j), 0)),
                  pl.BlockSpec((1, tq, nh * LANES), lambda b, i, j: (b, jnp.minimum(i, j), 0))],
        out_specs=pl.BlockSpec((1, tq, cc), lambda b, i, j: (b, i, 0)),
        out_shape=jax.ShapeDtypeStruct((g, l, cc), BF16),
        scratch_shapes=[pltpu.VMEM((nh, tq, LANES), F32), pltpu.VMEM((nh, tq, LANES), F32),
                        pltpu.VMEM((nh, tq, LANES), F32), pltpu.VMEM((2, tq, tq), F32),
                        pltpu.VMEM((2, tq, tq), BF16)],
        compiler_params=_cparams("arbitrary", "arbitrary", "arbitrary"),
        name="attn_prompt",
    )(qp, kp, va)


def _attn_dec_kernel(pt_ref, qcol_ref, kncol_ref, vncol_ref, lfnew_ref, uincl_ref, ones_ref, *rest, npg, nh):
    k_refs = rest[:npg]
    v_refs = rest[npg:2 * npg]
    lf_refs = rest[2 * npg:3 * npg]
    o_ref = rest[3 * npg]
    m_sc, l_sc, acc_sc, c_sc = rest[3 * npg + 1:]
    g = pl.program_id(1)

    @pl.when(g == 0)
    def _():
        m_sc[...] = jnp.full(m_sc.shape, NEG, F32)
        l_sc[...] = jnp.zeros_like(l_sc)
        acc_sc[...] = jnp.zeros_like(acc_sc)
        c_sc[...] = jnp.zeros_like(c_sc)

    parts = []
    for j in range(npg):
        hi, mid, lo = _split3(lf_refs[j][0, 0])
        parts += [hi.astype(F32), mid.astype(F32), lo.astype(F32), jnp.zeros((nh, LANES), F32)]
    parts = jnp.concatenate(parts, axis=0).astype(BF16)
    cum = _dot(parts, uincl_ref[...])
    tot = _dot(parts, ones_ref[...])
    fs = []
    c = c_sc[...]
    for j in range(npg):
        r = 4 * nh * j
        fs.append(c + (cum[r:r + nh] + cum[r + nh:r + 2 * nh] + cum[r + 2 * nh:r + 3 * nh]))
        c = c + (tot[r:r + nh] + tot[r + nh:r + 2 * nh] + tot[r + 2 * nh:r + 3 * nh])
    c_sc[...] = c

    for h in range(nh):
        qc = qcol_ref[0, h]
        rows = [jnp.sum(qc * k_refs[j][0, 0, h], axis=0, keepdims=True) - fs[j][h:h + 1, :] for j in range(npg)]
        m_old = m_sc[h]
        m_new = jnp.maximum(m_old, jnp.max(functools.reduce(jnp.maximum, rows), axis=1, keepdims=True))
        alpha = jnp.exp(m_old - m_new)
        ps = [jnp.exp(r - m_new) for r in rows]
        l_sc[h] = alpha * l_sc[h] + jnp.sum(functools.reduce(jnp.add, ps), axis=1, keepdims=True)
        acc = alpha * acc_sc[h]
        for j in range(npg):
            acc = acc + ps[j] * v_refs[j][0, 0, h]
        acc_sc[h] = acc
        m_sc[h] = m_new

    @pl.when(g == pl.num_programs(1) - 1)
    def _():
        lane = lax.broadcasted_iota(jnp.int32, (1, LANES), 1)
        for h in range(nh):
            f_new = c_sc[h:h + 1, :] + lfnew_ref[0, h:h + 1, :]
            s_new = jnp.sum(qcol_ref[0, h] * kncol_ref[0, h], axis=0, keepdims=True) - f_new
            s_new = jnp.where(lane == 0, s_new, NEG)
            m_old = m_sc[h]
            m_new = jnp.maximum(m_old, jnp.max(s_new, axis=1, keepdims=True))
            alpha = jnp.exp(m_old - m_new)
            p_new = jnp.exp(s_new - m_new)
            l_fin = alpha * l_sc[h] + jnp.sum(p_new, axis=1, keepdims=True)
            acc = (alpha * acc_sc[h] + p_new * vncol_ref[0, h]) / l_fin
            hi, mid, lo = _split3(acc)
            ones = ones_ref[0:ONES_ROWS, :]
            o = _dot_nt(ones, hi) + _dot_nt(ones, mid) + _dot_nt(ones, lo)
            o_ref[0, h:h + 1, :] = o[0:1, :]


def _attn_decode(page_table, q, knew, vnew, lfnew, cache_kt, cache_vt, cache_lft, e, nh, hd):
    n_seq, n_pages = page_table.shape
    page = cache_kt.shape[-1]
    assert page == LANES and n_pages % PAGES_PER_STEP == 0
    npg = PAGES_PER_STEP
    col = lambda a: jnp.broadcast_to(a.reshape(n_seq, nh, hd, 1), (n_seq, nh, hd, LANES))
    uincl = jnp.asarray(np.triu(np.ones((page, page))), BF16)
    ones = jnp.ones((LANES, LANES), BF16)

    def pg(j, nd):
        return lambda b, g, pt: (pt[b, g * npg + j], e) + (0,) * nd

    per_seq4 = lambda b, g, pt: (b, 0, 0, 0)
    per_seq3 = lambda b, g, pt: (b, 0, 0)
    const2 = lambda b, g, pt: (0, 0)
    in_specs = [pl.BlockSpec((1, nh, hd, LANES), per_seq4), pl.BlockSpec((1, nh, hd, LANES), per_seq4),
                pl.BlockSpec((1, nh, hd, LANES), per_seq4), pl.BlockSpec((1, nh, 1), per_seq3),
                pl.BlockSpec((page, page), const2), pl.BlockSpec((LANES, LANES), const2)]
    in_specs += [pl.BlockSpec((1, 1, nh, hd, page), pg(j, 3)) for j in range(npg)]
    in_specs += [pl.BlockSpec((1, 1, nh, hd, page), pg(j, 3)) for j in range(npg)]
    in_specs += [pl.BlockSpec((1, 1, nh, page), pg(j, 2)) for j in range(npg)]
    grid_spec = pltpu.PrefetchScalarGridSpec(
        num_scalar_prefetch=1, grid=(n_seq, n_pages // npg), in_specs=in_specs,
        out_specs=pl.BlockSpec((1, nh, hd), per_seq3),
        scratch_shapes=[pltpu.VMEM((nh, 1, LANES), F32), pltpu.VMEM((nh, 1, LANES), F32),
                        pltpu.VMEM((nh, hd, LANES), F32), pltpu.VMEM((nh, LANES), F32)])
    o = pl.pallas_call(
        functools.partial(_attn_dec_kernel, npg=npg, nh=nh),
        grid_spec=grid_spec,
        out_shape=jax.ShapeDtypeStruct((n_seq, nh, hd), F32),
        compiler_params=_cparams("arbitrary", "arbitrary"),
        name="attn_decode",
    )(page_table, col(q), col(knew), col(vnew), lfnew.reshape(n_seq, nh, 1), uincl, ones,
      *([cache_kt] * npg), *([cache_vt] * npg), *([cache_lft] * npg))
    return o.reshape(n_seq, 1, nh * hd)


def _odd_in_kernel(x_ref, g_ref, sh_ref, sc_ref, w_ref, wgd_ref, wgk_ref, bgk_ref,
                   q_ref, k_ref, v_ref, sr_ref, gl_ref, *, qk, vv, dk):
    hb = _norm_mod(x_ref[0], g_ref[...], sh_ref[0], sc_ref[0]).astype(BF16)
    q_ref[0] = _dot(hb, w_ref[:, 0:qk]) * (dk ** -0.5)
    k_ref[0] = _dot(hb, w_ref[:, qk:2 * qk])
    v_ref[0] = _dot(hb, w_ref[:, 2 * qk:2 * qk + vv]).astype(BF16)
    sr_ref[0] = _silu(_dot(hb, w_ref[:, 2 * qk + vv:2 * qk + 2 * vv]))
    gd = _dot(hb, wgd_ref[...]).astype(BF16)
    gl_ref[0] = _log_sigmoid(_dot(gd, wgk_ref[...]) + bgk_ref[...]) * (1.0 / GLA_TAU)


def _odd_weights(w_in, w_gk, b_gk, qk, vv):
    d = w_in.shape[0]
    rank = w_gk.shape[0]
    w_main = w_in[:, :2 * qk + 2 * vv].astype(BF16)
    w_gd = jnp.zeros((d, LANES), BF16).at[:, :rank].set(w_in[:, 2 * qk + 2 * vv:].astype(BF16))
    wgk = jnp.zeros((LANES, qk), BF16).at[:rank].set(w_gk.astype(BF16))
    return w_main, w_gd, wgk, b_gk.reshape(1, qk)


def _odd_in(x, gain, mod, wts, qk, vv, dk, tm):
    g, l, d = x.shape
    w_main, w_gd, wgk, bgk = wts
    r = mod.shape[1]
    row = lambda b, i: (b, i, 0)
    in_specs = [pl.BlockSpec((1, tm, d), row), _full((1, d)),
                pl.BlockSpec((1, r, d), lambda b, i: (b, 0, 0)), pl.BlockSpec((1, r, d), lambda b, i: (b, 0, 1)),
                _full(w_main.shape), _full(w_gd.shape), _full(wgk.shape), _full(bgk.shape)]
    out_shape = [jax.ShapeDtypeStruct((g, l, qk), F32), jax.ShapeDtypeStruct((g, l, qk), F32),
                 jax.ShapeDtypeStruct((g, l, vv), BF16), jax.ShapeDtypeStruct((g, l, vv), F32),
                 jax.ShapeDtypeStruct((g, l, qk), F32)]
    out_specs = [pl.BlockSpec((1, tm, qk), row), pl.BlockSpec((1, tm, qk), row), pl.BlockSpec((1, tm, vv), row),
                 pl.BlockSpec((1, tm, vv), row), pl.BlockSpec((1, tm, qk), row)]
    return pl.pallas_call(
        functools.partial(_odd_in_kernel, qk=qk, vv=vv, dk=dk),
        grid=(g, l // tm), in_specs=in_specs, out_specs=out_specs, out_shape=out_shape,
        compiler_params=_cparams("arbitrary", "arbitrary"),
        name="odd_in",
    )(x, gain, mod, mod, w_main, w_gd, wgk, bgk)


def _gla_seq_kernel(q_ref, k_ref, g_ref, v_ref, sr_ref, go_ref, tri_ref, o_ref, st_ref, s_sc, *, nh, dk, dv, tt):
    t = pl.program_id(1)
    c = GLA_CHUNK

    @pl.when(t == 0)
    def _():
        s_sc[...] = jnp.zeros_like(s_sc)

    qk = nh * dk
    b_all = _sum3(_dot(tri_ref[...], jnp.concatenate(_split3(g_ref[0]), axis=1)), qk)
    row = lax.broadcasted_iota(jnp.int32, (c, c), 0)
    col = lax.broadcasted_iota(jnp.int32, (c, c), 1)
    causal = col <= row
    for n in range(tt // c):
        rows = slice(n * c, (n + 1) * c)
        b = b_all[rows]
        b_last = b[c - 1:c, :]
        q = q_ref[0, rows, :]
        k = k_ref[0, rows, :]
        q_e = (q * jnp.exp(b)).astype(BF16)
        k_e = (k * jnp.exp(-b)).astype(BF16)
        k_d = (k * jnp.exp(b_last - b)).astype(BF16)
        decay = jnp.exp(b_last)
        for h in range(nh):
            kc = slice(h * dk, (h + 1) * dk)
            vc = slice(h * dv, (h + 1) * dv)
            v = v_ref[0, rows, vc]
            a = jnp.where(causal, _dot_nt(q_e[:, kc], k_e[:, kc]), 0.0)
            s_t = s_sc[h]
            o = _dot(a.astype(BF16), v) + _dot_nt(q_e[:, kc], s_t.astype(BF16))
            s_sc[h] = s_t * decay[:, kc] + _dot_tn(v, k_d[:, kc])
            y = o * lax.rsqrt(jnp.mean(o * o, axis=-1, keepdims=True) + EPS) * go_ref[...]
            o_ref[0, rows, vc] = (y * sr_ref[0, rows, vc]).astype(BF16)
    st_ref[0] = s_sc[...]


def _gla_seq(q, k, gl, v, sr, g_o, nh, dk, dv, tt):
    g, l, qk = q.shape
    vv = nh * dv
    tri = jnp.asarray(np.kron(np.eye(tt // GLA_CHUNK), np.tril(np.ones((GLA_CHUNK, GLA_CHUNK)))), BF16)
    row = lambda b, i: (b, i, 0)
    return pl.pallas_call(
        functools.partial(_gla_seq_kernel, nh=nh, dk=dk, dv=dv, tt=tt),
        grid=(g, l // tt),
        in_specs=[pl.BlockSpec((1, tt, qk), row), pl.BlockSpec((1, tt, qk), row), pl.BlockSpec((1, tt, qk), row),
                  pl.BlockSpec((1, tt, vv), row), pl.BlockSpec((1, tt, vv), row), _full((1, dv)), _full((tt, tt))],
        out_specs=[pl.BlockSpec((1, tt, vv), row), pl.BlockSpec((1, nh, dv, dk), lambda b, i: (b, 0, 0, 0))],
        out_shape=[jax.ShapeDtypeStruct((g, l, vv), BF16), jax.ShapeDtypeStruct((g, nh, dv, dk), F32)],
        scratch_shapes=[pltpu.VMEM((nh, dv, dk), F32)],
        compiler_params=_cparams("arbitrary", "arbitrary"),
        name="gla_seq",
    )(q, k, gl, v, sr, g_o.reshape(1, dv), tri)


def _gla_dec_kernel(q_ref, k_ref, g_ref, v_ref, sr_ref, go_ref, s0_ref, o_ref, s_ref, *, nh, dk, dv):
    def column(r):
        return jnp.broadcast_to(r, (dk, dk)).T

    for h in range(nh):
        kc = slice(h * dk, (h + 1) * dk)
        dcol = jnp.exp(column(g_ref[0, :, kc]))
        kcol = column(k_ref[0, :, kc])
        qcol = column(q_ref[0, :, kc])
        halves = []
        for half in range(dv // dk):
            lanes = slice(half * dk, (half + 1) * dk)
            v = v_ref[0, :, h * dv + half * dk:h * dv + (half + 1) * dk].astype(F32)
            s_new = dcol * s0_ref[0, 0, h, :, lanes] + kcol * v
            s_ref[0, h, :, lanes] = s_new
            halves.append(jnp.sum(qcol * s_new, axis=0, keepdims=True))
        o = jnp.concatenate(halves, axis=1)
        y = o * lax.rsqrt(jnp.mean(o * o, axis=-1, keepdims=True) + EPS) * go_ref[...]
        vc = slice(h * dv, (h + 1) * dv)
        o_ref[0, :, vc] = y * sr_ref[0, :, vc]


def _gla_dec(q, k, gl, v, sr, g_o, state_gla, o_idx, nh, dk, dv):
    n_seq = q.shape[0]
    qk, vv = nh * dk, nh * dv
    r3 = lambda a: a.reshape(n_seq, 1, a.shape[-1])
    per_seq = lambda b: (b, 0, 0)
    return pl.pallas_call(
        functools.partial(_gla_dec_kernel, nh=nh, dk=dk, dv=dv),
        grid=(n_seq,),
        in_specs=[pl.BlockSpec((1, 1, qk), per_seq), pl.BlockSpec((1, 1, qk), per_seq),
                  pl.BlockSpec((1, 1, qk), per_seq), pl.BlockSpec((1, 1, vv), per_seq),
                  pl.BlockSpec((1, 1, vv), per_seq), pl.BlockSpec((1, dv), lambda b: (0, 0)),
                  pl.BlockSpec((1, 1, nh, dk, dv), lambda b: (b, o_idx, 0, 0, 0))],
        out_specs=[pl.BlockSpec((1, 1, vv), per_seq), pl.BlockSpec((1, nh, dk, dv), lambda b: (b, 0, 0, 0))],
        out_shape=[jax.ShapeDtypeStruct((n_seq, 1, vv), F32), jax.ShapeDtypeStruct((n_seq, nh, dk, dv), F32)],
        compiler_params=_cparams("arbitrary"),
        name="gla_dec",
    )(r3(q), r3(k), r3(gl), r3(v), r3(sr), g_o.reshape(1, dv), state_gla)


def _post_kernel(x_ref, a1_ref, a2_ref, wo_ref, gtm_ref, gf_ref, shf_ref, scf_ref, gtf_ref, w1_ref, w2_ref,
                 o_ref, h_sc, *, ka):
    f = pl.program_id(2)

    @pl.when(f == 0)
    def _():
        m = _dot(a1_ref[0], wo_ref[0:ka, :]) + _dot(a2_ref[0], wo_ref[ka:2 * ka, :])
        x1 = x_ref[0] + gtm_ref[0] * m
        o_ref[0] = x1
        h_sc[...] = _norm_mod(x1, gf_ref[...], shf_ref[0], scf_ref[0]).astype(BF16)

    t = jnp.maximum(_dot(h_sc[...], w1_ref[...]), 0.0)
    o_ref[0] += gtf_ref[0] * _dot((t * t).astype(BF16), w2_ref[...])


def _post(x, a1, a1_col, a2, a2_col, w_out, gain_f, mod, w1, w2, tm, tf):
    g, l, d = x.shape
    ka = w_out.shape[0] // 2
    dff = w1.shape[1]
    r = mod.shape[1]
    row = lambda b, i, f: (b, i, 0)
    modc = lambda c: pl.BlockSpec((1, r, d), lambda b, i, f: (b, 0, c))
    const = lambda s: pl.BlockSpec(s, lambda b, i, f: (0, 0))
    return pl.pallas_call(
        functools.partial(_post_kernel, ka=ka),
        grid=(g, l // tm, dff // tf),
        in_specs=[pl.BlockSpec((1, tm, d), row),
                  pl.BlockSpec((1, tm, ka), lambda b, i, f: (b, i, a1_col)),
                  pl.BlockSpec((1, tm, ka), lambda b, i, f: (b, i, a2_col)),
                  const(w_out.shape), modc(2), const((1, d)), modc(3), modc(4), modc(5),
                  pl.BlockSpec((d, tf), lambda b, i, f: (0, f)), pl.BlockSpec((tf, d), lambda b, i, f: (f, 0))],
        out_specs=pl.BlockSpec((1, tm, d), row),
        out_shape=jax.ShapeDtypeStruct((g, l, d), F32),
        scratch_shapes=[pltpu.VMEM((tm, d), BF16)],
        compiler_params=_cparams("arbitrary", "arbitrary", "arbitrary"),
        name="post_mlp",
    )(x, a1, a2, w_out, mod, gain_f, mod, mod, mod, w1, w2)


def kernel(x_prompt, x_sample, c_prompt, c_sample, cache_k, cache_v, cache_logf, page_table, state_conv, state_gla,
           g_mix, g_ffn, w_ada, b_ada, w_in_even, b_f, g_q, g_k, w_conv, w_out_even, w_in_odd, w_gk, b_gk, g_o,
           w_out_odd, w_ff1, w_ff2):
    bp, seq, d = x_prompt.shape
    n_seq = x_sample.shape[0]
    depth = w_ada.shape[0]
    n_even, nh_fox = b_f.shape
    hd_fox = g_q.shape[1]
    cc = nh_fox * hd_fox
    _, n_odd, nh_gla, dk, dv = state_gla.shape
    qk, vv = nh_gla * dk, nh_gla * dv

    mod_all = _adaln(jnp.concatenate([c_prompt, c_sample], axis=0), w_ada, b_ada)

    cache_kt = jnp.transpose(cache_k, (0, 1, 3, 4, 2))
    cache_vt = jnp.transpose(cache_v, (0, 1, 3, 4, 2))
    cache_lft = jnp.swapaxes(cache_logf, 2, 3)

    xp = x_prompt
    xs = x_sample.reshape(1, n_seq, d)
    fox_p, fox_s, conv_p, conv_s, gla_p, gla_s = [], [], [], [], [], []
    for li in range(depth):
        mod_p = mod_all[li, :bp].reshape(bp, 1, 6 * d)
        mod_s = mod_all[li, bp:].reshape(1, n_seq, 6 * d)
        gm = g_mix[li].reshape(1, d)
        gf = g_ffn[li].reshape(1, d)
        w1 = w_ff1[li].astype(BF16)
        w2 = w_ff2[li].astype(BF16)
        if li % 2 == 0:
            e = li // 2
            wts = _even_weights(w_in_even[e], b_f[e], g_q[e], g_k[e], nh_fox, hd_fox)
            w_out = w_out_even[e].astype(BF16)
            ya, utail, qp, kp, va, k_rows, v_rows, lf_rows = _even_in_seq(
                xp, gm, mod_p, wts, w_conv[e], nh_fox, hd_fox, tm=TM_EVEN)
            o = _attn_prompt(qp, kp, va, nh_fox, hd_fox, tq=TQ_ATTN)
            xp = _post(xp, ya, 0, o, 0, w_out, gf, mod_p, w1, w2, tm=TM_POST, tf=TF_POST)
            fox_p.append((k_rows, v_rows, lf_rows))
            conv_p.append(utail[:, 6:8])
            prev0 = state_conv[:, e, 0].reshape(1, n_seq, cc)
            prev1 = state_conv[:, e, 1].reshape(1, n_seq, cc)
            ya, u, qn, kn, v, lfpad = _even_in_dec(xs, gm, mod_s, wts, w_conv[e], prev0, prev1, nh_fox, hd_fox)
            lf_new = lfpad[0, :, :nh_fox]
            o = _attn_decode(page_table, qn[0], kn[0], v[0], lf_new, cache_kt, cache_vt, cache_lft, e,
                             nh_fox, hd_fox)
            o = o.reshape(1, n_seq, cc).astype(BF16)
            xs = _post(xs, ya, 0, o, 0, w_out, gf, mod_s, w1, w2, tm=n_seq, tf=TF_POST)
            fox_s.append((kn[0], v[0], lf_new))
            conv_s.append(jnp.stack([prev1[0], u[0]], axis=1))
        else:
            oi = li // 2
            wts = _odd_weights(w_in_odd[oi], w_gk[oi], b_gk[oi], qk, vv)
            w_out = w_out_odd[oi].astype(BF16)
            q, k, v, sr, gl = _odd_in(xp, gm, mod_p, wts, qk, vv, dk, tm=TM_ODD)
            og, s_t = _gla_seq(q, k, gl, v, sr, g_o[oi], nh_gla, dk, dv, tt=TT_GLA)
            xp = _post(xp, og, 0, og, 1, w_out, gf, mod_p, w1, w2, tm=TM_POST, tf=TF_POST)
            gla_p.append(jnp.swapaxes(s_t, 2, 3))
            q, k, v, sr, gl = _odd_in(xs, gm, mod_s, wts, qk, vv, dk, tm=n_seq)
            og, s_new = _gla_dec(q[0], k[0], gl[0], v[0].astype(F32), sr[0], g_o[oi], state_gla, oi,
                                 nh_gla, dk, dv)
            og = og.reshape(1, n_seq, vv).astype(BF16)
            xs = _post(xs, og, 0, og, 1, w_out, gf, mod_s, w1, w2, tm=n_seq, tf=TF_POST)
            gla_s.append(s_new)

    heads = lambda a, n: a.reshape(a.shape[0], n, nh_fox, hd_fox)
    k_p = jnp.stack([heads(r[0], seq) for r in fox_p], axis=1)
    v_p = jnp.stack([heads(r[1], seq) for r in fox_p], axis=1)
    lf_p = jnp.stack([r[2] for r in fox_p], axis=1)
    k_s = jnp.stack([heads(r[0], 1) for r in fox_s], axis=1)
    v_s = jnp.stack([heads(r[1], 1) for r in fox_s], axis=1)
    lf_s = jnp.stack([r[2].reshape(n_seq, 1, nh_fox) for r in fox_s], axis=1)
    return (xp, xs.reshape(n_seq, 1, d), k_p, v_p, lf_p, k_s, v_s, lf_s,
            jnp.stack(conv_p, axis=1), jnp.stack(conv_s, axis=1),
            jnp.stack(gla_p, axis=1), jnp.stack(gla_s, axis=1))
```

```python
import functools

import numpy as np
import jax
import jax.numpy as jnp
from jax import lax
from jax.experimental import pallas as pl
from jax.experimental.pallas import tpu as pltpu

F32 = jnp.float32
BF16 = jnp.bfloat16

EPS = 1e-6
GLA_TAU = 16.0
GLA_CHUNK = 64
LANES = 128
VMEM_LIMIT = 56 * 1024 * 1024
NEG = -1e30
LOG2E = 1.4426950408889634
PAGES_PER_STEP = 16
ONES_ROWS = 16
TM_EVEN = 512
TM_ODD = 512
TM_POST = 1024
TF_POST = 1024
TQ_ATTN = 512
TT_GLA = 256


def _cparams(*sem):
    return pltpu.CompilerParams(dimension_semantics=sem, vmem_limit_bytes=VMEM_LIMIT)


def _silu(x):
    return x / (1.0 + jnp.exp(-x))


def _log_sigmoid(z):
    return jnp.minimum(z, 0.0) - jnp.log1p(jnp.exp(-jnp.abs(z)))


def _norm_mod(x, gain, shift, scale):
    y = x * lax.rsqrt(jnp.mean(x * x, axis=-1, keepdims=True) + EPS)
    return (y * gain) * (1.0 + scale) + shift


def _split3(a):
    hi = a.astype(BF16)
    r = a - hi.astype(F32)
    mid = r.astype(BF16)
    lo = (r - mid.astype(F32)).astype(BF16)
    return hi, mid, lo


def _dot(a, b):
    return jnp.dot(a, b, preferred_element_type=F32)


def _dot_nt(a, b):
    return lax.dot_general(a, b, (((1,), (1,)), ((), ())), preferred_element_type=F32)


def _dot_tn(a, b):
    return lax.dot_general(a, b, (((0,), (0,)), ((), ())), preferred_element_type=F32)


def _sum3(c, w):
    return c[:, 0:w] + c[:, w:2 * w] + c[:, 2 * w:3 * w]


def _adaln_kernel(c_ref, w_ref, b_ref, o_ref):
    a = _silu(c_ref[...]).astype(BF16)
    o_ref[0] = _dot(a, w_ref[0].astype(BF16)) + b_ref[0]


def _adaln(c_all, w_ada, b_ada):
    depth, d, n = w_ada.shape
    r = c_all.shape[0]
    tn = n // 4
    return pl.pallas_call(
        _adaln_kernel,
        grid=(depth, n // tn),
        in_specs=[pl.BlockSpec((r, d), lambda l, j: (0, 0)),
                  pl.BlockSpec((1, d, tn), lambda l, j: (l, 0, j)),
                  pl.BlockSpec((1, 1, tn), lambda l, j: (l, 0, j))],
        out_specs=pl.BlockSpec((1, r, tn), lambda l, j: (l, 0, j)),
        out_shape=jax.ShapeDtypeStruct((depth, r, n), F32),
        compiler_params=_cparams("arbitrary", "arbitrary"),
        name="adaln",
    )(c_all, w_ada, b_ada.reshape(depth, 1, n))


def _even_project(x_ref, g_ref, sh_ref, sc_ref, w_ref, wf_ref, bf_ref, gq_ref, gk_ref, p_ref, cc, nh):
    hb = _norm_mod(x_ref[0], g_ref[...], sh_ref[0], sc_ref[0]).astype(BF16)
    gb = _dot(hb, w_ref[:, 0:cc])
    u = _dot(hb, w_ref[:, cc:2 * cc]) * _dot(hb, w_ref[:, 2 * cc:3 * cc])
    q = _dot(hb, w_ref[:, 3 * cc:4 * cc])
    k = _dot(hb, w_ref[:, 4 * cc:5 * cc])
    v = _dot(hb, w_ref[:, 5 * cc:6 * cc])
    hd = cc // nh
    qms = _dot((q * q).astype(BF16), p_ref[...]) * (1.0 / hd)
    kms = _dot((k * k).astype(BF16), p_ref[...]) * (1.0 / hd)
    qn = (q * lax.rsqrt(qms + EPS)) * gq_ref[...]
    kn = (k * lax.rsqrt(kms + EPS)) * gk_ref[...]
    fz = _dot(hb, wf_ref[...]) + bf_ref[...]
    lane = lax.broadcasted_iota(jnp.int32, fz.shape, 1)
    lf = jnp.where(lane < nh, _log_sigmoid(fz), 0.0)
    return gb, u, qn, kn, v, lf


def _even_seq_kernel(x_ref, g_ref, sh_ref, sc_ref, w_ref, wf_ref, bf_ref, gq_ref, gk_ref, wc_ref, p_ref,
                     tri_ref, eq_ref, ek_ref, oq_ref, ok_ref, *rest, nh, cc, tm, aliased):
    if aliased:
        rest = rest[2:]
    ya_ref, ut_ref, qp_ref, kp_ref, va_ref, ko_ref, vo_ref, lf_ref, ubuf, fcarry = rest
    i = pl.program_id(1)
    gb, u, qn, kn, v, lf = _even_project(x_ref, g_ref, sh_ref, sc_ref, w_ref, wf_ref, bf_ref,
                                         gq_ref, gk_ref, p_ref, cc, nh)

    @pl.when(i == 0)
    def _():
        ubuf[0:8, :] = jnp.zeros((8, cc), F32)
        fcarry[...] = jnp.zeros_like(fcarry)

    ubuf[8:8 + tm, :] = u
    conv = wc_ref[0:1, :] * ubuf[6:6 + tm, :] + wc_ref[1:2, :] * ubuf[7:7 + tm, :] + wc_ref[2:3, :] * u
    ya_ref[0] = (gb * conv).astype(BF16)
    tail = ubuf[tm:tm + 8, :]
    ubuf[0:8, :] = tail
    ut_ref[0] = tail

    ko_ref[0, 0] = kn.T.reshape(nh, cc // nh, tm)
    vo_ref[0, 0] = v.T.reshape(nh, cc // nh, tm)
    lf_ref[0] = lf[:, 0:nh]
    vb = v.astype(BF16)
    for pair in range(nh // 2):
        va_ref[0, :, 2 * pair * LANES:(2 * pair + 1) * LANES] = vb[:, pair * LANES:(pair + 1) * LANES]
        va_ref[0, :, (2 * pair + 1) * LANES:(2 * pair + 2) * LANES] = jnp.ones((tm, LANES), BF16)

    c3 = _dot(tri_ref[...], jnp.concatenate(_split3(lf), axis=1))
    f = _sum3(c3, LANES) + fcarry[...]
    fcarry[...] = f[tm - 1:tm, :]
    f3 = jnp.concatenate(_split3(f * LOG2E), axis=1)

    qp_ref[0] = (_dot(f3, eq_ref[...]) + oq_ref[...]).astype(BF16)
    kp_ref[0] = (_dot(f3, ek_ref[...]) + ok_ref[...]).astype(BF16)
    qb = qn.astype(BF16)
    kb = kn.astype(BF16)
    hd = cc // nh
    for h in range(nh):
        qp_ref[0, :, h * LANES:h * LANES + hd] = qb[:, h * hd:(h + 1) * hd]
        kp_ref[0, :, h * LANES:h * LANES + hd] = kb[:, h * hd:(h + 1) * hd]


def _even_dec_kernel(x_ref, g_ref, sh_ref, sc_ref, w_ref, wf_ref, bf_ref, gq_ref, gk_ref, wc_ref, p_ref,
                     p0_ref, p1_ref,
                     ya_ref, u_ref, q_ref, k_ref, v_ref, lf_ref, *, nh, cc):
    gb, u, qn, kn, v, lf = _even_project(x_ref, g_ref, sh_ref, sc_ref, w_ref, wf_ref, bf_ref,
                                         gq_ref, gk_ref, p_ref, cc, nh)
    conv = wc_ref[0:1, :] * p0_ref[0] + wc_ref[1:2, :] * p1_ref[0] + wc_ref[2:3, :] * u
    ya_ref[0] = (gb * conv).astype(BF16)
    u_ref[0] = u
    q_ref[0] = qn
    k_ref[0] = kn
    v_ref[0] = v
    lf_ref[0] = lf


def _even_consts(nh, hd, tm):
    cc = nh * hd
    p = np.kron(np.eye(nh), np.ones((hd, hd)))
    tri = np.tril(np.ones((tm, tm)))
    eq = np.zeros((3 * LANES, nh * LANES))
    ek = np.zeros((3 * LANES, nh * LANES))
    oq = np.zeros((1, nh * LANES))
    ok = np.zeros((1, nh * LANES))
    for h in range(nh):
        for part in range(3):
            eq[part * LANES + h, h * LANES + hd + part] = 1.0
            ek[part * LANES + h, h * LANES + hd + 3 + part] = -1.0
            oq[0, h * LANES + hd + 3 + part] = 1.0
            ok[0, h * LANES + hd + part] = 1.0
    bf = lambda a: jnp.asarray(a, BF16)
    return bf(p), bf(tri), bf(eq), bf(ek), jnp.asarray(oq, F32), jnp.asarray(ok, F32)


def _even_weights(w_in, b_f, g_q, g_k, nh, hd):
    n, d, _ = w_in.shape
    cc = nh * hd
    w_main = w_in[:, :, :6 * cc].astype(BF16)
    w_f = jnp.zeros((n, d, LANES), BF16).at[:, :, :nh].set(w_in[:, :, 6 * cc:].astype(BF16))
    bf = jnp.zeros((n, 1, LANES), F32).at[:, 0, :nh].set(b_f)
    gq = (jnp.tile(g_q, (1, nh)) * (hd ** -0.5)).reshape(n, 1, cc)
    gk = jnp.tile(g_k, (1, nh)).reshape(n, 1, cc)
    return w_main, w_f, bf, gq, gk


def _full(shape):
    nd = len(shape)
    return pl.BlockSpec(shape, lambda *_: (0,) * nd)


def _layer(a, idx):
    nd = a.ndim
    return pl.BlockSpec((None,) + a.shape[1:], lambda *_: (idx,) + (0,) * (nd - 1))


def _mod_spec(mod, li, col, d, nargs):
    r = mod.shape[2]
    if nargs == 1:
        return pl.BlockSpec((None, 1, r, d), lambda i: (li, 0, 0, col))
    if nargs == 2:
        return pl.BlockSpec((None, 1, r, d), lambda b, i: (li, b, 0, col))
    return pl.BlockSpec((None, 1, r, d), lambda b, i, f: (li, b, 0, col))


def _even_in_seq(x, gain, mod, wts, w_conv, li, kv_prev, nh, hd, tm):
    g, l, d = x.shape
    cc = nh * hd
    e = li // 2
    n_even = w_conv.shape[0]
    w_main, w_f, bf, gq, gk = wts
    gq = gq * LOG2E
    p, tri, eq, ek, oq, ok = _even_consts(nh, hd, tm)
    row = lambda b, i: (b, i, 0)
    in_specs = [pl.BlockSpec((1, tm, d), row), _layer(gain, li), _mod_spec(mod, li, 0, d, 2), _mod_spec(mod, li, 1, d, 2),
                _layer(w_main, e), _layer(w_f, e), _layer(bf, e), _layer(gq, e), _layer(gk, e), _layer(w_conv, e),
                _full(p.shape), _full(tri.shape), _full(eq.shape), _full(ek.shape), _full(oq.shape), _full(ok.shape)]
    args = [x, gain, mod, mod, w_main, w_f, bf, gq, gk, w_conv, p, tri, eq, ek, oq, ok]
    kt_shape = jax.ShapeDtypeStruct((g, n_even, nh, hd, l), F32)
    kt_spec = pl.BlockSpec((1, 1, nh, hd, tm), lambda b, i: (b, e, 0, 0, i))
    out_shape = [jax.ShapeDtypeStruct((g, l, cc), BF16),
                 jax.ShapeDtypeStruct((g, 8, cc), F32),
                 jax.ShapeDtypeStruct((g, l, nh * LANES), BF16),
                 jax.ShapeDtypeStruct((g, l, nh * LANES), BF16),
                 jax.ShapeDtypeStruct((g, l, nh * LANES), BF16),
                 kt_shape, kt_shape,
                 jax.ShapeDtypeStruct((g, l, nh), F32)]
    out_specs = [pl.BlockSpec((1, tm, cc), row), pl.BlockSpec((1, 8, cc), lambda b, i: (b, 0, 0)),
                 pl.BlockSpec((1, tm, nh * LANES), row), pl.BlockSpec((1, tm, nh * LANES), row),
                 pl.BlockSpec((1, tm, nh * LANES), row), kt_spec, kt_spec, pl.BlockSpec((1, tm, nh), row)]
    aliases = {}
    if kv_prev is not None:
        in_specs += [pl.BlockSpec(memory_space=pl.ANY)] * 2
        args += list(kv_prev)
        aliases = {len(args) - 2: 5, len(args) - 1: 6}
    return pl.pallas_call(
        functools.partial(_even_seq_kernel, nh=nh, cc=cc, tm=tm, aliased=kv_prev is not None),
        grid=(g, l // tm), in_specs=in_specs, out_specs=out_specs, out_shape=out_shape,
        scratch_shapes=[pltpu.VMEM((tm + 8, cc), F32), pltpu.VMEM((1, LANES), F32)],
        input_output_aliases=aliases,
        compiler_params=_cparams("arbitrary", "arbitrary"),
        name="even_in_seq",
    )(*args)


def _even_in_dec(x, gain, mod, wts, w_conv, li, prev0, prev1, nh, hd):
    g, l, d = x.shape
    cc = nh * hd
    e = li // 2
    w_main, w_f, bf, gq, gk = wts
    p = _even_consts(nh, hd, 8)[0]
    whole = lambda s: pl.BlockSpec(s, lambda i: (0, 0, 0))
    in_specs = [whole((1, l, d)), _layer(gain, li), _mod_spec(mod, li, 0, d, 1), _mod_spec(mod, li, 1, d, 1),
                _layer(w_main, e), _layer(w_f, e), _layer(bf, e), _layer(gq, e), _layer(gk, e), _layer(w_conv, e),
                _full(p.shape), whole((1, l, cc)), whole((1, l, cc))]
    out_shape = [jax.ShapeDtypeStruct((1, l, cc), BF16)] + [jax.ShapeDtypeStruct((1, l, cc), F32)] * 4 \
        + [jax.ShapeDtypeStruct((1, l, LANES), F32)]
    out_specs = [whole((1, l, cc))] * 5 + [whole((1, l, LANES))]
    return pl.pallas_call(
        functools.partial(_even_dec_kernel, nh=nh, cc=cc),
        grid=(1,), in_specs=in_specs, out_specs=out_specs, out_shape=out_shape,
        compiler_params=_cparams("arbitrary"),
        name="even_in_dec",
    )(x, gain, mod, mod, w_main, w_f, bf, gq, gk, w_conv, p, prev0, prev1)


def _attn_kernel(it_ref, jt_ref, qp_ref, kp_ref, va_ref, o_ref, m_sc, l_sc, acc_sc, s_sc, p_sc, *, nh, hd, tq):
    i = it_ref[pl.program_id(1)]
    j = jt_ref[pl.program_id(1)]
    reps = tq // LANES

    @pl.when(j == 0)
    def _():
        m_sc[...] = jnp.full(m_sc.shape, NEG, F32)
        l_sc[...] = jnp.zeros_like(l_sc)
        acc_sc[...] = jnp.zeros_like(acc_sc)

    def scores(h):
        s_sc[h % 2] = _dot_nt(qp_ref[0, :, h * LANES:(h + 1) * LANES], kp_ref[0, :, h * LANES:(h + 1) * LANES])

    def step(diagonal):
        scores(0)
        for h in range(nh):
            if h + 1 < nh:
                scores(h + 1)
            s = s_sc[h % 2]
            if diagonal:
                row = lax.broadcasted_iota(jnp.int32, s.shape, 0)
                col = lax.broadcasted_iota(jnp.int32, s.shape, 1)
                s = jnp.where(col <= row, s, NEG)
            m_old = m_sc[h]
            m_new = jnp.maximum(m_old, jnp.max(s, axis=1, keepdims=True))
            alpha = jnp.exp2(m_old - m_new)
            p_sc[h % 2] = jnp.exp2(s - jnp.concatenate([m_new] * reps, axis=1)).astype(BF16)
            pair = h // 2
            pv = _dot(p_sc[h % 2], va_ref[0, :, pair * 2 * LANES:(pair + 1) * 2 * LANES])
            acc_sc[h] = alpha * acc_sc[h] + pv[:, 0:LANES]
            l_sc[h] = alpha * l_sc[h] + pv[:, LANES:2 * LANES]
            m_sc[h] = m_new

    @pl.when(j < i)
    def _():
        step(False)

    @pl.when(j == i)
    def _():
        step(True)
        lane = lax.broadcasted_iota(jnp.int32, (tq, LANES), 1)
        for pair in range(nh // 2):
            o0 = acc_sc[2 * pair] / l_sc[2 * pair]
            o1 = acc_sc[2 * pair + 1] / l_sc[2 * pair + 1]
            o_ref[0, :, pair * LANES:(pair + 1) * LANES] = jnp.where(lane < hd, o0, o1).astype(BF16)


def _attn_prompt(qp, kp, va, nh, hd, tq):
    g, l, _ = qp.shape
    cc = nh * hd
    nq = l // tq
    pairs = [(i, j) for i in range(nq) for j in range(i + 1)]
    it = jnp.asarray([p[0] for p in pairs], jnp.int32)
    jt = jnp.asarray([p[1] for p in pairs], jnp.int32)
    qrow = lambda b, p, it, jt: (b, it[p], 0)
    krow = lambda b, p, it, jt: (b, jt[p], 0)
    grid_spec = pltpu.PrefetchScalarGridSpec(
        num_scalar_prefetch=2, grid=(g, len(pairs)),
        in_specs=[pl.BlockSpec((1, tq, nh * LANES), qrow), pl.BlockSpec((1, tq, nh * LANES), krow),
                  pl.BlockSpec((1, tq, nh * LANES), krow)],
        out_specs=pl.BlockSpec((1, tq, cc), qrow),
        scratch_shapes=[pltpu.VMEM((nh, tq, LANES), F32), pltpu.VMEM((nh, tq, LANES), F32),
                        pltpu.VMEM((nh, tq, LANES), F32), pltpu.VMEM((2, tq, tq), F32),
                        pltpu.VMEM((2, tq, tq), BF16)])
    return pl.pallas_call(
        functools.partial(_attn_kernel, nh=nh, hd=hd, tq=tq),
        grid_spec=grid_spec,
        out_shape=jax.ShapeDtypeStruct((g, l, cc), BF16),
        compiler_params=_cparams("arbitrary", "arbitrary"),
        name="attn_prompt",
    )(it, jt, qp, kp, va)


def _attn_dec_kernel(pt_ref, qcol_ref, kncol_ref, vncol_ref, lfnew_ref, uincl_ref, ones_ref, *rest, npg, nh):
    k_refs = rest[:npg]
    v_refs = rest[npg:2 * npg]
    lf_refs = rest[2 * npg:3 * npg]
    o_ref = rest[3 * npg]
    m_sc, l_sc, acc_sc, c_sc = rest[3 * npg + 1:]
    g = pl.program_id(1)

    @pl.when(g == 0)
    def _():
        m_sc[...] = jnp.full(m_sc.shape, NEG, F32)
        l_sc[...] = jnp.zeros_like(l_sc)
        acc_sc[...] = jnp.zeros_like(acc_sc)
        c_sc[...] = jnp.zeros_like(c_sc)

    parts = []
    for j in range(npg):
        hi, mid, lo = _split3(lf_refs[j][0, 0])
        parts += [hi.astype(F32), mid.astype(F32), lo.astype(F32), jnp.zeros((nh, LANES), F32)]
    parts = jnp.concatenate(parts, axis=0).astype(BF16)
    cum = _dot(parts, uincl_ref[...])
    tot = _dot(parts, ones_ref[...])
    fs = []
    c = c_sc[...]
    for j in range(npg):
        r = 4 * nh * j
        fs.append(c + (cum[r:r + nh] + cum[r + nh:r + 2 * nh] + cum[r + 2 * nh:r + 3 * nh]))
        c = c + (tot[r:r + nh] + tot[r + nh:r + 2 * nh] + tot[r + 2 * nh:r + 3 * nh])
    c_sc[...] = c

    for h in range(nh):
        qc = qcol_ref[0, h]
        rows = [jnp.sum(qc * k_refs[j][0, 0, h], axis=0, keepdims=True) - fs[j][h:h + 1, :] for j in range(npg)]
        m_old = m_sc[h]
        m_new = jnp.maximum(m_old, jnp.max(functools.reduce(jnp.maximum, rows), axis=1, keepdims=True))
        alpha = jnp.exp(m_old - m_new)
        ps = [jnp.exp(r - m_new) for r in rows]
        l_sc[h] = alpha * l_sc[h] + jnp.sum(functools.reduce(jnp.add, ps), axis=1, keepdims=True)
        acc = alpha * acc_sc[h]
        for j in range(npg):
            acc = acc + ps[j] * v_refs[j][0, 0, h]
        acc_sc[h] = acc
        m_sc[h] = m_new

    @pl.when(g == pl.num_programs(1) - 1)
    def _():
        lane = lax.broadcasted_iota(jnp.int32, (1, LANES), 1)
        for h in range(nh):
            f_new = c_sc[h:h + 1, :] + lfnew_ref[0, h:h + 1, :]
            s_new = jnp.sum(qcol_ref[0, h] * kncol_ref[0, h], axis=0, keepdims=True) - f_new
            s_new = jnp.where(lane == 0, s_new, NEG)
            m_old = m_sc[h]
            m_new = jnp.maximum(m_old, jnp.max(s_new, axis=1, keepdims=True))
            alpha = jnp.exp(m_old - m_new)
            p_new = jnp.exp(s_new - m_new)
            l_fin = alpha * l_sc[h] + jnp.sum(p_new, axis=1, keepdims=True)
            acc = (alpha * acc_sc[h] + p_new * vncol_ref[0, h]) / l_fin
            hi, mid, lo = _split3(acc)
            ones = ones_ref[0:ONES_ROWS, :]
            o = _dot_nt(ones, hi) + _dot_nt(ones, mid) + _dot_nt(ones, lo)
            o_ref[0, h:h + 1, :] = o[0:1, :]


def _attn_decode(page_table, q, knew, vnew, lfnew, cache_kt, cache_vt, cache_lft, e, nh, hd):
    n_seq, n_pages = page_table.shape
    page = cache_kt.shape[-1]
    assert page == LANES and n_pages % PAGES_PER_STEP == 0
    npg = PAGES_PER_STEP
    col = lambda a: jnp.broadcast_to(a.reshape(n_seq, nh, hd, 1), (n_seq, nh, hd, LANES))
    uincl = jnp.asarray(np.triu(np.ones((page, page))), BF16)
    ones = jnp.ones((LANES, LANES), BF16)

    def pg(j, nd):
        return lambda b, g, pt: (pt[b, g * npg + j], e) + (0,) * nd

    per_seq4 = lambda b, g, pt: (b, 0, 0, 0)
    per_seq3 = lambda b, g, pt: (b, 0, 0)
    const2 = lambda b, g, pt: (0, 0)
    in_specs = [pl.BlockSpec((1, nh, hd, LANES), per_seq4), pl.BlockSpec((1, nh, hd, LANES), per_seq4),
                pl.BlockSpec((1, nh, hd, LANES), per_seq4), pl.BlockSpec((1, nh, 1), per_seq3),
                pl.BlockSpec((page, page), const2), pl.BlockSpec((LANES, LANES), const2)]
    in_specs += [pl.BlockSpec((1, 1, nh, hd, page), pg(j, 3)) for j in range(npg)]
    in_specs += [pl.BlockSpec((1, 1, nh, hd, page), pg(j, 3)) for j in range(npg)]
    in_specs += [pl.BlockSpec((1, 1, nh, page), pg(j, 2)) for j in range(npg)]
    grid_spec = pltpu.PrefetchScalarGridSpec(
        num_scalar_prefetch=1, grid=(n_seq, n_pages // npg), in_specs=in_specs,
        out_specs=pl.BlockSpec((1, nh, hd), per_seq3),
        scratch_shapes=[pltpu.VMEM((nh, 1, LANES), F32), pltpu.VMEM((nh, 1, LANES), F32),
                        pltpu.VMEM((nh, hd, LANES), F32), pltpu.VMEM((nh, LANES), F32)])
    o = pl.pallas_call(
        functools.partial(_attn_dec_kernel, npg=npg, nh=nh),
        grid_spec=grid_spec,
        out_shape=jax.ShapeDtypeStruct((n_seq, nh, hd), F32),
        compiler_params=_cparams("arbitrary", "arbitrary"),
        name="attn_decode",
    )(page_table, col(q), col(knew), col(vnew), lfnew.reshape(n_seq, nh, 1), uincl, ones,
      *([cache_kt] * npg), *([cache_vt] * npg), *([cache_lft] * npg))
    return o.reshape(n_seq, 1, nh * hd)


def _odd_in_kernel(x_ref, g_ref, sh_ref, sc_ref, w_ref, wgd_ref, wgk_ref, bgk_ref,
                   q_ref, k_ref, v_ref, sr_ref, gl_ref, *, qk, vv, dk):
    hb = _norm_mod(x_ref[0], g_ref[...], sh_ref[0], sc_ref[0]).astype(BF16)
    q_ref[0] = _dot(hb, w_ref[:, 0:qk]) * (dk ** -0.5)
    k_ref[0] = _dot(hb, w_ref[:, qk:2 * qk])
    v_ref[0] = _dot(hb, w_ref[:, 2 * qk:2 * qk + vv]).astype(BF16)
    sr_ref[0] = _silu(_dot(hb, w_ref[:, 2 * qk + vv:2 * qk + 2 * vv]))
    gd = _dot(hb, wgd_ref[...]).astype(BF16)
    gl_ref[0] = _log_sigmoid(_dot(gd, wgk_ref[...]) + bgk_ref[...]) * (1.0 / GLA_TAU)


def _odd_weights(w_in, w_gk, b_gk, qk, vv):
    n, d, _ = w_in.shape
    rank = w_gk.shape[1]
    w_main = w_in[:, :, :2 * qk + 2 * vv].astype(BF16)
    w_gd = jnp.zeros((n, d, LANES), BF16).at[:, :, :rank].set(w_in[:, :, 2 * qk + 2 * vv:].astype(BF16))
    wgk = jnp.zeros((n, LANES, qk), BF16).at[:, :rank].set(w_gk.astype(BF16))
    return w_main, w_gd, wgk, b_gk.reshape(n, 1, qk)


def _odd_in(x, gain, mod, wts, li, qk, vv, dk, tm):
    g, l, d = x.shape
    oi = li // 2
    w_main, w_gd, wgk, bgk = wts
    row = lambda b, i: (b, i, 0)
    in_specs = [pl.BlockSpec((1, tm, d), row), _layer(gain, li), _mod_spec(mod, li, 0, d, 2), _mod_spec(mod, li, 1, d, 2),
                _layer(w_main, oi), _layer(w_gd, oi), _layer(wgk, oi), _layer(bgk, oi)]
    out_shape = [jax.ShapeDtypeStruct((g, l, qk), F32), jax.ShapeDtypeStruct((g, l, qk), F32),
                 jax.ShapeDtypeStruct((g, l, vv), BF16), jax.ShapeDtypeStruct((g, l, vv), F32),
                 jax.ShapeDtypeStruct((g, l, qk), F32)]
    out_specs = [pl.BlockSpec((1, tm, qk), row), pl.BlockSpec((1, tm, qk), row), pl.BlockSpec((1, tm, vv), row),
                 pl.BlockSpec((1, tm, vv), row), pl.BlockSpec((1, tm, qk), row)]
    return pl.pallas_call(
        functools.partial(_odd_in_kernel, qk=qk, vv=vv, dk=dk),
        grid=(g, l // tm), in_specs=in_specs, out_specs=out_specs, out_shape=out_shape,
        compiler_params=_cparams("arbitrary", "arbitrary"),
        name="odd_in",
    )(x, gain, mod, mod, w_main, w_gd, wgk, bgk)


def _gla_seq_kernel(q_ref, k_ref, g_ref, v_ref, sr_ref, go_ref, tri_ref, o_ref, st_ref, s_sc, *, nh, dk, dv, tt):
    t = pl.program_id(1)
    c = GLA_CHUNK

    @pl.when(t == 0)
    def _():
        s_sc[...] = jnp.zeros_like(s_sc)

    qk = nh * dk
    b_all = _sum3(_dot(tri_ref[...], jnp.concatenate(_split3(g_ref[0]), axis=1)), qk)
    row = lax.broadcasted_iota(jnp.int32, (c, c), 0)
    col = lax.broadcasted_iota(jnp.int32, (c, c), 1)
    causal = col <= row
    for n in range(tt // c):
        rows = slice(n * c, (n + 1) * c)
        b = b_all[rows]
        b_last = b[c - 1:c, :]
        q = q_ref[0, rows, :]
        k = k_ref[0, rows, :]
        q_e = (q * jnp.exp(b)).astype(BF16)
        k_e = (k * jnp.exp(-b)).astype(BF16)
        k_d = (k * jnp.exp(b_last - b)).astype(BF16)
        decay = jnp.exp(b_last)
        for h in range(nh):
            kc = slice(h * dk, (h + 1) * dk)
            vc = slice(h * dv, (h + 1) * dv)
            v = v_ref[0, rows, vc]
            a = jnp.where(causal, _dot_nt(q_e[:, kc], k_e[:, kc]), 0.0)
            s_t = s_sc[h]
            o = _dot(a.astype(BF16), v) + _dot_nt(q_e[:, kc], s_t.astype(BF16))
            s_sc[h] = s_t * decay[:, kc] + _dot_tn(v, k_d[:, kc])
            y = o * lax.rsqrt(jnp.mean(o * o, axis=-1, keepdims=True) + EPS) * go_ref[...]
            o_ref[0, rows, vc] = (y * sr_ref[0, rows, vc]).astype(BF16)
    st_ref[0] = s_sc[...]


def _gla_seq(q, k, gl, v, sr, g_o, nh, dk, dv, tt):
    g, l, qk = q.shape
    vv = nh * dv
    tri = jnp.asarray(np.kron(np.eye(tt // GLA_CHUNK), np.tril(np.ones((GLA_CHUNK, GLA_CHUNK)))), BF16)
    row = lambda b, i: (b, i, 0)
    return pl.pallas_call(
        functools.partial(_gla_seq_kernel, nh=nh, dk=dk, dv=dv, tt=tt),
        grid=(g, l // tt),
        in_specs=[pl.BlockSpec((1, tt, qk), row), pl.BlockSpec((1, tt, qk), row), pl.BlockSpec((1, tt, qk), row),
                  pl.BlockSpec((1, tt, vv), row), pl.BlockSpec((1, tt, vv), row), _full((1, dv)), _full((tt, tt))],
        out_specs=[pl.BlockSpec((1, tt, vv), row), pl.BlockSpec((1, nh, dv, dk), lambda b, i: (b, 0, 0, 0))],
        out_shape=[jax.ShapeDtypeStruct((g, l, vv), BF16), jax.ShapeDtypeStruct((g, nh, dv, dk), F32)],
        scratch_shapes=[pltpu.VMEM((nh, dv, dk), F32)],
        compiler_params=_cparams("arbitrary", "arbitrary"),
        name="gla_seq",
    )(q, k, gl, v, sr, g_o.reshape(1, dv), tri)


def _gla_dec_kernel(q_ref, k_ref, g_ref, v_ref, sr_ref, go_ref, s0_ref, *rest, nh, dk, dv, aliased):
    o_ref, s_ref = rest[1:] if aliased else rest

    def column(r):
        return jnp.broadcast_to(r, (dk, dk)).T

    for h in range(nh):
        kc = slice(h * dk, (h + 1) * dk)
        dcol = jnp.exp(column(g_ref[0, :, kc]))
        kcol = column(k_ref[0, :, kc])
        qcol = column(q_ref[0, :, kc])
        halves = []
        for half in range(dv // dk):
            lanes = slice(half * dk, (half + 1) * dk)
            v = v_ref[0, :, h * dv + half * dk:h * dv + (half + 1) * dk].astype(F32)
            s_new = dcol * s0_ref[0, 0, h, :, lanes] + kcol * v
            s_ref[0, h, :, lanes] = s_new
            halves.append(jnp.sum(qcol * s_new, axis=0, keepdims=True))
        o = jnp.concatenate(halves, axis=1)
        y = o * lax.rsqrt(jnp.mean(o * o, axis=-1, keepdims=True) + EPS) * go_ref[...]
        vc = slice(h * dv, (h + 1) * dv)
        o_ref[0, :, vc] = y * sr_ref[0, :, vc]


def _gla_dec(q, k, gl, v, sr, g_o, state_gla, o_idx, s_prev, nh, dk, dv):
    n_seq = q.shape[0]
    qk, vv = nh * dk, nh * dv
    r3 = lambda a: a.reshape(n_seq, 1, a.shape[-1])
    per_seq = lambda b: (b, 0, 0)
    in_specs = [pl.BlockSpec((1, 1, qk), per_seq), pl.BlockSpec((1, 1, qk), per_seq),
                pl.BlockSpec((1, 1, qk), per_seq), pl.BlockSpec((1, 1, vv), per_seq),
                pl.BlockSpec((1, 1, vv), per_seq), pl.BlockSpec((1, dv), lambda b: (0, 0)),
                pl.BlockSpec((1, 1, nh, dk, dv), lambda b: (b, o_idx, 0, 0, 0))]
    args = [r3(q), r3(k), r3(gl), r3(v), r3(sr), g_o.reshape(1, dv), state_gla]
    aliases = {}
    if s_prev is not None:
        in_specs.append(pl.BlockSpec(memory_space=pl.ANY))
        args.append(s_prev)
        aliases = {len(args) - 1: 1}
    return pl.pallas_call(
        functools.partial(_gla_dec_kernel, nh=nh, dk=dk, dv=dv, aliased=s_prev is not None),
        grid=(n_seq,),
        in_specs=in_specs,
        out_specs=[pl.BlockSpec((1, 1, vv), per_seq),
                   pl.BlockSpec((1, None, nh, dk, dv), lambda b: (b, o_idx, 0, 0, 0))],
        out_shape=[jax.ShapeDtypeStruct((n_seq, 1, vv), F32), jax.ShapeDtypeStruct(state_gla.shape, F32)],
        input_output_aliases=aliases,
        compiler_params=_cparams("arbitrary"),
        name="gla_dec",
    )(*args)


def _post_kernel(x_ref, a1_ref, a2_ref, wo_ref, gtm_ref, gf_ref, shf_ref, scf_ref, gtf_ref, w1_ref, w2_ref,
                 o_ref, h_sc, *, ka):
    f = pl.program_id(2)

    @pl.when(f == 0)
    def _():
        m = _dot(a1_ref[0], wo_ref[0:ka, :]) + _dot(a2_ref[0], wo_ref[ka:2 * ka, :])
        x1 = x_ref[0] + gtm_ref[0] * m
        o_ref[0] = x1
        h_sc[...] = _norm_mod(x1, gf_ref[...], shf_ref[0], scf_ref[0]).astype(BF16)

    t = jnp.maximum(_dot(h_sc[...], w1_ref[...]), 0.0)
    o_ref[0] += gtf_ref[0] * _dot((t * t).astype(BF16), w2_ref[...])


def _post(x, a1, a1_col, a2, a2_col, w_out, gain_f, mod, w1, w2, li, tm, tf):
    g, l, d = x.shape
    ka = w_out.shape[1] // 2
    dff = w1.shape[2]
    row = lambda b, i, f: (b, i, 0)
    modc = lambda c: _mod_spec(mod, li, c, d, 3)
    return pl.pallas_call(
        functools.partial(_post_kernel, ka=ka),
        grid=(g, l // tm, dff // tf),
        in_specs=[pl.BlockSpec((1, tm, d), row),
                  pl.BlockSpec((1, tm, ka), lambda b, i, f: (b, i, a1_col)),
                  pl.BlockSpec((1, tm, ka), lambda b, i, f: (b, i, a2_col)),
                  _layer(w_out, li // 2), modc(2), _layer(gain_f, li), modc(3), modc(4), modc(5),
                  pl.BlockSpec((None, d, tf), lambda b, i, f: (li, 0, f)),
                  pl.BlockSpec((None, tf, d), lambda b, i, f: (li, f, 0))],
        out_specs=pl.BlockSpec((1, tm, d), row),
        out_shape=jax.ShapeDtypeStruct((g, l, d), F32),
        scratch_shapes=[pltpu.VMEM((tm, d), BF16)],
        compiler_params=_cparams("arbitrary", "arbitrary", "arbitrary"),
        name="post_mlp",
    )(x, a1, a2, w_out, mod, gain_f, mod, mod, mod, w1, w2)


def kernel(x_prompt, x_sample, c_prompt, c_sample, cache_k, cache_v, cache_logf, page_table, state_conv, state_gla,
           g_mix, g_ffn, w_ada, b_ada, w_in_even, b_f, g_q, g_k, w_conv, w_out_even, w_in_odd, w_gk, b_gk, g_o,
           w_out_odd, w_ff1, w_ff2):
    bp, seq, d = x_prompt.shape
    n_seq = x_sample.shape[0]
    depth = w_ada.shape[0]
    n_even, nh_fox = b_f.shape
    hd_fox = g_q.shape[1]
    cc = nh_fox * hd_fox
    _, n_odd, nh_gla, dk, dv = state_gla.shape
    qk, vv = nh_gla * dk, nh_gla * dv

    mod_all = _adaln(jnp.concatenate([c_prompt, c_sample], axis=0), w_ada, b_ada)

    cache_kt = jnp.transpose(cache_k, (0, 1, 3, 4, 2))
    cache_vt = jnp.transpose(cache_v, (0, 1, 3, 4, 2))
    cache_lft = jnp.swapaxes(cache_logf, 2, 3)

    mod_p = mod_all[:, :bp].reshape(depth, bp, 1, 6 * d)
    mod_s = mod_all[:, bp:].reshape(depth, 1, n_seq, 6 * d)
    gm = g_mix.reshape(depth, 1, d)
    gf = g_ffn.reshape(depth, 1, d)
    w1 = w_ff1.astype(BF16)
    w2 = w_ff2.astype(BF16)
    wts_even = _even_weights(w_in_even, b_f, g_q, g_k, nh_fox, hd_fox)
    wts_odd = _odd_weights(w_in_odd, w_gk, b_gk, qk, vv)
    wo_even = w_out_even.astype(BF16)
    wo_odd = w_out_odd.astype(BF16)

    xp = x_prompt
    xs = x_sample.reshape(1, n_seq, d)
    kv_rows = None
    gla_s = None
    lf_rows_p, fox_s, conv_p, conv_s, gla_p = [], [], [], [], []
    for li in range(depth):
        if li % 2 == 0:
            e = li // 2
            ya, utail, qp, kp, va, kt_rows, vt_rows, lf_rows = _even_in_seq(
                xp, gm, mod_p, wts_even, w_conv, li, kv_rows, nh_fox, hd_fox, tm=TM_EVEN)
            kv_rows = (kt_rows, vt_rows)
            o = _attn_prompt(qp, kp, va, nh_fox, hd_fox, tq=TQ_ATTN)
            xp = _post(xp, ya, 0, o, 0, wo_even, gf, mod_p, w1, w2, li, tm=TM_POST, tf=TF_POST)
            lf_rows_p.append(lf_rows)
            conv_p.append(utail[:, 6:8])
            prev0 = state_conv[:, e, 0].reshape(1, n_seq, cc)
            prev1 = state_conv[:, e, 1].reshape(1, n_seq, cc)
            ya, u, qn, kn, v, lfpad = _even_in_dec(xs, gm, mod_s, wts_even, w_conv, li, prev0, prev1, nh_fox, hd_fox)
            lf_new = lfpad[0, :, :nh_fox]
            o = _attn_decode(page_table, qn[0], kn[0], v[0], lf_new, cache_kt, cache_vt, cache_lft, e,
                             nh_fox, hd_fox)
            o = o.reshape(1, n_seq, cc).astype(BF16)
            xs = _post(xs, ya, 0, o, 0, wo_even, gf, mod_s, w1, w2, li, tm=n_seq, tf=TF_POST)
            fox_s.append((kn[0], v[0], lf_new))
            conv_s.append(jnp.stack([prev1[0], u[0]], axis=1))
        else:
            oi = li // 2
            q, k, v, sr, gl = _odd_in(xp, gm, mod_p, wts_odd, li, qk, vv, dk, tm=TM_ODD)
            og, s_t = _gla_seq(q, k, gl, v, sr, g_o[oi], nh_gla, dk, dv, tt=TT_GLA)
            xp = _post(xp, og, 0, og, 1, wo_odd, gf, mod_p, w1, w2, li, tm=TM_POST, tf=TF_POST)
            gla_p.append(jnp.swapaxes(s_t, 2, 3))
            q, k, v, sr, gl = _odd_in(xs, gm, mod_s, wts_odd, li, qk, vv, dk, tm=n_seq)
            og, gla_s = _gla_dec(q[0], k[0], gl[0], v[0].astype(F32), sr[0], g_o[oi], state_gla, oi, gla_s,
                                 nh_gla, dk, dv)
            og = og.reshape(1, n_seq, vv).astype(BF16)
            xs = _post(xs, og, 0, og, 1, wo_odd, gf, mod_s, w1, w2, li, tm=n_seq, tf=TF_POST)

    heads = lambda a, n: a.reshape(a.shape[0], n, nh_fox, hd_fox)
    k_p = jnp.transpose(kv_rows[0], (0, 1, 4, 2, 3))
    v_p = jnp.transpose(kv_rows[1], (0, 1, 4, 2, 3))
    lf_p = jnp.stack(lf_rows_p, axis=1)
    k_s = jnp.stack([heads(r[0], 1) for r in fox_s], axis=1)
    v_s = jnp.stack([heads(r[1], 1) for r in fox_s], axis=1)
    lf_s = jnp.stack([r[2].reshape(n_seq, 1, nh_fox) for r in fox_s], axis=1)
    return (xp, xs.reshape(n_seq, 1, d), k_p, v_p, lf_p, k_s, v_s, lf_s,
            jnp.stack(conv_p, axis=1), jnp.stack(conv_s, axis=1),
            jnp.stack(gla_p, axis=1), gla_s)
```

```python
import functools

import numpy as np
import jax
import jax.numpy as jnp
from jax import lax
from jax.experimental import pallas as pl
from jax.experimental.pallas import tpu as pltpu

F32 = jnp.float32
BF16 = jnp.bfloat16

EPS = 1e-6
GLA_TAU = 16.0
GLA_CHUNK = 64
LANES = 128
VMEM_LIMIT = 56 * 1024 * 1024
NEG = -1e30
LOG2E = 1.4426950408889634
PAGES_PER_STEP = 16
ONES_ROWS = 16
TM_EVEN = 512
TM_ODD = 512
TM_POST = 1024
TF_POST = 1024
TQ_ATTN = 512
TT_GLA = 256


def _cparams(*sem):
    return pltpu.CompilerParams(dimension_semantics=sem, vmem_limit_bytes=VMEM_LIMIT)


def _silu(x):
    return x / (1.0 + jnp.exp(-x))


def _log_sigmoid(z):
    return jnp.minimum(z, 0.0) - jnp.log1p(jnp.exp(-jnp.abs(z)))


def _norm_mod(x, gain, shift, scale):
    y = x * lax.rsqrt(jnp.mean(x * x, axis=-1, keepdims=True) + EPS)
    return (y * gain) * (1.0 + scale) + shift


def _split3(a):
    hi = a.astype(BF16)
    r = a - hi.astype(F32)
    mid = r.astype(BF16)
    lo = (r - mid.astype(F32)).astype(BF16)
    return hi, mid, lo


def _dot(a, b):
    return jnp.dot(a, b, preferred_element_type=F32)


def _dot_nt(a, b):
    return lax.dot_general(a, b, (((1,), (1,)), ((), ())), preferred_element_type=F32)


def _dot_tn(a, b):
    return lax.dot_general(a, b, (((0,), (0,)), ((), ())), preferred_element_type=F32)


def _sum3(c, w):
    return c[:, 0:w] + c[:, w:2 * w] + c[:, 2 * w:3 * w]


def _adaln_kernel(c_ref, w_ref, b_ref, o_ref):
    a = _silu(c_ref[...]).astype(BF16)
    o_ref[0] = _dot(a, w_ref[0].astype(BF16)) + b_ref[0]


def _adaln(c_all, w_ada, b_ada):
    depth, d, n = w_ada.shape
    r = c_all.shape[0]
    tn = n // 4
    return pl.pallas_call(
        _adaln_kernel,
        grid=(depth, n // tn),
        in_specs=[pl.BlockSpec((r, d), lambda l, j: (0, 0)),
                  pl.BlockSpec((1, d, tn), lambda l, j: (l, 0, j)),
                  pl.BlockSpec((1, 1, tn), lambda l, j: (l, 0, j))],
        out_specs=pl.BlockSpec((1, r, tn), lambda l, j: (l, 0, j)),
        out_shape=jax.ShapeDtypeStruct((depth, r, n), F32),
        compiler_params=_cparams("arbitrary", "arbitrary"),
        name="adaln",
    )(c_all, w_ada, b_ada.reshape(depth, 1, n))


def _even_project(x_ref, g_ref, sh_ref, sc_ref, w_ref, wf_ref, bf_ref, gq_ref, gk_ref, p_ref, cc, nh):
    hb = _norm_mod(x_ref[0], g_ref[...], sh_ref[0], sc_ref[0]).astype(BF16)
    gb = _dot(hb, w_ref[:, 0:cc])
    u = _dot(hb, w_ref[:, cc:2 * cc]) * _dot(hb, w_ref[:, 2 * cc:3 * cc])
    q = _dot(hb, w_ref[:, 3 * cc:4 * cc])
    k = _dot(hb, w_ref[:, 4 * cc:5 * cc])
    v = _dot(hb, w_ref[:, 5 * cc:6 * cc])
    hd = cc // nh
    qms = _dot((q * q).astype(BF16), p_ref[...]) * (1.0 / hd)
    kms = _dot((k * k).astype(BF16), p_ref[...]) * (1.0 / hd)
    qn = (q * lax.rsqrt(qms + EPS)) * gq_ref[...]
    kn = (k * lax.rsqrt(kms + EPS)) * gk_ref[...]
    fz = _dot(hb, wf_ref[...]) + bf_ref[...]
    lane = lax.broadcasted_iota(jnp.int32, fz.shape, 1)
    lf = jnp.where(lane < nh, _log_sigmoid(fz), 0.0)
    return gb, u, qn, kn, v, lf


def _even_seq_kernel(x_ref, g_ref, sh_ref, sc_ref, w_ref, wf_ref, bf_ref, gq_ref, gk_ref, wc_ref, p_ref,
                     tri_ref, eq_ref, ek_ref, oq_ref, ok_ref, *rest, nh, cc, tm, aliased):
    if aliased:
        rest = rest[2:]
    ya_ref, ut_ref, qp_ref, kp_ref, va_ref, ko_ref, vo_ref, lf_ref, ubuf, fcarry = rest
    i = pl.program_id(1)
    gb, u, qn, kn, v, lf = _even_project(x_ref, g_ref, sh_ref, sc_ref, w_ref, wf_ref, bf_ref,
                                         gq_ref, gk_ref, p_ref, cc, nh)

    @pl.when(i == 0)
    def _():
        ubuf[0:8, :] = jnp.zeros((8, cc), F32)
        fcarry[...] = jnp.zeros_like(fcarry)

    ubuf[8:8 + tm, :] = u
    conv = wc_ref[0:1, :] * ubuf[6:6 + tm, :] + wc_ref[1:2, :] * ubuf[7:7 + tm, :] + wc_ref[2:3, :] * u
    ya_ref[0] = (gb * conv).astype(BF16)
    tail = ubuf[tm:tm + 8, :]
    ubuf[0:8, :] = tail
    ut_ref[0] = tail

    ko_ref[0, 0] = kn.T.reshape(nh, cc // nh, tm)
    vo_ref[0, 0] = v.T.reshape(nh, cc // nh, tm)
    lf_ref[0] = lf[:, 0:nh]
    vb = v.astype(BF16)
    va_ref[0] = jnp.ones((tm, nh * LANES), BF16)
    for h in range(nh):
        va_ref[0, :, h * LANES:h * LANES + cc // nh] = vb[:, h * (cc // nh):(h + 1) * (cc // nh)]

    c3 = _dot(tri_ref[...], jnp.concatenate(_split3(lf), axis=1))
    f = _sum3(c3, LANES) + fcarry[...]
    fcarry[...] = f[tm - 1:tm, :]
    f3 = jnp.concatenate(_split3(f * LOG2E), axis=1)

    qp_ref[0] = (_dot(f3, eq_ref[...]) + oq_ref[...]).astype(BF16)
    kp_ref[0] = (_dot(f3, ek_ref[...]) + ok_ref[...]).astype(BF16)
    qb = qn.astype(BF16)
    kb = kn.astype(BF16)
    hd = cc // nh
    for h in range(nh):
        qp_ref[0, :, h * LANES:h * LANES + hd] = qb[:, h * hd:(h + 1) * hd]
        kp_ref[0, :, h * LANES:h * LANES + hd] = kb[:, h * hd:(h + 1) * hd]


def _even_dec_kernel(x_ref, g_ref, sh_ref, sc_ref, w_ref, wf_ref, bf_ref, gq_ref, gk_ref, wc_ref, p_ref,
                     p0_ref, p1_ref,
                     ya_ref, u_ref, q_ref, k_ref, v_ref, lf_ref, *, nh, cc):
    gb, u, qn, kn, v, lf = _even_project(x_ref, g_ref, sh_ref, sc_ref, w_ref, wf_ref, bf_ref,
                                         gq_ref, gk_ref, p_ref, cc, nh)
    conv = wc_ref[0:1, :] * p0_ref[0] + wc_ref[1:2, :] * p1_ref[0] + wc_ref[2:3, :] * u
    ya_ref[0] = (gb * conv).astype(BF16)
    u_ref[0] = u
    q_ref[0] = qn
    k_ref[0] = kn
    v_ref[0] = v
    lf_ref[0] = lf


def _even_consts(nh, hd, tm):
    cc = nh * hd
    p = np.kron(np.eye(nh), np.ones((hd, hd)))
    tri = np.tril(np.ones((tm, tm)))
    eq = np.zeros((3 * LANES, nh * LANES))
    ek = np.zeros((3 * LANES, nh * LANES))
    oq = np.zeros((1, nh * LANES))
    ok = np.zeros((1, nh * LANES))
    for h in range(nh):
        for part in range(3):
            eq[part * LANES + h, h * LANES + hd + part] = 1.0
            ek[part * LANES + h, h * LANES + hd + 3 + part] = -1.0
            oq[0, h * LANES + hd + 3 + part] = 1.0
            ok[0, h * LANES + hd + part] = 1.0
    bf = lambda a: jnp.asarray(a, BF16)
    return bf(p), bf(tri), bf(eq), bf(ek), jnp.asarray(oq, F32), jnp.asarray(ok, F32)


def _even_weights(w_in, b_f, g_q, g_k, nh, hd):
    n, d, _ = w_in.shape
    cc = nh * hd
    w_main = w_in[:, :, :6 * cc].astype(BF16)
    w_f = jnp.zeros((n, d, LANES), BF16).at[:, :, :nh].set(w_in[:, :, 6 * cc:].astype(BF16))
    bf = jnp.zeros((n, 1, LANES), F32).at[:, 0, :nh].set(b_f)
    gq = (jnp.tile(g_q, (1, nh)) * (hd ** -0.5)).reshape(n, 1, cc)
    gk = jnp.tile(g_k, (1, nh)).reshape(n, 1, cc)
    return w_main, w_f, bf, gq, gk


def _full(shape):
    nd = len(shape)
    return pl.BlockSpec(shape, lambda *_: (0,) * nd)


def _layer(a, idx):
    nd = a.ndim
    return pl.BlockSpec((None,) + a.shape[1:], lambda *_: (idx,) + (0,) * (nd - 1))


def _mod_spec(mod, li, col, d, nargs):
    r = mod.shape[2]
    if nargs == 1:
        return pl.BlockSpec((None, 1, r, d), lambda i: (li, 0, 0, col))
    if nargs == 2:
        return pl.BlockSpec((None, 1, r, d), lambda b, i: (li, b, 0, col))
    return pl.BlockSpec((None, 1, r, d), lambda b, i, f: (li, b, 0, col))


def _even_in_seq(x, gain, mod, wts, w_conv, li, kv_prev, nh, hd, tm):
    g, l, d = x.shape
    cc = nh * hd
    e = li // 2
    n_even = w_conv.shape[0]
    w_main, w_f, bf, gq, gk = wts
    gq = gq * LOG2E
    p, tri, eq, ek, oq, ok = _even_consts(nh, hd, tm)
    row = lambda b, i: (b, i, 0)
    in_specs = [pl.BlockSpec((1, tm, d), row), _layer(gain, li), _mod_spec(mod, li, 0, d, 2), _mod_spec(mod, li, 1, d, 2),
                _layer(w_main, e), _layer(w_f, e), _layer(bf, e), _layer(gq, e), _layer(gk, e), _layer(w_conv, e),
                _full(p.shape), _full(tri.shape), _full(eq.shape), _full(ek.shape), _full(oq.shape), _full(ok.shape)]
    args = [x, gain, mod, mod, w_main, w_f, bf, gq, gk, w_conv, p, tri, eq, ek, oq, ok]
    kt_shape = jax.ShapeDtypeStruct((g, n_even, nh, hd, l), F32)
    kt_spec = pl.BlockSpec((1, 1, nh, hd, tm), lambda b, i: (b, e, 0, 0, i))
    out_shape = [jax.ShapeDtypeStruct((g, l, cc), BF16),
                 jax.ShapeDtypeStruct((g, 8, cc), F32),
                 jax.ShapeDtypeStruct((g, l, nh * LANES), BF16),
                 jax.ShapeDtypeStruct((g, l, nh * LANES), BF16),
                 jax.ShapeDtypeStruct((g, l, nh * LANES), BF16),
                 kt_shape, kt_shape,
                 jax.ShapeDtypeStruct((g, l, nh), F32)]
    out_specs = [pl.BlockSpec((1, tm, cc), row), pl.BlockSpec((1, 8, cc), lambda b, i: (b, 0, 0)),
                 pl.BlockSpec((1, tm, nh * LANES), row), pl.BlockSpec((1, tm, nh * LANES), row),
                 pl.BlockSpec((1, tm, nh * LANES), row), kt_spec, kt_spec, pl.BlockSpec((1, tm, nh), row)]
    aliases = {}
    if kv_prev is not None:
        in_specs += [pl.BlockSpec(memory_space=pl.ANY)] * 2
        args += list(kv_prev)
        aliases = {len(args) - 2: 5, len(args) - 1: 6}
    return pl.pallas_call(
        functools.partial(_even_seq_kernel, nh=nh, cc=cc, tm=tm, aliased=kv_prev is not None),
        grid=(g, l // tm), in_specs=in_specs, out_specs=out_specs, out_shape=out_shape,
        scratch_shapes=[pltpu.VMEM((tm + 8, cc), F32), pltpu.VMEM((1, LANES), F32)],
        input_output_aliases=aliases,
        compiler_params=_cparams("arbitrary", "arbitrary"),
        name="even_in_seq",
    )(*args)


def _even_in_dec(x, gain, mod, wts, w_conv, li, prev0, prev1, nh, hd):
    g, l, d = x.shape
    cc = nh * hd
    e = li // 2
    w_main, w_f, bf, gq, gk = wts
    p = _even_consts(nh, hd, 8)[0]
    whole = lambda s: pl.BlockSpec(s, lambda i: (0, 0, 0))
    in_specs = [whole((1, l, d)), _layer(gain, li), _mod_spec(mod, li, 0, d, 1), _mod_spec(mod, li, 1, d, 1),
                _layer(w_main, e), _layer(w_f, e), _layer(bf, e), _layer(gq, e), _layer(gk, e), _layer(w_conv, e),
                _full(p.shape), whole((1, l, cc)), whole((1, l, cc))]
    out_shape = [jax.ShapeDtypeStruct((1, l, cc), BF16)] + [jax.ShapeDtypeStruct((1, l, cc), F32)] * 4 \
        + [jax.ShapeDtypeStruct((1, l, LANES), F32)]
    out_specs = [whole((1, l, cc))] * 5 + [whole((1, l, LANES))]
    return pl.pallas_call(
        functools.partial(_even_dec_kernel, nh=nh, cc=cc),
        grid=(1,), in_specs=in_specs, out_specs=out_specs, out_shape=out_shape,
        compiler_params=_cparams("arbitrary"),
        name="even_in_dec",
    )(x, gain, mod, mod, w_main, w_f, bf, gq, gk, w_conv, p, prev0, prev1)


def _attn_kernel(it_ref, jt_ref, qp_ref, kp_ref, va_ref, o_ref, m_sc, acc_sc, s_sc, p_sc, *, nh, hd, tq):
    i = it_ref[pl.program_id(1)]
    j = jt_ref[pl.program_id(1)]
    reps = tq // LANES

    @pl.when(j == 0)
    def _():
        m_sc[...] = jnp.full(m_sc.shape, NEG, F32)
        acc_sc[...] = jnp.zeros_like(acc_sc)

    def scores(h):
        s_sc[h % 2] = _dot_nt(qp_ref[0, :, h * LANES:(h + 1) * LANES], kp_ref[0, :, h * LANES:(h + 1) * LANES])

    def step(diagonal):
        scores(0)
        for h in range(nh):
            if h + 1 < nh:
                scores(h + 1)
            s = s_sc[h % 2]
            if diagonal:
                row = lax.broadcasted_iota(jnp.int32, s.shape, 0)
                col = lax.broadcasted_iota(jnp.int32, s.shape, 1)
                s = jnp.where(col <= row, s, NEG)
            m_old = m_sc[h]
            m_new = jnp.maximum(m_old, jnp.max(s, axis=1, keepdims=True))
            alpha = jnp.exp2(m_old - m_new)
            p_sc[h % 2] = jnp.exp2(s - jnp.concatenate([m_new] * reps, axis=1)).astype(BF16)
            pv = _dot(p_sc[h % 2], va_ref[0, :, h * LANES:(h + 1) * LANES])
            acc_sc[h] = alpha * acc_sc[h] + pv
            m_sc[h] = m_new

    @pl.when(j < i)
    def _():
        step(False)

    @pl.when(j == i)
    def _():
        step(True)
        lane = lax.broadcasted_iota(jnp.int32, (tq, LANES), 1)
        half = LANES // 2
        for pair in range(nh // 2):
            a0 = acc_sc[2 * pair]
            a1 = acc_sc[2 * pair + 1]
            o0 = a0 / pltpu.roll(a0, half, axis=1)
            o1 = a1 / pltpu.roll(a1, half, axis=1)
            o_ref[0, :, pair * LANES:(pair + 1) * LANES] = jnp.where(
                lane < hd, o0, pltpu.roll(o1, half, axis=1)).astype(BF16)


def _attn_prompt(qp, kp, va, nh, hd, tq):
    g, l, _ = qp.shape
    cc = nh * hd
    assert 2 * hd == LANES, "value slots are [v | ones] halves of one 128-lane slot"
    nq = l // tq
    pairs =[(i, j) for i in range(nq) for j in range(i + 1)]
    it = jnp.asarray([p[0] for p in pairs], jnp.int32)
    jt = jnp.asarray([p[1] for p in pairs], jnp.int32)
    qrow = lambda b, p, it, jt: (b, it[p], 0)
    krow = lambda b, p, it, jt: (b, jt[p], 0)
    grid_spec = pltpu.PrefetchScalarGridSpec(
        num_scalar_prefetch=2, grid=(g, len(pairs)),
        in_specs=[pl.BlockSpec((1, tq, nh * LANES), qrow), pl.BlockSpec((1, tq, nh * LANES), krow),
                  pl.BlockSpec((1, tq, nh * LANES), krow)],
        out_specs=pl.BlockSpec((1, tq, cc), qrow),
        scratch_shapes=[pltpu.VMEM((nh, tq, LANES), F32), pltpu.VMEM((nh, tq, LANES), F32),
                        pltpu.VMEM((2, tq, tq), F32), pltpu.VMEM((2, tq, tq), BF16)])
    return pl.pallas_call(
        functools.partial(_attn_kernel, nh=nh, hd=hd, tq=tq),
        grid_spec=grid_spec,
        out_shape=jax.ShapeDtypeStruct((g, l, cc), BF16),
        compiler_params=_cparams("arbitrary", "arbitrary"),
        name="attn_prompt",
    )(it, jt, qp, kp, va)


def _attn_dec_kernel(pt_ref, qcol_ref, kncol_ref, vncol_ref, lfnew_ref, uincl_ref, ones_ref, *rest, npg, nh):
    k_refs = rest[:npg]
    v_refs = rest[npg:2 * npg]
    lf_refs = rest[2 * npg:3 * npg]
    o_ref = rest[3 * npg]
    m_sc, l_sc, acc_sc, c_sc = rest[3 * npg + 1:]
    g = pl.program_id(1)

    @pl.when(g == 0)
    def _():
        m_sc[...] = jnp.full(m_sc.shape, NEG, F32)
        l_sc[...] = jnp.zeros_like(l_sc)
        acc_sc[...] = jnp.zeros_like(acc_sc)
        c_sc[...] = jnp.zeros_like(c_sc)

    parts = []
    for j in range(npg):
        hi, mid, lo = _split3(lf_refs[j][0, 0])
        parts += [hi.astype(F32), mid.astype(F32), lo.astype(F32), jnp.zeros((nh, LANES), F32)]
    parts = jnp.concatenate(parts, axis=0).astype(BF16)
    cum = _dot(parts, uincl_ref[...])
    tot = _dot(parts, ones_ref[...])
    fs = []
    c = c_sc[...]
    for j in range(npg):
        r = 4 * nh * j
        fs.append(c + (cum[r:r + nh] + cum[r + nh:r + 2 * nh] + cum[r + 2 * nh:r + 3 * nh]))
        c = c + (tot[r:r + nh] + tot[r + nh:r + 2 * nh] + tot[r + 2 * nh:r + 3 * nh])
    c_sc[...] = c

    for h in range(nh):
        qc = qcol_ref[0, h]
        rows = [jnp.sum(qc * k_refs[j][0, 0, h], axis=0, keepdims=True) - fs[j][h:h + 1, :] for j in range(npg)]
        m_old = m_sc[h]
        m_new = jnp.maximum(m_old, jnp.max(functools.reduce(jnp.maximum, rows), axis=1, keepdims=True))
        alpha = jnp.exp(m_old - m_new)
        ps = [jnp.exp(r - m_new) for r in rows]
        l_sc[h] = alpha * l_sc[h] + jnp.sum(functools.reduce(jnp.add, ps), axis=1, keepdims=True)
        acc = alpha * acc_sc[h]
        for j in range(npg):
            acc = acc + ps[j] * v_refs[j][0, 0, h]
        acc_sc[h] = acc
        m_sc[h] = m_new

    @pl.when(g == pl.num_programs(1) - 1)
    def _():
        lane = lax.broadcasted_iota(jnp.int32, (1, LANES), 1)
        for h in range(nh):
            f_new = c_sc[h:h + 1, :] + lfnew_ref[0, h:h + 1, :]
            s_new = jnp.sum(qcol_ref[0, h] * kncol_ref[0, h], axis=0, keepdims=True) - f_new
            s_new = jnp.where(lane == 0, s_new, NEG)
            m_old = m_sc[h]
            m_new = jnp.maximum(m_old, jnp.max(s_new, axis=1, keepdims=True))
            alpha = jnp.exp(m_old - m_new)
            p_new = jnp.exp(s_new - m_new)
            l_fin = alpha * l_sc[h] + jnp.sum(p_new, axis=1, keepdims=True)
            acc = (alpha * acc_sc[h] + p_new * vncol_ref[0, h]) / l_fin
            hi, mid, lo = _split3(acc)
            ones = ones_ref[0:ONES_ROWS, :]
            o = _dot_nt(ones, hi) + _dot_nt(ones, mid) + _dot_nt(ones, lo)
            o_ref[0, h:h + 1, :] = o[0:1, :]


def _attn_decode(page_table, q, knew, vnew, lfnew, cache_kt, cache_vt, cache_lft, e, nh, hd):
    n_seq, n_pages = page_table.shape
    page = cache_kt.shape[-1]
    assert page == LANES and n_pages % PAGES_PER_STEP == 0
    npg = PAGES_PER_STEP
    col = lambda a: jnp.broadcast_to(a.reshape(n_seq, nh, hd, 1), (n_seq, nh, hd, LANES))
    uincl = jnp.asarray(np.triu(np.ones((page, page))), BF16)
    ones = jnp.ones((LANES, LANES), BF16)

    def pg(j, nd):
        return lambda b, g, pt: (pt[b, g * npg + j], e) + (0,) * nd

    per_seq4 = lambda b, g, pt: (b, 0, 0, 0)
    per_seq3 = lambda b, g, pt: (b, 0, 0)
    const2 = lambda b, g, pt: (0, 0)
    in_specs = [pl.BlockSpec((1, nh, hd, LANES), per_seq4), pl.BlockSpec((1, nh, hd, LANES), per_seq4),
                pl.BlockSpec((1, nh, hd, LANES), per_seq4), pl.BlockSpec((1, nh, 1), per_seq3),
                pl.BlockSpec((page, page), const2), pl.BlockSpec((LANES, LANES), const2)]
    in_specs += [pl.BlockSpec((1, 1, nh, hd, page), pg(j, 3)) for j in range(npg)]
    in_specs += [pl.BlockSpec((1, 1, nh, hd, page), pg(j, 3)) for j in range(npg)]
    in_specs += [pl.BlockSpec((1, 1, nh, page), pg(j, 2)) for j in range(npg)]
    grid_spec = pltpu.PrefetchScalarGridSpec(
        num_scalar_prefetch=1, grid=(n_seq, n_pages // npg), in_specs=in_specs,
        out_specs=pl.BlockSpec((1, nh, hd), per_seq3),
        scratch_shapes=[pltpu.VMEM((nh, 1, LANES), F32), pltpu.VMEM((nh, 1, LANES), F32),
                        pltpu.VMEM((nh, hd, LANES), F32), pltpu.VMEM((nh, LANES), F32)])
    o = pl.pallas_call(
        functools.partial(_attn_dec_kernel, npg=npg, nh=nh),
        grid_spec=grid_spec,
        out_shape=jax.ShapeDtypeStruct((n_seq, nh, hd), F32),
        compiler_params=_cparams("arbitrary", "arbitrary"),
        name="attn_decode",
    )(page_table, col(q), col(knew), col(vnew), lfnew.reshape(n_seq, nh, 1), uincl, ones,
      *([cache_kt] * npg), *([cache_vt] * npg), *([cache_lft] * npg))
    return o.reshape(n_seq, 1, nh * hd)


def _odd_project(x_ref, g_ref, sh_ref, sc_ref, w_ref, wgd_ref, wgk_ref, bgk_ref, qk, vv, dk):
    hb = _norm_mod(x_ref[0], g_ref[...], sh_ref[0], sc_ref[0]).astype(BF16)
    q = _dot(hb, w_ref[:, 0:qk]) * (dk ** -0.5)
    k = _dot(hb, w_ref[:, qk:2 * qk])
    gd = _dot(hb, wgd_ref[...]).astype(BF16)
    gl = _log_sigmoid(_dot(gd, wgk_ref[...]) + bgk_ref[...]) * (1.0 / GLA_TAU)
    v = _dot(hb, w_ref[:, 2 * qk:2 * qk + vv]).astype(BF16)
    sr = _silu(_dot(hb, w_ref[:, 2 * qk + vv:2 * qk + 2 * vv]))
    return q, k, gl, v, sr


def _odd_in_kernel(x_ref, g_ref, sh_ref, sc_ref, w_ref, wgd_ref, wgk_ref, bgk_ref,
                   q_ref, k_ref, v_ref, sr_ref, gl_ref, *, qk, vv, dk):
    q, k, gl, v, sr = _odd_project(x_ref, g_ref, sh_ref, sc_ref, w_ref, wgd_ref, wgk_ref, bgk_ref, qk, vv, dk)
    q_ref[0] = q
    k_ref[0] = k
    v_ref[0] = v
    sr_ref[0] = sr
    gl_ref[0] = gl


def _gla_seq_kernel(q_ref, k_ref, g_ref, v_ref, sr_ref, go_ref, tri_ref, o_ref, st_ref, s_sc, *, nh, dk, dv, tt):
    @pl.when(pl.program_id(1) == 0)
    def _():
        s_sc[...] = jnp.zeros_like(s_sc)

    o_ref[0] = _gla_tile(q_ref[0], k_ref[0], g_ref[0], v_ref[0], sr_ref[0], go_ref, tri_ref, s_sc, nh, dk, dv, tt)
    st_ref[0] = s_sc[...]


def _odd_weights(w_in, w_gk, b_gk, qk, vv):
    n, d, _ = w_in.shape
    rank = w_gk.shape[1]
    w_main = w_in[:, :, :2 * qk + 2 * vv].astype(BF16)
    w_gd = jnp.zeros((n, d, LANES), BF16).at[:, :, :rank].set(w_in[:, :, 2 * qk + 2 * vv:].astype(BF16))
    wgk = jnp.zeros((n, LANES, qk), BF16).at[:, :rank].set(w_gk.astype(BF16))
    return w_main, w_gd, wgk, b_gk.reshape(n, 1, qk)


def _odd_in(x, gain, mod, wts, li, qk, vv, dk, tm):
    g, l, d = x.shape
    oi = li // 2
    w_main, w_gd, wgk, bgk = wts
    row = lambda b, i: (b, i, 0)
    in_specs = [pl.BlockSpec((1, tm, d), row), _layer(gain, li), _mod_spec(mod, li, 0, d, 2), _mod_spec(mod, li, 1, d, 2),
                _layer(w_main, oi), _layer(w_gd, oi), _layer(wgk, oi), _layer(bgk, oi)]
    out_shape = [jax.ShapeDtypeStruct((g, l, qk), F32), jax.ShapeDtypeStruct((g, l, qk), F32),
                 jax.ShapeDtypeStruct((g, l, vv), BF16), jax.ShapeDtypeStruct((g, l, vv), F32),
                 jax.ShapeDtypeStruct((g, l, qk), F32)]
    out_specs = [pl.BlockSpec((1, tm, qk), row), pl.BlockSpec((1, tm, qk), row), pl.BlockSpec((1, tm, vv), row),
                 pl.BlockSpec((1, tm, vv), row), pl.BlockSpec((1, tm, qk), row)]
    return pl.pallas_call(
        functools.partial(_odd_in_kernel, qk=qk, vv=vv, dk=dk),
        grid=(g, l // tm), in_specs=in_specs, out_specs=out_specs, out_shape=out_shape,
        compiler_params=_cparams("arbitrary", "arbitrary"),
        name="odd_in",
    )(x, gain, mod, mod, w_main, w_gd, wgk, bgk)


def _gla_tile(q, k, gl, v, sr, go_ref, tri_ref, s_sc, nh, dk, dv, tt):
    c = GLA_CHUNK
    qk = nh * dk
    b_all = _sum3(_dot(tri_ref[...], jnp.concatenate(_split3(gl), axis=1)), qk)
    nchunk = tt // c
    row = lax.broadcasted_iota(jnp.int32, (tt, tt), 0)
    col = lax.broadcasted_iota(jnp.int32, (tt, tt), 1)
    blockmask = (col <= row) & (col >= (row // c) * c)
    b_last = [b_all[(n + 1) * c - 1:(n + 1) * c, :] for n in range(nchunk)]
    b_last_rows = jnp.concatenate([jnp.broadcast_to(bl, (c, qk)) for bl in b_last], axis=0)
    q_e = (q * jnp.exp(b_all)).astype(BF16)
    k_e = (k * jnp.exp(-b_all)).astype(BF16)
    k_d = (k * jnp.exp(b_last_rows - b_all)).astype(BF16)
    decay = [jnp.exp(bl) for bl in b_last]
    out = []
    for h in range(nh):
        kc = slice(h * dk, (h + 1) * dk)
        vc = slice(h * dv, (h + 1) * dv)
        v_h = v[:, vc]
        a = jnp.where(blockmask, _dot_nt(q_e[:, kc], k_e[:, kc]), 0.0)
        o_intra = _dot(a.astype(BF16), v_h)
        s_t = s_sc[h]
        o_inter = []
        for n in range(nchunk):
            rows = slice(n * c, (n + 1) * c)
            o_inter.append(_dot_nt(q_e[rows, kc], s_t.astype(BF16)))
            s_t = s_t * decay[n][:, kc] + _dot_tn(v_h[rows], k_d[rows, kc])
        s_sc[h] = s_t
        o = o_intra + jnp.concatenate(o_inter, axis=0)
        y = o * lax.rsqrt(jnp.mean(o * o, axis=-1, keepdims=True) + EPS) * go_ref[...]
        out.append((y * sr[:, vc]).astype(BF16))
    return jnp.concatenate(out, axis=1)


def _gla_seq(q, k, gl, v, sr, g_o, oi, nh, dk, dv, tt):
    g, l, qk = q.shape
    vv = nh * dv
    tri = jnp.asarray(np.kron(np.eye(tt // GLA_CHUNK), np.tril(np.ones((GLA_CHUNK, GLA_CHUNK)))), BF16)
    row = lambda b, i: (b, i, 0)
    return pl.pallas_call(
        functools.partial(_gla_seq_kernel, nh=nh, dk=dk, dv=dv, tt=tt),
        grid=(g, l // tt),
        in_specs=[pl.BlockSpec((1, tt, qk), row), pl.BlockSpec((1, tt, qk), row), pl.BlockSpec((1, tt, qk), row),
                  pl.BlockSpec((1, tt, vv), row), pl.BlockSpec((1, tt, vv), row), _layer(g_o, oi), _full((tt, tt))],
        out_specs=[pl.BlockSpec((1, tt, vv), row), pl.BlockSpec((1, nh, dv, dk), lambda b, i: (b, 0, 0, 0))],
        out_shape=[jax.ShapeDtypeStruct((g, l, vv), BF16), jax.ShapeDtypeStruct((g, nh, dv, dk), F32)],
        scratch_shapes=[pltpu.VMEM((nh, dv, dk), F32)],
        compiler_params=_cparams("arbitrary", "arbitrary"),
        name="gla_seq",
    )(q, k, gl, v, sr, g_o, tri)


def _gla_dec_kernel(q_ref, k_ref, g_ref, v_ref, sr_ref, go_ref, s0_ref, *rest, nh, dk, dv, aliased):
    o_ref, s_ref = rest[1:] if aliased else rest

    def column(r):
        return jnp.broadcast_to(r, (dk, dk)).T

    for h in range(nh):
        kc = slice(h * dk, (h + 1) * dk)
        dcol = jnp.exp(column(g_ref[0, :, kc]))
        kcol = column(k_ref[0, :, kc])
        qcol = column(q_ref[0, :, kc])
        halves = []
        for half in range(dv // dk):
            lanes = slice(half * dk, (half + 1) * dk)
            v = v_ref[0, :, h * dv + half * dk:h * dv + (half + 1) * dk].astype(F32)
            s_new = dcol * s0_ref[0, 0, h, :, lanes] + kcol * v
            s_ref[0, h, :, lanes] = s_new
            halves.append(jnp.sum(qcol * s_new, axis=0, keepdims=True))
        o = jnp.concatenate(halves, axis=1)
        y = o * lax.rsqrt(jnp.mean(o * o, axis=-1, keepdims=True) + EPS) * go_ref[...]
        vc = slice(h * dv, (h + 1) * dv)
        o_ref[0, :, vc] = y * sr_ref[0, :, vc]


def _gla_dec(q, k, gl, v, sr, g_o, state_gla, o_idx, s_prev, nh, dk, dv):
    n_seq = q.shape[0]
    qk, vv = nh * dk, nh * dv
    r3 = lambda a: a.reshape(n_seq, 1, a.shape[-1])
    per_seq = lambda b: (b, 0, 0)
    in_specs = [pl.BlockSpec((1, 1, qk), per_seq), pl.BlockSpec((1, 1, qk), per_seq),
                pl.BlockSpec((1, 1, qk), per_seq), pl.BlockSpec((1, 1, vv), per_seq),
                pl.BlockSpec((1, 1, vv), per_seq), pl.BlockSpec((1, dv), lambda b: (0, 0)),
                pl.BlockSpec((1, 1, nh, dk, dv), lambda b: (b, o_idx, 0, 0, 0))]
    args = [r3(q), r3(k), r3(gl), r3(v), r3(sr), g_o.reshape(1, dv), state_gla]
    aliases = {}
    if s_prev is not None:
        in_specs.append(pl.BlockSpec(memory_space=pl.ANY))
        args.append(s_prev)
        aliases = {len(args) - 1: 1}
    return pl.pallas_call(
        functools.partial(_gla_dec_kernel, nh=nh, dk=dk, dv=dv, aliased=s_prev is not None),
        grid=(n_seq,),
        in_specs=in_specs,
        out_specs=[pl.BlockSpec((1, 1, vv), per_seq),
                   pl.BlockSpec((1, None, nh, dk, dv), lambda b: (b, o_idx, 0, 0, 0))],
        out_shape=[jax.ShapeDtypeStruct((n_seq, 1, vv), F32), jax.ShapeDtypeStruct(state_gla.shape, F32)],
        input_output_aliases=aliases,
        compiler_params=_cparams("arbitrary"),
        name="gla_dec",
    )(*args)


def _post_kernel(x_ref, a1_ref, a2_ref, wo_ref, gtm_ref, gf_ref, shf_ref, scf_ref, gtf_ref, w1_ref, w2_ref,
                 o_ref, h_sc, *, ka):
    f = pl.program_id(2)

    @pl.when(f == 0)
    def _():
        m = _dot(a1_ref[0], wo_ref[0:ka, :]) + _dot(a2_ref[0], wo_ref[ka:2 * ka, :])
        x1 = x_ref[0] + gtm_ref[0] * m
        o_ref[0] = x1
        h_sc[...] = _norm_mod(x1, gf_ref[...], shf_ref[0], scf_ref[0]).astype(BF16)

    t = jnp.maximum(_dot(h_sc[...], w1_ref[...]), 0.0)
    o_ref[0] += gtf_ref[0] * _dot((t * t).astype(BF16), w2_ref[...])


def _post(x, a1, a1_col, a2, a2_col, w_out, gain_f, mod, w1, w2, li, tm, tf):
    g, l, d = x.shape
    ka = w_out.shape[1] // 2
    dff = w1.shape[2]
    row = lambda b, i, f: (b, i, 0)
    modc = lambda c: _mod_spec(mod, li, c, d, 3)
    return pl.pallas_call(
        functools.partial(_post_kernel, ka=ka),
        grid=(g, l // tm, dff // tf),
        in_specs=[pl.BlockSpec((1, tm, d), row),
                  pl.BlockSpec((1, tm, ka), lambda b, i, f: (b, i, a1_col)),
                  pl.BlockSpec((1, tm, ka), lambda b, i, f: (b, i, a2_col)),
                  _layer(w_out, li // 2), modc(2), _layer(gain_f, li), modc(3), modc(4), modc(5),
                  pl.BlockSpec((None, d, tf), lambda b, i, f: (li, 0, f)),
                  pl.BlockSpec((None, tf, d), lambda b, i, f: (li, f, 0))],
        out_specs=pl.BlockSpec((1, tm, d), row),
        out_shape=jax.ShapeDtypeStruct((g, l, d), F32),
        scratch_shapes=[pltpu.VMEM((tm, d), BF16)],
        compiler_params=_cparams("arbitrary", "arbitrary", "arbitrary"),
        name="post_mlp",
    )(x, a1, a2, w_out, mod, gain_f, mod, mod, mod, w1, w2)


def kernel(x_prompt, x_sample, c_prompt, c_sample, cache_k, cache_v, cache_logf, page_table, state_conv, state_gla,
           g_mix, g_ffn, w_ada, b_ada, w_in_even, b_f, g_q, g_k, w_conv, w_out_even, w_in_odd, w_gk, b_gk, g_o,
           w_out_odd, w_ff1, w_ff2):
    bp, seq, d = x_prompt.shape
    n_seq = x_sample.shape[0]
    depth = w_ada.shape[0]
    n_even, nh_fox = b_f.shape
    hd_fox = g_q.shape[1]
    cc = nh_fox * hd_fox
    _, n_odd, nh_gla, dk, dv = state_gla.shape
    qk, vv = nh_gla * dk, nh_gla * dv

    mod_all = _adaln(jnp.concatenate([c_prompt, c_sample], axis=0), w_ada, b_ada)

    cache_kt = jnp.transpose(cache_k, (0, 1, 3, 4, 2))
    cache_vt = jnp.transpose(cache_v, (0, 1, 3, 4, 2))
    cache_lft = jnp.swapaxes(cache_logf, 2, 3)

    mod_p = mod_all[:, :bp].reshape(depth, bp, 1, 6 * d)
    mod_s = mod_all[:, bp:].reshape(depth, 1, n_seq, 6 * d)
    gm = g_mix.reshape(depth, 1, d)
    gf = g_ffn.reshape(depth, 1, d)
    w1 = w_ff1.astype(BF16)
    w2 = w_ff2.astype(BF16)
    wts_even = _even_weights(w_in_even, b_f, g_q, g_k, nh_fox, hd_fox)
    wts_odd = _odd_weights(w_in_odd, w_gk, b_gk, qk, vv)
    wo_even = w_out_even.astype(BF16)
    wo_odd = w_out_odd.astype(BF16)
    go = g_o.reshape(n_odd, 1, dv)

    xp = x_prompt
    xs = x_sample.reshape(1, n_seq, d)
    kv_rows = None
    gla_s = None
    lf_rows_p, fox_s, conv_p, conv_s, gla_p = [], [], [], [], []
    for li in range(depth):
        if li % 2 == 0:
            e = li // 2
            ya, utail, qp, kp, va, kt_rows, vt_rows, lf_rows = _even_in_seq(
                xp, gm, mod_p, wts_even, w_conv, li, kv_rows, nh_fox, hd_fox, tm=TM_EVEN)
            kv_rows = (kt_rows, vt_rows)
            o = _attn_prompt(qp, kp, va, nh_fox, hd_fox, tq=TQ_ATTN)
            xp = _post(xp, ya, 0, o, 0, wo_even, gf, mod_p, w1, w2, li, tm=TM_POST, tf=TF_POST)
            lf_rows_p.append(lf_rows)
            conv_p.append(utail[:, 6:8])
            prev0 = state_conv[:, e, 0].reshape(1, n_seq, cc)
            prev1 = state_conv[:, e, 1].reshape(1, n_seq, cc)
            ya, u, qn, kn, v, lfpad = _even_in_dec(xs, gm, mod_s, wts_even, w_conv, li, prev0, prev1, nh_fox, hd_fox)
            lf_new = lfpad[0, :, :nh_fox]
            o = _attn_decode(page_table, qn[0], kn[0], v[0], lf_new, cache_kt, cache_vt, cache_lft, e,
                             nh_fox, hd_fox)
            o = o.reshape(1, n_seq, cc).astype(BF16)
            xs = _post(xs, ya, 0, o, 0, wo_even, gf, mod_s, w1, w2, li, tm=n_seq, tf=TF_POST)
            fox_s.append((kn[0], v[0], lf_new))
            conv_s.append(jnp.stack([prev1[0], u[0]], axis=1))
        else:
            oi = li // 2
            q, k, v, sr, gl = _odd_in(xp, gm, mod_p, wts_odd, li, qk, vv, dk, tm=TM_ODD)
            og, s_t = _gla_seq(q, k, gl, v, sr, go, oi, nh_gla, dk, dv, tt=TT_GLA)
            xp = _post(xp, og, 0, og, 1, wo_odd, gf, mod_p, w1, w2, li, tm=TM_POST, tf=TF_POST)
            gla_p.append(jnp.swapaxes(s_t, 2, 3))
            q, k, v, sr, gl = _odd_in(xs, gm, mod_s, wts_odd, li, qk, vv, dk, tm=n_seq)
            og, gla_s = _gla_dec(q[0], k[0], gl[0], v[0].astype(F32), sr[0], g_o[oi], state_gla, oi, gla_s,
                                 nh_gla, dk, dv)
            og = og.reshape(1, n_seq, vv).astype(BF16)
            xs = _post(xs, og, 0, og, 1, wo_odd, gf, mod_s, w1, w2, li, tm=n_seq, tf=TF_POST)

    heads = lambda a, n: a.reshape(a.shape[0], n, nh_fox, hd_fox)
    k_p = jnp.transpose(kv_rows[0], (0, 1, 4, 2, 3))
    v_p = jnp.transpose(kv_rows[1], (0, 1, 4, 2, 3))
    lf_p = jnp.stack(lf_rows_p, axis=1)
    k_s = jnp.stack([heads(r[0], 1) for r in fox_s], axis=1)
    v_s = jnp.stack([heads(r[1], 1) for r in fox_s], axis=1)
    lf_s = jnp.stack([r[2].reshape(n_seq, 1, nh_fox) for r in fox_s], axis=1)
    return (xp, xs.reshape(n_seq, 1, d), k_p, v_p, lf_p, k_s, v_s, lf_s,
            jnp.stack(conv_p, axis=1), jnp.stack(conv_s, axis=1),
            jnp.stack(gla_p, axis=1), gla_s)
```

```python
import functools

import numpy as np
import jax
import jax.numpy as jnp
from jax import lax
from jax.experimental import pallas as pl
from jax.experimental.pallas import tpu as pltpu

F32 = jnp.float32
BF16 = jnp.bfloat16

EPS = 1e-6
GLA_TAU = 16.0
GLA_CHUNK = 64
LANES = 128
VMEM_LIMIT = 56 * 1024 * 1024
NEG = -1e30
LOG2E = 1.4426950408889634
PAGES_PER_STEP = 16
ONES_ROWS = 16
TM_EVEN = 512
TM_ODD = 512
TM_POST = 1024
TF_POST = 1024
TQ_ATTN = 512
TT_GLA = 256


def _cparams(*sem):
    return pltpu.CompilerParams(dimension_semantics=sem, vmem_limit_bytes=VMEM_LIMIT)


def _silu(x):
    return x / (1.0 + jnp.exp(-x))


def _log_sigmoid(z):
    return jnp.minimum(z, 0.0) - jnp.log1p(jnp.exp(-jnp.abs(z)))


def _norm_mod(x, gain, shift, scale):
    y = x * lax.rsqrt(jnp.mean(x * x, axis=-1, keepdims=True) + EPS)
    return (y * gain) * (1.0 + scale) + shift


def _split3(a):
    hi = a.astype(BF16)
    r = a - hi.astype(F32)
    mid = r.astype(BF16)
    lo = (r - mid.astype(F32)).astype(BF16)
    return hi, mid, lo


def _dot(a, b):
    return jnp.dot(a, b, preferred_element_type=F32)


def _dot_nt(a, b):
    return lax.dot_general(a, b, (((1,), (1,)), ((), ())), preferred_element_type=F32)


def _dot_tn(a, b):
    return lax.dot_general(a, b, (((0,), (0,)), ((), ())), preferred_element_type=F32)


def _sum3(c, w):
    return c[:, 0:w] + c[:, w:2 * w] + c[:, 2 * w:3 * w]


def _adaln_kernel(c_ref, w_ref, b_ref, o_ref):
    a = _silu(c_ref[...]).astype(BF16)
    o_ref[0] = _dot(a, w_ref[0].astype(BF16)) + b_ref[0]


def _adaln(c_all, w_ada, b_ada):
    depth, d, n = w_ada.shape
    r = c_all.shape[0]
    tn = n // 4
    return pl.pallas_call(
        _adaln_kernel,
        grid=(depth, n // tn),
        in_specs=[pl.BlockSpec((r, d), lambda l, j: (0, 0)),
                  pl.BlockSpec((1, d, tn), lambda l, j: (l, 0, j)),
                  pl.BlockSpec((1, 1, tn), lambda l, j: (l, 0, j))],
        out_specs=pl.BlockSpec((1, r, tn), lambda l, j: (l, 0, j)),
        out_shape=jax.ShapeDtypeStruct((depth, r, n), F32),
        compiler_params=_cparams("arbitrary", "arbitrary"),
        name="adaln",
    )(c_all, w_ada, b_ada.reshape(depth, 1, n))


def _even_project(x_ref, g_ref, sh_ref, sc_ref, w_ref, wf_ref, bf_ref, gq_ref, gk_ref, p_ref, cc, nh):
    hb = _norm_mod(x_ref[0], g_ref[...], sh_ref[0], sc_ref[0]).astype(BF16)
    gb = _dot(hb, w_ref[:, 0:cc])
    u = _dot(hb, w_ref[:, cc:2 * cc]) * _dot(hb, w_ref[:, 2 * cc:3 * cc])
    q = _dot(hb, w_ref[:, 3 * cc:4 * cc])
    k = _dot(hb, w_ref[:, 4 * cc:5 * cc])
    v = _dot(hb, w_ref[:, 5 * cc:6 * cc])
    hd = cc // nh
    qms = _dot((q * q).astype(BF16), p_ref[...]) * (1.0 / hd)
    kms = _dot((k * k).astype(BF16), p_ref[...]) * (1.0 / hd)
    qn = (q * lax.rsqrt(qms + EPS)) * gq_ref[...]
    kn = (k * lax.rsqrt(kms + EPS)) * gk_ref[...]
    fz = _dot(hb, wf_ref[...]) + bf_ref[...]
    lane = lax.broadcasted_iota(jnp.int32, fz.shape, 1)
    lf = jnp.where(lane < nh, _log_sigmoid(fz), 0.0)
    return gb, u, qn, kn, v, lf


def _even_seq_kernel(x_ref, g_ref, sh_ref, sc_ref, w_ref, wf_ref, bf_ref, gq_ref, gk_ref, wc_ref, p_ref,
                     tri_ref, eq_ref, ek_ref, oq_ref, ok_ref, *rest, nh, cc, tm, aliased):
    if aliased:
        rest = rest[2:]
    ya_ref, ut_ref, qp_ref, kp_ref, va_ref, ko_ref, vo_ref, lf_ref, ubuf, fcarry = rest
    i = pl.program_id(1)
    gb, u, qn, kn, v, lf = _even_project(x_ref, g_ref, sh_ref, sc_ref, w_ref, wf_ref, bf_ref,
                                         gq_ref, gk_ref, p_ref, cc, nh)

    @pl.when(i == 0)
    def _():
        ubuf[0:8, :] = jnp.zeros((8, cc), F32)
        fcarry[...] = jnp.zeros_like(fcarry)

    ubuf[8:8 + tm, :] = u
    conv = wc_ref[0:1, :] * ubuf[6:6 + tm, :] + wc_ref[1:2, :] * ubuf[7:7 + tm, :] + wc_ref[2:3, :] * u
    ya_ref[0] = (gb * conv).astype(BF16)
    tail = ubuf[tm:tm + 8, :]
    ubuf[0:8, :] = tail
    ut_ref[0] = tail

    ko_ref[0, 0] = kn.T.reshape(nh, cc // nh, tm)
    vo_ref[0, 0] = v.T.reshape(nh, cc // nh, tm)
    lf_ref[0] = lf[:, 0:nh]
    vb = v.astype(BF16)
    va_ref[0] = jnp.ones((tm, nh * LANES), BF16)
    for h in range(nh):
        va_ref[0, :, h * LANES:h * LANES + cc // nh] = vb[:, h * (cc // nh):(h + 1) * (cc // nh)]

    c3 = _dot(tri_ref[...], jnp.concatenate(_split3(lf), axis=1))
    f = _sum3(c3, LANES) + fcarry[...]
    fcarry[...] = f[tm - 1:tm, :]
    f3 = jnp.concatenate(_split3(f * LOG2E), axis=1)

    qp_ref[0] = (_dot(f3, eq_ref[...]) + oq_ref[...]).astype(BF16)
    kp_ref[0] = (_dot(f3, ek_ref[...]) + ok_ref[...]).astype(BF16)
    qb = qn.astype(BF16)
    kb = kn.astype(BF16)
    hd = cc // nh
    for h in range(nh):
        qp_ref[0, :, h * LANES:h * LANES + hd] = qb[:, h * hd:(h + 1) * hd]
        kp_ref[0, :, h * LANES:h * LANES + hd] = kb[:, h * hd:(h + 1) * hd]


def _even_dec_kernel(x_ref, g_ref, sh_ref, sc_ref, w_ref, wf_ref, bf_ref, gq_ref, gk_ref, wc_ref, p_ref,
                     p0_ref, p1_ref,
                     ya_ref, u_ref, q_ref, k_ref, v_ref, lf_ref, *, nh, cc):
    gb, u, qn, kn, v, lf = _even_project(x_ref, g_ref, sh_ref, sc_ref, w_ref, wf_ref, bf_ref,
                                         gq_ref, gk_ref, p_ref, cc, nh)
    conv = wc_ref[0:1, :] * p0_ref[0] + wc_ref[1:2, :] * p1_ref[0] + wc_ref[2:3, :] * u
    ya_ref[0] = (gb * conv).astype(BF16)
    u_ref[0] = u
    q_ref[0] = qn
    k_ref[0] = kn
    v_ref[0] = v
    lf_ref[0] = lf


def _even_consts(nh, hd, tm):
    cc = nh * hd
    p = np.kron(np.eye(nh), np.ones((hd, hd)))
    tri = np.tril(np.ones((tm, tm)))
    eq = np.zeros((3 * LANES, nh * LANES))
    ek = np.zeros((3 * LANES, nh * LANES))
    oq = np.zeros((1, nh * LANES))
    ok = np.zeros((1, nh * LANES))
    for h in range(nh):
        for part in range(3):
            eq[part * LANES + h, h * LANES + hd + part] = 1.0
            ek[part * LANES + h, h * LANES + hd + 3 + part] = -1.0
            oq[0, h * LANES + hd + 3 + part] = 1.0
            ok[0, h * LANES + hd + part] = 1.0
    bf = lambda a: jnp.asarray(a, BF16)
    return bf(p), bf(tri), bf(eq), bf(ek), jnp.asarray(oq, F32), jnp.asarray(ok, F32)


def _even_weights(w_in, b_f, g_q, g_k, nh, hd):
    n, d, _ = w_in.shape
    cc = nh * hd
    w_main = w_in.astype(BF16)
    w_f = jnp.zeros((n, d, LANES), BF16).at[:, :, :nh].set(w_in[:, :, 6 * cc:].astype(BF16))
    bf = jnp.zeros((n, 1, LANES), F32).at[:, 0, :nh].set(b_f)
    gq = (jnp.tile(g_q, (1, nh)) * (hd ** -0.5)).reshape(n, 1, cc)
    gk = jnp.tile(g_k, (1, nh)).reshape(n, 1, cc)
    return w_main, w_f, bf, gq, gk


def _full(shape):
    nd = len(shape)
    return pl.BlockSpec(shape, lambda *_: (0,) * nd)


def _layer(a, idx, width=None):
    nd = a.ndim
    shape = a.shape[1:] if width is None else a.shape[1:-1] + (width,)
    return pl.BlockSpec((None,) + shape, lambda *_: (idx,) + (0,) * (nd - 1))


def _mod_spec(mod, li, col, d, nargs):
    r = mod.shape[2]
    if nargs == 1:
        return pl.BlockSpec((None, 1, r, d), lambda i: (li, 0, 0, col))
    if nargs == 2:
        return pl.BlockSpec((None, 1, r, d), lambda b, i: (li, b, 0, col))
    return pl.BlockSpec((None, 1, r, d), lambda b, i, f: (li, b, 0, col))


def _even_in_seq(x, gain, mod, wts, w_conv, li, kv_prev, nh, hd, tm):
    g, l, d = x.shape
    cc = nh * hd
    e = li // 2
    n_even = w_conv.shape[0]
    w_main, w_f, bf, gq, gk = wts
    gq = gq * LOG2E
    p, tri, eq, ek, oq, ok = _even_consts(nh, hd, tm)
    row = lambda b, i: (b, i, 0)
    in_specs = [pl.BlockSpec((1, tm, d), row), _layer(gain, li), _mod_spec(mod, li, 0, d, 2), _mod_spec(mod, li, 1, d, 2),
                _layer(w_main, e, 6 * cc), _layer(w_f, e), _layer(bf, e), _layer(gq, e), _layer(gk, e), _layer(w_conv, e),
                _full(p.shape), _full(tri.shape), _full(eq.shape), _full(ek.shape), _full(oq.shape), _full(ok.shape)]
    args = [x, gain, mod, mod, w_main, w_f, bf, gq, gk, w_conv, p, tri, eq, ek, oq, ok]
    kt_shape = jax.ShapeDtypeStruct((g, n_even, nh, hd, l), F32)
    kt_spec = pl.BlockSpec((1, 1, nh, hd, tm), lambda b, i: (b, e, 0, 0, i))
    out_shape = [jax.ShapeDtypeStruct((g, l, cc), BF16),
                 jax.ShapeDtypeStruct((g, 8, cc), F32),
                 jax.ShapeDtypeStruct((g, l, nh * LANES), BF16),
                 jax.ShapeDtypeStruct((g, l, nh * LANES), BF16),
                 jax.ShapeDtypeStruct((g, l, nh * LANES), BF16),
                 kt_shape, kt_shape,
                 jax.ShapeDtypeStruct((g, l, nh), F32)]
    out_specs = [pl.BlockSpec((1, tm, cc), row), pl.BlockSpec((1, 8, cc), lambda b, i: (b, 0, 0)),
                 pl.BlockSpec((1, tm, nh * LANES), row), pl.BlockSpec((1, tm, nh * LANES), row),
                 pl.BlockSpec((1, tm, nh * LANES), row), kt_spec, kt_spec, pl.BlockSpec((1, tm, nh), row)]
    aliases = {}
    if kv_prev is not None:
        in_specs += [pl.BlockSpec(memory_space=pl.ANY)] * 2
        args += list(kv_prev)
        aliases = {len(args) - 2: 5, len(args) - 1: 6}
    return pl.pallas_call(
        functools.partial(_even_seq_kernel, nh=nh, cc=cc, tm=tm, aliased=kv_prev is not None),
        grid=(g, l // tm), in_specs=in_specs, out_specs=out_specs, out_shape=out_shape,
        scratch_shapes=[pltpu.VMEM((tm + 8, cc), F32), pltpu.VMEM((1, LANES), F32)],
        input_output_aliases=aliases,
        compiler_params=_cparams("arbitrary", "arbitrary"),
        name="even_in_seq",
    )(*args)


def _even_in_dec(x, gain, mod, wts, w_conv, li, prev0, prev1, nh, hd):
    g, l, d = x.shape
    cc = nh * hd
    e = li // 2
    w_main, w_f, bf, gq, gk = wts
    p = _even_consts(nh, hd, 8)[0]
    whole = lambda s: pl.BlockSpec(s, lambda i: (0, 0, 0))
    in_specs = [whole((1, l, d)), _layer(gain, li), _mod_spec(mod, li, 0, d, 1), _mod_spec(mod, li, 1, d, 1),
                _layer(w_main, e, 6 * cc), _layer(w_f, e), _layer(bf, e), _layer(gq, e), _layer(gk, e), _layer(w_conv, e),
                _full(p.shape), whole((1, l, cc)), whole((1, l, cc))]
    out_shape = [jax.ShapeDtypeStruct((1, l, cc), BF16)] + [jax.ShapeDtypeStruct((1, l, cc), F32)] * 4 \
        + [jax.ShapeDtypeStruct((1, l, LANES), F32)]
    out_specs = [whole((1, l, cc))] * 5 + [whole((1, l, LANES))]
    return pl.pallas_call(
        functools.partial(_even_dec_kernel, nh=nh, cc=cc),
        grid=(1,), in_specs=in_specs, out_specs=out_specs, out_shape=out_shape,
        compiler_params=_cparams("arbitrary"),
        name="even_in_dec",
    )(x, gain, mod, mod, w_main, w_f, bf, gq, gk, w_conv, p, prev0, prev1)


def _attn_kernel(it_ref, jt_ref, qp_ref, kp_ref, va_ref, o_ref, m_sc, acc_sc, s_sc, p_sc, *, nh, hd, tq):
    i = it_ref[pl.program_id(1)]
    j = jt_ref[pl.program_id(1)]
    reps = tq // LANES

    @pl.when(j == 0)
    def _():
        m_sc[...] = jnp.full(m_sc.shape, NEG, F32)
        acc_sc[...] = jnp.zeros_like(acc_sc)

    def scores(h):
        s_sc[h % 2] = _dot_nt(qp_ref[0, :, h * LANES:(h + 1) * LANES], kp_ref[0, :, h * LANES:(h + 1) * LANES])

    def step(diagonal):
        scores(0)
        for h in range(nh):
            if h + 1 < nh:
                scores(h + 1)
            s = s_sc[h % 2]
            if diagonal:
                row = lax.broadcasted_iota(jnp.int32, s.shape, 0)
                col = lax.broadcasted_iota(jnp.int32, s.shape, 1)
                s = jnp.where(col <= row, s, NEG)
            m_old = m_sc[h]
            m_new = jnp.maximum(m_old, jnp.max(s, axis=1, keepdims=True))
            alpha = jnp.exp2(m_old - m_new)
            p_sc[h % 2] = jnp.exp2(s - jnp.concatenate([m_new] * reps, axis=1)).astype(BF16)
            pv = _dot(p_sc[h % 2], va_ref[0, :, h * LANES:(h + 1) * LANES])
            acc_sc[h] = alpha * acc_sc[h] + pv
            m_sc[h] = m_new

    @pl.when(j < i)
    def _():
        step(False)

    @pl.when(j == i)
    def _():
        step(True)
        lane = lax.broadcasted_iota(jnp.int32, (tq, LANES), 1)
        half = LANES // 2
        for pair in range(nh // 2):
            a0 = acc_sc[2 * pair]
            a1 = acc_sc[2 * pair + 1]
            o0 = a0 / pltpu.roll(a0, half, axis=1)
            o1 = a1 / pltpu.roll(a1, half, axis=1)
            o_ref[0, :, pair * LANES:(pair + 1) * LANES] = jnp.where(
                lane < hd, o0, pltpu.roll(o1, half, axis=1)).astype(BF16)


def _attn_prompt(qp, kp, va, nh, hd, tq):
    g, l, _ = qp.shape
    cc = nh * hd
    assert 2 * hd == LANES, "value slots are [v | ones] halves of one 128-lane slot"
    nq = l // tq
    pairs =[(i, j) for i in range(nq) for j in range(i + 1)]
    it = jnp.asarray([p[0] for p in pairs], jnp.int32)
    jt = jnp.asarray([p[1] for p in pairs], jnp.int32)
    qrow = lambda b, p, it, jt: (b, it[p], 0)
    krow = lambda b, p, it, jt: (b, jt[p], 0)
    grid_spec = pltpu.PrefetchScalarGridSpec(
        num_scalar_prefetch=2, grid=(g, len(pairs)),
        in_specs=[pl.BlockSpec((1, tq, nh * LANES), qrow), pl.BlockSpec((1, tq, nh * LANES), krow),
                  pl.BlockSpec((1, tq, nh * LANES), krow)],
        out_specs=pl.BlockSpec((1, tq, cc), qrow),
        scratch_shapes=[pltpu.VMEM((nh, tq, LANES), F32), pltpu.VMEM((nh, tq, LANES), F32),
                        pltpu.VMEM((2, tq, tq), F32), pltpu.VMEM((2, tq, tq), BF16)])
    return pl.pallas_call(
        functools.partial(_attn_kernel, nh=nh, hd=hd, tq=tq),
        grid_spec=grid_spec,
        out_shape=jax.ShapeDtypeStruct((g, l, cc), BF16),
        compiler_params=_cparams("arbitrary", "arbitrary"),
        name="attn_prompt",
    )(it, jt, qp, kp, va)


def _attn_dec_kernel(pt_ref, qcol_ref, kncol_ref, vncol_ref, lfnew_ref, uincl_ref, ones_ref, *rest, npg, nh):
    k_refs = rest[:npg]
    v_refs = rest[npg:2 * npg]
    lf_refs = rest[2 * npg:3 * npg]
    o_ref = rest[3 * npg]
    m_sc, l_sc, acc_sc, c_sc = rest[3 * npg + 1:]
    g = pl.program_id(1)

    @pl.when(g == 0)
    def _():
        m_sc[...] = jnp.full(m_sc.shape, NEG, F32)
        l_sc[...] = jnp.zeros_like(l_sc)
        acc_sc[...] = jnp.zeros_like(acc_sc)
        c_sc[...] = jnp.zeros_like(c_sc)

    parts = []
    for j in range(npg):
        hi, mid, lo = _split3(lf_refs[j][0, 0])
        parts += [hi.astype(F32), mid.astype(F32), lo.astype(F32), jnp.zeros((nh, LANES), F32)]
    parts = jnp.concatenate(parts, axis=0).astype(BF16)
    cum = _dot(parts, uincl_ref[...])
    tot = _dot(parts, ones_ref[...])
    fs = []
    c = c_sc[...]
    for j in range(npg):
        r = 4 * nh * j
        fs.append(c + (cum[r:r + nh] + cum[r + nh:r + 2 * nh] + cum[r + 2 * nh:r + 3 * nh]))
        c = c + (tot[r:r + nh] + tot[r + nh:r + 2 * nh] + tot[r + 2 * nh:r + 3 * nh])
    c_sc[...] = c

    for h in range(nh):
        qc = qcol_ref[0, h]
        rows = [jnp.sum(qc * k_refs[j][0, 0, h], axis=0, keepdims=True) - fs[j][h:h + 1, :] for j in range(npg)]
        m_old = m_sc[h]
        m_new = jnp.maximum(m_old, jnp.max(functools.reduce(jnp.maximum, rows), axis=1, keepdims=True))
        alpha = jnp.exp(m_old - m_new)
        ps = [jnp.exp(r - m_new) for r in rows]
        l_sc[h] = alpha * l_sc[h] + jnp.sum(functools.reduce(jnp.add, ps), axis=1, keepdims=True)
        acc = alpha * acc_sc[h]
        for j in range(npg):
            acc = acc + ps[j] * v_refs[j][0, 0, h]
        acc_sc[h] = acc
        m_sc[h] = m_new

    @pl.when(g == pl.num_programs(1) - 1)
    def _():
        lane = lax.broadcasted_iota(jnp.int32, (1, LANES), 1)
        for h in range(nh):
            f_new = c_sc[h:h + 1, :] + lfnew_ref[0, h:h + 1, :]
            s_new = jnp.sum(qcol_ref[0, h] * kncol_ref[0, h], axis=0, keepdims=True) - f_new
            s_new = jnp.where(lane == 0, s_new, NEG)
            m_old = m_sc[h]
            m_new = jnp.maximum(m_old, jnp.max(s_new, axis=1, keepdims=True))
            alpha = jnp.exp(m_old - m_new)
            p_new = jnp.exp(s_new - m_new)
            l_fin = alpha * l_sc[h] + jnp.sum(p_new, axis=1, keepdims=True)
            acc = (alpha * acc_sc[h] + p_new * vncol_ref[0, h]) / l_fin
            hi, mid, lo = _split3(acc)
            ones = ones_ref[0:ONES_ROWS, :]
            o = _dot_nt(ones, hi) + _dot_nt(ones, mid) + _dot_nt(ones, lo)
            o_ref[0, h:h + 1, :] = o[0:1, :]


def _attn_decode(page_table, q, knew, vnew, lfnew, cache_kt, cache_vt, cache_lft, e, nh, hd):
    n_seq, n_pages = page_table.shape
    page = cache_kt.shape[-1]
    assert page == LANES and n_pages % PAGES_PER_STEP == 0
    npg = PAGES_PER_STEP
    col = lambda a: jnp.broadcast_to(a.reshape(n_seq, nh, hd, 1), (n_seq, nh, hd, LANES))
    uincl = jnp.asarray(np.triu(np.ones((page, page))), BF16)
    ones = jnp.ones((LANES, LANES), BF16)

    def pg(j, nd):
        return lambda b, g, pt: (pt[b, g * npg + j], e) + (0,) * nd

    per_seq4 = lambda b, g, pt: (b, 0, 0, 0)
    per_seq3 = lambda b, g, pt: (b, 0, 0)
    const2 = lambda b, g, pt: (0, 0)
    in_specs = [pl.BlockSpec((1, nh, hd, LANES), per_seq4), pl.BlockSpec((1, nh, hd, LANES), per_seq4),
                pl.BlockSpec((1, nh, hd, LANES), per_seq4), pl.BlockSpec((1, nh, 1), per_seq3),
                pl.BlockSpec((page, page), const2), pl.BlockSpec((LANES, LANES), const2)]
    in_specs += [pl.BlockSpec((1, 1, nh, hd, page), pg(j, 3)) for j in range(npg)]
    in_specs += [pl.BlockSpec((1, 1, nh, hd, page), pg(j, 3)) for j in range(npg)]
    in_specs += [pl.BlockSpec((1, 1, nh, page), pg(j, 2)) for j in range(npg)]
    grid_spec = pltpu.PrefetchScalarGridSpec(
        num_scalar_prefetch=1, grid=(n_seq, n_pages // npg), in_specs=in_specs,
        out_specs=pl.BlockSpec((1, nh, hd), per_seq3),
        scratch_shapes=[pltpu.VMEM((nh, 1, LANES), F32), pltpu.VMEM((nh, 1, LANES), F32),
                        pltpu.VMEM((nh, hd, LANES), F32), pltpu.VMEM((nh, LANES), F32)])
    o = pl.pallas_call(
        functools.partial(_attn_dec_kernel, npg=npg, nh=nh),
        grid_spec=grid_spec,
        out_shape=jax.ShapeDtypeStruct((n_seq, nh, hd), F32),
        compiler_params=_cparams("arbitrary", "arbitrary"),
        name="attn_decode",
    )(page_table, col(q), col(knew), col(vnew), lfnew.reshape(n_seq, nh, 1), uincl, ones,
      *([cache_kt] * npg), *([cache_vt] * npg), *([cache_lft] * npg))
    return o.reshape(n_seq, 1, nh * hd)


def _odd_in_kernel(x_ref, g_ref, sh_ref, sc_ref, w_ref, wgd_ref, wgk_ref, bgk_ref,
                   q_ref, k_ref, v_ref, sr_ref, gl_ref, *, qk, vv, dk):
    hb = _norm_mod(x_ref[0], g_ref[...], sh_ref[0], sc_ref[0]).astype(BF16)
    q_ref[0] = _dot(hb, w_ref[:, 0:qk]) * (dk ** -0.5)
    k_ref[0] = _dot(hb, w_ref[:, qk:2 * qk])
    gd = _dot(hb, wgd_ref[...]).astype(BF16)
    gl_ref[0] = _log_sigmoid(_dot(gd, wgk_ref[...]) + bgk_ref[...]) * (1.0 / GLA_TAU)
    v_ref[0] = _dot(hb, w_ref[:, 2 * qk:2 * qk + vv]).astype(BF16)
    sr_ref[0] = _silu(_dot(hb, w_ref[:, 2 * qk + vv:2 * qk + 2 * vv]))


def _gla_seq_kernel(q_ref, k_ref, g_ref, v_ref, sr_ref, go_ref, tri_ref, o_ref, st_ref, s_sc, *, nh, dk, dv, tt):
    @pl.when(pl.program_id(1) == 0)
    def _():
        s_sc[...] = jnp.zeros_like(s_sc)

    o_ref[0] = _gla_tile(q_ref[0], k_ref[0], g_ref[0], v_ref[0], sr_ref[0], go_ref, tri_ref, s_sc, nh, dk, dv, tt)
    st_ref[0] = s_sc[...]


def _odd_weights(w_in, w_gk, b_gk, qk, vv):
    n, d, _ = w_in.shape
    rank = w_gk.shape[1]
    w_main = w_in.astype(BF16)
    w_gd = jnp.zeros((n, d, LANES), BF16).at[:, :, :rank].set(w_in[:, :, 2 * qk + 2 * vv:].astype(BF16))
    wgk = jnp.zeros((n, LANES, qk), BF16).at[:, :rank].set(w_gk.astype(BF16))
    return w_main, w_gd, wgk, b_gk.reshape(n, 1, qk)


def _odd_in(x, gain, mod, wts, li, qk, vv, dk, tm):
    g, l, d = x.shape
    oi = li // 2
    w_main, w_gd, wgk, bgk = wts
    row = lambda b, i: (b, i, 0)
    in_specs = [pl.BlockSpec((1, tm, d), row), _layer(gain, li), _mod_spec(mod, li, 0, d, 2), _mod_spec(mod, li, 1, d, 2),
                _layer(w_main, oi, 2 * qk + 2 * vv), _layer(w_gd, oi), _layer(wgk, oi), _layer(bgk, oi)]
    out_shape = [jax.ShapeDtypeStruct((g, l, qk), F32), jax.ShapeDtypeStruct((g, l, qk), F32),
                 jax.ShapeDtypeStruct((g, l, vv), BF16), jax.ShapeDtypeStruct((g, l, vv), F32),
                 jax.ShapeDtypeStruct((g, l, qk), F32)]
    out_specs = [pl.BlockSpec((1, tm, qk), row), pl.BlockSpec((1, tm, qk), row), pl.BlockSpec((1, tm, vv), row),
                 pl.BlockSpec((1, tm, vv), row), pl.BlockSpec((1, tm, qk), row)]
    return pl.pallas_call(
        functools.partial(_odd_in_kernel, qk=qk, vv=vv, dk=dk),
        grid=(g, l // tm), in_specs=in_specs, out_specs=out_specs, out_shape=out_shape,
        compiler_params=_cparams("arbitrary", "arbitrary"),
        name="odd_in",
    )(x, gain, mod, mod, w_main, w_gd, wgk, bgk)


def _gla_tile(q, k, gl, v, sr, go_ref, tri_ref, s_sc, nh, dk, dv, tt):
    c = GLA_CHUNK
    qk = nh * dk
    b_all = _sum3(_dot(tri_ref[...], jnp.concatenate(_split3(gl), axis=1)), qk)
    nchunk = tt // c
    row = lax.broadcasted_iota(jnp.int32, (tt, tt), 0)
    col = lax.broadcasted_iota(jnp.int32, (tt, tt), 1)
    blockmask = (col <= row) & (col >= (row // c) * c)
    b_last = [b_all[(n + 1) * c - 1:(n + 1) * c, :] for n in range(nchunk)]
    b_last_rows = jnp.concatenate([jnp.broadcast_to(bl, (c, qk)) for bl in b_last], axis=0)
    q_e = (q * jnp.exp(b_all)).astype(BF16)
    k_e = (k * jnp.exp(-b_all)).astype(BF16)
    k_d = (k * jnp.exp(b_last_rows - b_all)).astype(BF16)
    decay = [jnp.exp(bl) for bl in b_last]
    out = []
    for h in range(nh):
        kc = slice(h * dk, (h + 1) * dk)
        vc = slice(h * dv, (h + 1) * dv)
        v_h = v[:, vc]
        a = jnp.where(blockmask, _dot_nt(q_e[:, kc], k_e[:, kc]), 0.0)
        o_intra = _dot(a.astype(BF16), v_h)
        s_t = s_sc[h]
        o_inter = []
        for n in range(nchunk):
            rows = slice(n * c, (n + 1) * c)
            o_inter.append(_dot_nt(q_e[rows, kc], s_t.astype(BF16)))
            s_t = s_t * decay[n][:, kc] + _dot_tn(v_h[rows], k_d[rows, kc])
        s_sc[h] = s_t
        o = o_intra + jnp.concatenate(o_inter, axis=0)
        y = o * lax.rsqrt(jnp.mean(o * o, axis=-1, keepdims=True) + EPS) * go_ref[...]
        out.append((y * sr[:, vc]).astype(BF16))
    return jnp.concatenate(out, axis=1)


def _gla_seq(q, k, gl, v, sr, g_o, oi, nh, dk, dv, tt):
    g, l, qk = q.shape
    vv = nh * dv
    tri = jnp.asarray(np.kron(np.eye(tt // GLA_CHUNK), np.tril(np.ones((GLA_CHUNK, GLA_CHUNK)))), BF16)
    row = lambda b, i: (b, i, 0)
    return pl.pallas_call(
        functools.partial(_gla_seq_kernel, nh=nh, dk=dk, dv=dv, tt=tt),
        grid=(g, l // tt),
        in_specs=[pl.BlockSpec((1, tt, qk), row), pl.BlockSpec((1, tt, qk), row), pl.BlockSpec((1, tt, qk), row),
                  pl.BlockSpec((1, tt, vv), row), pl.BlockSpec((1, tt, vv), row), _layer(g_o, oi), _full((tt, tt))],
        out_specs=[pl.BlockSpec((1, tt, vv), row), pl.BlockSpec((1, nh, dv, dk), lambda b, i: (b, 0, 0, 0))],
        out_shape=[jax.ShapeDtypeStruct((g, l, vv), BF16), jax.ShapeDtypeStruct((g, nh, dv, dk), F32)],
        scratch_shapes=[pltpu.VMEM((nh, dv, dk), F32)],
        compiler_params=_cparams("arbitrary", "arbitrary"),
        name="gla_seq",
    )(q, k, gl, v, sr, g_o, tri)


def _gla_dec_kernel(q_ref, k_ref, g_ref, v_ref, sr_ref, go_ref, s0_ref, *rest, nh, dk, dv, aliased):
    o_ref, s_ref = rest[1:] if aliased else rest

    def column(r):
        return jnp.broadcast_to(r, (dk, dk)).T

    for h in range(nh):
        kc = slice(h * dk, (h + 1) * dk)
        dcol = jnp.exp(column(g_ref[0, :, kc]))
        kcol = column(k_ref[0, :, kc])
        qcol = column(q_ref[0, :, kc])
        halves = []
        for half in range(dv // dk):
            lanes = slice(half * dk, (half + 1) * dk)
            v = v_ref[0, :, h * dv + half * dk:h * dv + (half + 1) * dk].astype(F32)
            s_new = dcol * s0_ref[0, 0, h, :, lanes] + kcol * v
            s_ref[0, h, :, lanes] = s_new
            halves.append(jnp.sum(qcol * s_new, axis=0, keepdims=True))
        o = jnp.concatenate(halves, axis=1)
        y = o * lax.rsqrt(jnp.mean(o * o, axis=-1, keepdims=True) + EPS) * go_ref[...]
        vc = slice(h * dv, (h + 1) * dv)
        o_ref[0, :, vc] = y * sr_ref[0, :, vc]


def _gla_dec(q, k, gl, v, sr, g_o, state_gla, o_idx, s_prev, nh, dk, dv):
    n_seq = q.shape[0]
    qk, vv = nh * dk, nh * dv
    r3 = lambda a: a.reshape(n_seq, 1, a.shape[-1])
    per_seq = lambda b: (b, 0, 0)
    in_specs = [pl.BlockSpec((1, 1, qk), per_seq), pl.BlockSpec((1, 1, qk), per_seq),
                pl.BlockSpec((1, 1, qk), per_seq), pl.BlockSpec((1, 1, vv), per_seq),
                pl.BlockSpec((1, 1, vv), per_seq), pl.BlockSpec((1, dv), lambda b: (0, 0)),
                pl.BlockSpec((1, 1, nh, dk, dv), lambda b: (b, o_idx, 0, 0, 0))]
    args = [r3(q), r3(k), r3(gl), r3(v), r3(sr), g_o.reshape(1, dv), state_gla]
    aliases = {}
    if s_prev is not None:
        in_specs.append(pl.BlockSpec(memory_space=pl.ANY))
        args.append(s_prev)
        aliases = {len(args) - 1: 1}
    return pl.pallas_call(
        functools.partial(_gla_dec_kernel, nh=nh, dk=dk, dv=dv, aliased=s_prev is not None),
        grid=(n_seq,),
        in_specs=in_specs,
        out_specs=[pl.BlockSpec((1, 1, vv), per_seq),
                   pl.BlockSpec((1, None, nh, dk, dv), lambda b: (b, o_idx, 0, 0, 0))],
        out_shape=[jax.ShapeDtypeStruct((n_seq, 1, vv), F32), jax.ShapeDtypeStruct(state_gla.shape, F32)],
        input_output_aliases=aliases,
        compiler_params=_cparams("arbitrary"),
        name="gla_dec",
    )(*args)


def _post_kernel(x_ref, a1_ref, a2_ref, wo_ref, gtm_ref, gf_ref, shf_ref, scf_ref, gtf_ref, w1_ref, w2_ref,
                 o_ref, h_sc, *, ka):
    f = pl.program_id(2)

    @pl.when(f == 0)
    def _():
        m = _dot(a1_ref[0], wo_ref[0:ka, :]) + _dot(a2_ref[0], wo_ref[ka:2 * ka, :])
        x1 = x_ref[0] + gtm_ref[0] * m
        o_ref[0] = x1
        h_sc[...] = _norm_mod(x1, gf_ref[...], shf_ref[0], scf_ref[0]).astype(BF16)

    t = jnp.maximum(_dot(h_sc[...], w1_ref[...]), 0.0)
    o_ref[0] += gtf_ref[0] * _dot((t * t).astype(BF16), w2_ref[...])


def _post(x, a1, a1_col, a2, a2_col, w_out, gain_f, mod, w1, w2, li, tm, tf):
    g, l, d = x.shape
    ka = w_out.shape[1] // 2
    dff = w1.shape[2]
    row = lambda b, i, f: (b, i, 0)
    modc = lambda c: _mod_spec(mod, li, c, d, 3)
    return pl.pallas_call(
        functools.partial(_post_kernel, ka=ka),
        grid=(g, l // tm, dff // tf),
        in_specs=[pl.BlockSpec((1, tm, d), row),
                  pl.BlockSpec((1, tm, ka), lambda b, i, f: (b, i, a1_col)),
                  pl.BlockSpec((1, tm, ka), lambda b, i, f: (b, i, a2_col)),
                  _layer(w_out, li // 2), modc(2), _layer(gain_f, li), modc(3), modc(4), modc(5),
                  pl.BlockSpec((None, d, tf), lambda b, i, f: (li, 0, f)),
                  pl.BlockSpec((None, tf, d), lambda b, i, f: (li, f, 0))],
        out_specs=pl.BlockSpec((1, tm, d), row),
        out_shape=jax.ShapeDtypeStruct((g, l, d), F32),
        scratch_shapes=[pltpu.VMEM((tm, d), BF16)],
        compiler_params=_cparams("arbitrary", "arbitrary", "arbitrary"),
        name="post_mlp",
    )(x, a1, a2, w_out, mod, gain_f, mod, mod, mod, w1, w2)


def kernel(x_prompt, x_sample, c_prompt, c_sample, cache_k, cache_v, cache_logf, page_table, state_conv, state_gla,
           g_mix, g_ffn, w_ada, b_ada, w_in_even, b_f, g_q, g_k, w_conv, w_out_even, w_in_odd, w_gk, b_gk, g_o,
           w_out_odd, w_ff1, w_ff2):
    bp, seq, d = x_prompt.shape
    n_seq = x_sample.shape[0]
    depth = w_ada.shape[0]
    n_even, nh_fox = b_f.shape
    hd_fox = g_q.shape[1]
    cc = nh_fox * hd_fox
    _, n_odd, nh_gla, dk, dv = state_gla.shape
    qk, vv = nh_gla * dk, nh_gla * dv

    mod_all = _adaln(jnp.concatenate([c_prompt, c_sample], axis=0), w_ada, b_ada)

    cache_kt = jnp.transpose(cache_k, (0, 1, 3, 4, 2))
    cache_vt = jnp.transpose(cache_v, (0, 1, 3, 4, 2))
    cache_lft = jnp.swapaxes(cache_logf, 2, 3)

    mod_p = mod_all[:, :bp].reshape(depth, bp, 1, 6 * d)
    mod_s = mod_all[:, bp:].reshape(depth, 1, n_seq, 6 * d)
    gm = g_mix.reshape(depth, 1, d)
    gf = g_ffn.reshape(depth, 1, d)
    w1 = w_ff1.astype(BF16)
    w2 = w_ff2.astype(BF16)
    wts_even = _even_weights(w_in_even, b_f, g_q, g_k, nh_fox, hd_fox)
    wts_odd = _odd_weights(w_in_odd, w_gk, b_gk, qk, vv)
    wo_even = w_out_even.astype(BF16)
    wo_odd = w_out_odd.astype(BF16)
    go = g_o.reshape(n_odd, 1, dv)

    xp = x_prompt
    xs = x_sample.reshape(1, n_seq, d)
    kv_rows = None
    gla_s = None
    lf_rows_p, fox_s, conv_p, conv_s, gla_p = [], [], [], [], []
    for li in range(depth):
        if li % 2 == 0:
            e = li // 2
            ya, utail, qp, kp, va, kt_rows, vt_rows, lf_rows = _even_in_seq(
                xp, gm, mod_p, wts_even, w_conv, li, kv_rows, nh_fox, hd_fox, tm=TM_EVEN)
            kv_rows = (kt_rows, vt_rows)
            o = _attn_prompt(qp, kp, va, nh_fox, hd_fox, tq=TQ_ATTN)
            xp = _post(xp, ya, 0, o, 0, wo_even, gf, mod_p, w1, w2, li, tm=TM_POST, tf=TF_POST)
            lf_rows_p.append(lf_rows)
            conv_p.append(utail[:, 6:8])
            prev0 = state_conv[:, e, 0].reshape(1, n_seq, cc)
            prev1 = state_conv[:, e, 1].reshape(1, n_seq, cc)
            ya, u, qn, kn, v, lfpad = _even_in_dec(xs, gm, mod_s, wts_even, w_conv, li, prev0, prev1, nh_fox, hd_fox)
            lf_new = lfpad[0, :, :nh_fox]
            o = _attn_decode(page_table, qn[0], kn[0], v[0], lf_new, cache_kt, cache_vt, cache_lft, e,
                             nh_fox, hd_fox)
            o = o.reshape(1, n_seq, cc).astype(BF16)
            xs = _post(xs, ya, 0, o, 0, wo_even, gf, mod_s, w1, w2, li, tm=n_seq, tf=TF_POST)
            fox_s.append((kn[0], v[0], lf_new))
            conv_s.append(jnp.stack([prev1[0], u[0]], axis=1))
        else:
            oi = li // 2
            q, k, v, sr, gl = _odd_in(xp, gm, mod_p, wts_odd, li, qk, vv, dk, tm=TM_ODD)
            og, s_t = _gla_seq(q, k, gl, v, sr, go, oi, nh_gla, dk, dv, tt=TT_GLA)
            xp = _post(xp, og, 0, og, 1, wo_odd, gf, mod_p, w1, w2, li, tm=TM_POST, tf=TF_POST)
            gla_p.append(jnp.swapaxes(s_t, 2, 3))
            q, k, v, sr, gl = _odd_in(xs, gm, mod_s, wts_odd, li, qk, vv, dk, tm=n_seq)
            og, gla_s = _gla_dec(q[0], k[0], gl[0], v[0].astype(F32), sr[0], g_o[oi], state_gla, oi, gla_s,
                                 nh_gla, dk, dv)
            og = og.reshape(1, n_seq, vv).astype(BF16)
            xs = _post(xs, og, 0, og, 1, wo_odd, gf, mod_s, w1, w2, li, tm=n_seq, tf=TF_POST)

    heads = lambda a, n: a.reshape(a.shape[0], n, nh_fox, hd_fox)
    k_p = jnp.transpose(kv_rows[0], (0, 1, 4, 2, 3))
    v_p = jnp.transpose(kv_rows[1], (0, 1, 4, 2, 3))
    lf_p = jnp.stack(lf_rows_p, axis=1)
    k_s = jnp.stack([heads(r[0], 1) for r in fox_s], axis=1)
    v_s = jnp.stack([heads(r[1], 1) for r in fox_s], axis=1)
    lf_s = jnp.stack([r[2].reshape(n_seq, 1, nh_fox) for r in fox_s], axis=1)
    return (xp, xs.reshape(n_seq, 1, d), k_p, v_p, lf_p, k_s, v_s, lf_s,
            jnp.stack(conv_p, axis=1), jnp.stack(conv_s, axis=1),
            jnp.stack(gla_p, axis=1), gla_s)
```

```python
import functools

import numpy as np
import jax
import jax.numpy as jnp
from jax import lax
from jax.experimental import pallas as pl
from jax.experimental.pallas import tpu as pltpu

F32 = jnp.float32
BF16 = jnp.bfloat16

EPS = 1e-6
GLA_TAU = 16.0
GLA_CHUNK = 64
LANES = 128
VMEM_LIMIT = 56 * 1024 * 1024
NEG = -1e30
LOG2E = 1.4426950408889634
PAGES_PER_STEP = 16
ONES_ROWS = 16
TM_EVEN = 512
TM_ODD = 512
TM_POST = 1024
TF_POST = 1024
TQ_ATTN = 1024
TT_GLA = 256


def _cparams(*sem):
    return pltpu.CompilerParams(dimension_semantics=sem, vmem_limit_bytes=VMEM_LIMIT)


def _silu(x):
    return x / (1.0 + jnp.exp(-x))


def _log_sigmoid(z):
    return jnp.minimum(z, 0.0) - jnp.log1p(jnp.exp(-jnp.abs(z)))


def _norm_mod(x, gain, shift, scale):
    y = x * lax.rsqrt(jnp.mean(x * x, axis=-1, keepdims=True) + EPS)
    return (y * gain) * (1.0 + scale) + shift


def _split3(a):
    hi = a.astype(BF16)
    r = a - hi.astype(F32)
    mid = r.astype(BF16)
    lo = (r - mid.astype(F32)).astype(BF16)
    return hi, mid, lo


def _dot(a, b):
    return jnp.dot(a, b, preferred_element_type=F32)


def _dot_nt(a, b):
    return lax.dot_general(a, b, (((1,), (1,)), ((), ())), preferred_element_type=F32)


def _dot_tn(a, b):
    return lax.dot_general(a, b, (((0,), (0,)), ((), ())), preferred_element_type=F32)


def _sum3(c, w):
    return c[:, 0:w] + c[:, w:2 * w] + c[:, 2 * w:3 * w]


def _adaln_kernel(c_ref, w_ref, b_ref, o_ref):
    a = _silu(c_ref[...]).astype(BF16)
    o_ref[0] = _dot(a, w_ref[0].astype(BF16)) + b_ref[0]


def _adaln(c_all, w_ada, b_ada):
    depth, d, n = w_ada.shape
    r = c_all.shape[0]
    tn = n // 4
    return pl.pallas_call(
        _adaln_kernel,
        grid=(depth, n // tn),
        in_specs=[pl.BlockSpec((r, d), lambda l, j: (0, 0)),
                  pl.BlockSpec((1, d, tn), lambda l, j: (l, 0, j)),
                  pl.BlockSpec((1, 1, tn), lambda l, j: (l, 0, j))],
        out_specs=pl.BlockSpec((1, r, tn), lambda l, j: (l, 0, j)),
        out_shape=jax.ShapeDtypeStruct((depth, r, n), F32),
        compiler_params=_cparams("arbitrary", "arbitrary"),
        name="adaln",
    )(c_all, w_ada, b_ada.reshape(depth, 1, n))


def _even_project(x_ref, g_ref, sh_ref, sc_ref, w_ref, wf_ref, bf_ref, gq_ref, gk_ref, p_ref, cc, nh):
    hb = _norm_mod(x_ref[0], g_ref[...], sh_ref[0], sc_ref[0]).astype(BF16)
    gb = _dot(hb, w_ref[:, 0:cc])
    u = _dot(hb, w_ref[:, cc:2 * cc]) * _dot(hb, w_ref[:, 2 * cc:3 * cc])
    q = _dot(hb, w_ref[:, 3 * cc:4 * cc])
    k = _dot(hb, w_ref[:, 4 * cc:5 * cc])
    v = _dot(hb, w_ref[:, 5 * cc:6 * cc])
    hd = cc // nh
    qms = _dot((q * q).astype(BF16), p_ref[...]) * (1.0 / hd)
    kms = _dot((k * k).astype(BF16), p_ref[...]) * (1.0 / hd)
    qn = (q * lax.rsqrt(qms + EPS)) * gq_ref[...]
    kn = (k * lax.rsqrt(kms + EPS)) * gk_ref[...]
    fz = _dot(hb, wf_ref[...]) + bf_ref[...]
    lane = lax.broadcasted_iota(jnp.int32, fz.shape, 1)
    lf = jnp.where(lane < nh, _log_sigmoid(fz), 0.0)
    return gb, u, qn, kn, v, lf


def _even_seq_kernel(x_ref, g_ref, sh_ref, sc_ref, w_ref, wf_ref, bf_ref, gq_ref, gk_ref, wc_ref, p_ref,
                     tri_ref, eq_ref, ek_ref, oq_ref, ok_ref, *rest, nh, cc, tm, aliased):
    if aliased:
        rest = rest[2:]
    ya_ref, ut_ref, qp_ref, kp_ref, va_ref, ko_ref, vo_ref, lf_ref, ubuf, fcarry = rest
    i = pl.program_id(1)
    gb, u, qn, kn, v, lf = _even_project(x_ref, g_ref, sh_ref, sc_ref, w_ref, wf_ref, bf_ref,
                                         gq_ref, gk_ref, p_ref, cc, nh)

    @pl.when(i == 0)
    def _():
        ubuf[0:8, :] = jnp.zeros((8, cc), F32)
        fcarry[...] = jnp.zeros_like(fcarry)

    ubuf[8:8 + tm, :] = u
    conv = wc_ref[0:1, :] * ubuf[6:6 + tm, :] + wc_ref[1:2, :] * ubuf[7:7 + tm, :] + wc_ref[2:3, :] * u
    ya_ref[0] = (gb * conv).astype(BF16)
    tail = ubuf[tm:tm + 8, :]
    ubuf[0:8, :] = tail
    ut_ref[0] = tail

    ko_ref[0, 0] = kn.T.reshape(nh, cc // nh, tm)
    vo_ref[0, 0] = v.T.reshape(nh, cc // nh, tm)
    lf_ref[0] = lf[:, 0:nh]
    vb = v.astype(BF16)
    va_ref[0] = jnp.ones((tm, nh * LANES), BF16)
    for h in range(nh):
        va_ref[0, :, h * LANES:h * LANES + cc // nh] = vb[:, h * (cc // nh):(h + 1) * (cc // nh)]

    c3 = _dot(tri_ref[...], jnp.concatenate(_split3(lf), axis=1))
    f = _sum3(c3, LANES) + fcarry[...]
    fcarry[...] = f[tm - 1:tm, :]
    f3 = jnp.concatenate(_split3(f * LOG2E), axis=1)

    qp_ref[0] = (_dot(f3, eq_ref[...]) + oq_ref[...]).astype(BF16)
    kp_ref[0] = (_dot(f3, ek_ref[...]) + ok_ref[...]).astype(BF16)
    qb = qn.astype(BF16)
    kb = kn.astype(BF16)
    hd = cc // nh
    for h in range(nh):
        qp_ref[0, :, h * LANES:h * LANES + hd] = qb[:, h * hd:(h + 1) * hd]
        kp_ref[0, :, h * LANES:h * LANES + hd] = kb[:, h * hd:(h + 1) * hd]


def _even_dec_kernel(x_ref, g_ref, sh_ref, sc_ref, w_ref, wf_ref, bf_ref, gq_ref, gk_ref, wc_ref, p_ref,
                     p0_ref, p1_ref,
                     ya_ref, u_ref, q_ref, k_ref, v_ref, lf_ref, *, nh, cc):
    gb, u, qn, kn, v, lf = _even_project(x_ref, g_ref, sh_ref, sc_ref, w_ref, wf_ref, bf_ref,
                                         gq_ref, gk_ref, p_ref, cc, nh)
    conv = wc_ref[0:1, :] * p0_ref[0] + wc_ref[1:2, :] * p1_ref[0] + wc_ref[2:3, :] * u
    ya_ref[0] = (gb * conv).astype(BF16)
    u_ref[0] = u
    q_ref[0] = qn
    k_ref[0] = kn
    v_ref[0] = v
    lf_ref[0] = lf


def _even_consts(nh, hd, tm):
    cc = nh * hd
    p = np.kron(np.eye(nh), np.ones((hd, hd)))
    tri = np.tril(np.ones((tm, tm)))
    eq = np.zeros((3 * LANES, nh * LANES))
    ek = np.zeros((3 * LANES, nh * LANES))
    oq = np.zeros((1, nh * LANES))
    ok = np.zeros((1, nh * LANES))
    for h in range(nh):
        for part in range(3):
            eq[part * LANES + h, h * LANES + hd + part] = 1.0
            ek[part * LANES + h, h * LANES + hd + 3 + part] = -1.0
            oq[0, h * LANES + hd + 3 + part] = 1.0
            ok[0, h * LANES + hd + part] = 1.0
    bf = lambda a: jnp.asarray(a, BF16)
    return bf(p), bf(tri), bf(eq), bf(ek), jnp.asarray(oq, F32), jnp.asarray(ok, F32)


def _even_weights(w_in, b_f, g_q, g_k, nh, hd):
    n, d, _ = w_in.shape
    cc = nh * hd
    w_main = w_in.astype(BF16)
    w_f = jnp.zeros((n, d, LANES), BF16).at[:, :, :nh].set(w_in[:, :, 6 * cc:].astype(BF16))
    bf = jnp.zeros((n, 1, LANES), F32).at[:, 0, :nh].set(b_f)
    gq = (jnp.tile(g_q, (1, nh)) * (hd ** -0.5)).reshape(n, 1, cc)
    gk = jnp.tile(g_k, (1, nh)).reshape(n, 1, cc)
    return w_main, w_f, bf, gq, gk


def _full(shape):
    nd = len(shape)
    return pl.BlockSpec(shape, lambda *_: (0,) * nd)


def _layer(a, idx, width=None):
    nd = a.ndim
    shape = a.shape[1:] if width is None else a.shape[1:-1] + (width,)
    return pl.BlockSpec((None,) + shape, lambda *_: (idx,) + (0,) * (nd - 1))


def _mod_spec(mod, li, col, d, nargs):
    r = mod.shape[2]
    if nargs == 1:
        return pl.BlockSpec((None, 1, r, d), lambda i: (li, 0, 0, col))
    if nargs == 2:
        return pl.BlockSpec((None, 1, r, d), lambda b, i: (li, b, 0, col))
    return pl.BlockSpec((None, 1, r, d), lambda b, i, f: (li, b, 0, col))


def _even_in_seq(x, gain, mod, wts, w_conv, li, kv_prev, nh, hd, tm):
    g, l, d = x.shape
    cc = nh * hd
    e = li // 2
    n_even = w_conv.shape[0]
    w_main, w_f, bf, gq, gk = wts
    gq = gq * LOG2E
    p, tri, eq, ek, oq, ok = _even_consts(nh, hd, tm)
    row = lambda b, i: (b, i, 0)
    in_specs = [pl.BlockSpec((1, tm, d), row), _layer(gain, li), _mod_spec(mod, li, 0, d, 2), _mod_spec(mod, li, 1, d, 2),
                _layer(w_main, e, 6 * cc), _layer(w_f, e), _layer(bf, e), _layer(gq, e), _layer(gk, e), _layer(w_conv, e),
                _full(p.shape), _full(tri.shape), _full(eq.shape), _full(ek.shape), _full(oq.shape), _full(ok.shape)]
    args = [x, gain, mod, mod, w_main, w_f, bf, gq, gk, w_conv, p, tri, eq, ek, oq, ok]
    kt_shape = jax.ShapeDtypeStruct((g, n_even, nh, hd, l), F32)
    kt_spec = pl.BlockSpec((1, 1, nh, hd, tm), lambda b, i: (b, e, 0, 0, i))
    out_shape = [jax.ShapeDtypeStruct((g, l, cc), BF16),
                 jax.ShapeDtypeStruct((g, 8, cc), F32),
                 jax.ShapeDtypeStruct((g, l, nh * LANES), BF16),
                 jax.ShapeDtypeStruct((g, l, nh * LANES), BF16),
                 jax.ShapeDtypeStruct((g, l, nh * LANES), BF16),
                 kt_shape, kt_shape,
                 jax.ShapeDtypeStruct((g, l, nh), F32)]
    out_specs = [pl.BlockSpec((1, tm, cc), row), pl.BlockSpec((1, 8, cc), lambda b, i: (b, 0, 0)),
                 pl.BlockSpec((1, tm, nh * LANES), row), pl.BlockSpec((1, tm, nh * LANES), row),
                 pl.BlockSpec((1, tm, nh * LANES), row), kt_spec, kt_spec, pl.BlockSpec((1, tm, nh), row)]
    aliases = {}
    if kv_prev is not None:
        in_specs += [pl.BlockSpec(memory_space=pl.ANY)] * 2
        args += list(kv_prev)
        aliases = {len(args) - 2: 5, len(args) - 1: 6}
    return pl.pallas_call(
        functools.partial(_even_seq_kernel, nh=nh, cc=cc, tm=tm, aliased=kv_prev is not None),
        grid=(g, l // tm), in_specs=in_specs, out_specs=out_specs, out_shape=out_shape,
        scratch_shapes=[pltpu.VMEM((tm + 8, cc), F32), pltpu.VMEM((1, LANES), F32)],
        input_output_aliases=aliases,
        compiler_params=_cparams("arbitrary", "arbitrary"),
        name="even_in_seq",
    )(*args)


def _even_in_dec(x, gain, mod, wts, w_conv, li, prev0, prev1, nh, hd):
    g, l, d = x.shape
    cc = nh * hd
    e = li // 2
    w_main, w_f, bf, gq, gk = wts
    p = _even_consts(nh, hd, 8)[0]
    whole = lambda s: pl.BlockSpec(s, lambda i: (0, 0, 0))
    in_specs = [whole((1, l, d)), _layer(gain, li), _mod_spec(mod, li, 0, d, 1), _mod_spec(mod, li, 1, d, 1),
                _layer(w_main, e, 6 * cc), _layer(w_f, e), _layer(bf, e), _layer(gq, e), _layer(gk, e), _layer(w_conv, e),
                _full(p.shape), whole((1, l, cc)), whole((1, l, cc))]
    out_shape = [jax.ShapeDtypeStruct((1, l, cc), BF16)] + [jax.ShapeDtypeStruct((1, l, cc), F32)] * 4 \
        + [jax.ShapeDtypeStruct((1, l, LANES), F32)]
    out_specs = [whole((1, l, cc))] * 5 + [whole((1, l, LANES))]
    return pl.pallas_call(
        functools.partial(_even_dec_kernel, nh=nh, cc=cc),
        grid=(1,), in_specs=in_specs, out_specs=out_specs, out_shape=out_shape,
        compiler_params=_cparams("arbitrary"),
        name="even_in_dec",
    )(x, gain, mod, mod, w_main, w_f, bf, gq, gk, w_conv, p, prev0, prev1)


def _attn_kernel(it_ref, jt_ref, qp_ref, kp_ref, va_ref, o_ref, m_sc, acc_sc, s_sc, p_sc, *, nh, hd, tq):
    i = it_ref[pl.program_id(1)]
    j = jt_ref[pl.program_id(1)]
    reps = tq // LANES

    @pl.when(j == 0)
    def _():
        m_sc[...] = jnp.full(m_sc.shape, NEG, F32)
        acc_sc[...] = jnp.zeros_like(acc_sc)

    def scores(h):
        s_sc[h % 2] = _dot_nt(qp_ref[0, :, h * LANES:(h + 1) * LANES], kp_ref[0, :, h * LANES:(h + 1) * LANES])

    def step(diagonal):
        scores(0)
        for h in range(nh):
            if h + 1 < nh:
                scores(h + 1)
            s = s_sc[h % 2]
            if diagonal:
                row = lax.broadcasted_iota(jnp.int32, s.shape, 0)
                col = lax.broadcasted_iota(jnp.int32, s.shape, 1)
                s = jnp.where(col <= row, s, NEG)
            m_old = m_sc[h]
            m_new = jnp.maximum(m_old, jnp.max(s, axis=1, keepdims=True))
            alpha = jnp.exp2(m_old - m_new)
            p_sc[h % 2] = jnp.exp2(s - jnp.concatenate([m_new] * reps, axis=1)).astype(BF16)
            pv = _dot(p_sc[h % 2], va_ref[0, :, h * LANES:(h + 1) * LANES])
            acc_sc[h] = alpha * acc_sc[h] + pv
            m_sc[h] = m_new

    @pl.when(j < i)
    def _():
        step(False)

    @pl.when(j == i)
    def _():
        step(True)
        lane = lax.broadcasted_iota(jnp.int32, (tq, LANES), 1)
        half = LANES // 2
        for pair in range(nh // 2):
            a0 = acc_sc[2 * pair]
            a1 = acc_sc[2 * pair + 1]
            o0 = a0 / pltpu.roll(a0, half, axis=1)
            o1 = a1 / pltpu.roll(a1, half, axis=1)
            o_ref[0, :, pair * LANES:(pair + 1) * LANES] = jnp.where(
                lane < hd, o0, pltpu.roll(o1, half, axis=1)).astype(BF16)


def _attn_prompt(qp, kp, va, nh, hd, tq):
    g, l, _ = qp.shape
    cc = nh * hd
    assert 2 * hd == LANES, "value slots are [v | ones] halves of one 128-lane slot"
    nq = l // tq
    pairs =[(i, j) for i in range(nq) for j in range(i + 1)]
    it = jnp.asarray([p[0] for p in pairs], jnp.int32)
    jt = jnp.asarray([p[1] for p in pairs], jnp.int32)
    qrow = lambda b, p, it, jt: (b, it[p], 0)
    krow = lambda b, p, it, jt: (b, jt[p], 0)
    grid_spec = pltpu.PrefetchScalarGridSpec(
        num_scalar_prefetch=2, grid=(g, len(pairs)),
        in_specs=[pl.BlockSpec((1, tq, nh * LANES), qrow), pl.BlockSpec((1, tq, nh * LANES), krow),
                  pl.BlockSpec((1, tq, nh * LANES), krow)],
        out_specs=pl.BlockSpec((1, tq, cc), qrow),
        scratch_shapes=[pltpu.VMEM((nh, tq, LANES), F32), pltpu.VMEM((nh, tq, LANES), F32),
                        pltpu.VMEM((2, tq, tq), F32), pltpu.VMEM((2, tq, tq), BF16)])
    return pl.pallas_call(
        functools.partial(_attn_kernel, nh=nh, hd=hd, tq=tq),
        grid_spec=grid_spec,
        out_shape=jax.ShapeDtypeStruct((g, l, cc), BF16),
        compiler_params=_cparams("arbitrary", "arbitrary"),
        name="attn_prompt",
    )(it, jt, qp, kp, va)


def _attn_dec_kernel(pt_ref, qcol_ref, kncol_ref, vncol_ref, lfnew_ref, uincl_ref, ones_ref, *rest, npg, nh):
    k_refs = rest[:npg]
    v_refs = rest[npg:2 * npg]
    lf_refs = rest[2 * npg:3 * npg]
    o_ref = rest[3 * npg]
    m_sc, l_sc, acc_sc, c_sc = rest[3 * npg + 1:]
    g = pl.program_id(1)

    @pl.when(g == 0)
    def _():
        m_sc[...] = jnp.full(m_sc.shape, NEG, F32)
        l_sc[...] = jnp.zeros_like(l_sc)
        acc_sc[...] = jnp.zeros_like(acc_sc)
        c_sc[...] = jnp.zeros_like(c_sc)

    parts = []
    for j in range(npg):
        hi, mid, lo = _split3(lf_refs[j][0, 0])
        parts += [hi.astype(F32), mid.astype(F32), lo.astype(F32), jnp.zeros((nh, LANES), F32)]
    parts = jnp.concatenate(parts, axis=0).astype(BF16)
    cum = _dot(parts, uincl_ref[...])
    tot = _dot(parts, ones_ref[...])
    fs = []
    c = c_sc[...]
    for j in range(npg):
        r = 4 * nh * j
        fs.append(c + (cum[r:r + nh] + cum[r + nh:r + 2 * nh] + cum[r + 2 * nh:r + 3 * nh]))
        c = c + (tot[r:r + nh] + tot[r + nh:r + 2 * nh] + tot[r + 2 * nh:r + 3 * nh])
    c_sc[...] = c

    for h in range(nh):
        qc = qcol_ref[0, h]
        rows = [jnp.sum(qc * k_refs[j][0, 0, h], axis=0, keepdims=True) - fs[j][h:h + 1, :] for j in range(npg)]
        m_old = m_sc[h]
        m_new = jnp.maximum(m_old, jnp.max(functools.reduce(jnp.maximum, rows), axis=1, keepdims=True))
        alpha = jnp.exp(m_old - m_new)
        ps = [jnp.exp(r - m_new) for r in rows]
        l_sc[h] = alpha * l_sc[h] + jnp.sum(functools.reduce(jnp.add, ps), axis=1, keepdims=True)
        acc = alpha * acc_sc[h]
        for j in range(npg):
            acc = acc + ps[j] * v_refs[j][0, 0, h]
        acc_sc[h] = acc
        m_sc[h] = m_new

    @pl.when(g == pl.num_programs(1) - 1)
    def _():
        lane = lax.broadcasted_iota(jnp.int32, (1, LANES), 1)
        for h in range(nh):
            f_new = c_sc[h:h + 1, :] + lfnew_ref[0, h:h + 1, :]
            s_new = jnp.sum(qcol_ref[0, h] * kncol_ref[0, h], axis=0, keepdims=True) - f_new
            s_new = jnp.where(lane == 0, s_new, NEG)
            m_old = m_sc[h]
            m_new = jnp.maximum(m_old, jnp.max(s_new, axis=1, keepdims=True))
            alpha = jnp.exp(m_old - m_new)
            p_new = jnp.exp(s_new - m_new)
            l_fin = alpha * l_sc[h] + jnp.sum(p_new, axis=1, keepdims=True)
            acc = (alpha * acc_sc[h] + p_new * vncol_ref[0, h]) / l_fin
            hi, mid, lo = _split3(acc)
            ones = ones_ref[0:ONES_ROWS, :]
            o = _dot_nt(ones, hi) + _dot_nt(ones, mid) + _dot_nt(ones, lo)
            o_ref[0, h:h + 1, :] = o[0:1, :]


def _attn_decode(page_table, q, knew, vnew, lfnew, cache_kt, cache_vt, cache_lft, e, nh, hd):
    n_seq, n_pages = page_table.shape
    page = cache_kt.shape[-1]
    assert page == LANES and n_pages % PAGES_PER_STEP == 0
    npg = PAGES_PER_STEP
    col = lambda a: jnp.broadcast_to(a.reshape(n_seq, nh, hd, 1), (n_seq, nh, hd, LANES))
    uincl = jnp.asarray(np.triu(np.ones((page, page))), BF16)
    ones = jnp.ones((LANES, LANES), BF16)

    def pg(j, nd):
        return lambda b, g, pt: (pt[b, g * npg + j], e) + (0,) * nd

    per_seq4 = lambda b, g, pt: (b, 0, 0, 0)
    per_seq3 = lambda b, g, pt: (b, 0, 0)
    const2 = lambda b, g, pt: (0, 0)
    in_specs = [pl.BlockSpec((1, nh, hd, LANES), per_seq4), pl.BlockSpec((1, nh, hd, LANES), per_seq4),
                pl.BlockSpec((1, nh, hd, LANES), per_seq4), pl.BlockSpec((1, nh, 1), per_seq3),
                pl.BlockSpec((page, page), const2), pl.BlockSpec((LANES, LANES), const2)]
    in_specs += [pl.BlockSpec((1, 1, nh, hd, page), pg(j, 3)) for j in range(npg)]
    in_specs += [pl.BlockSpec((1, 1, nh, hd, page), pg(j, 3)) for j in range(npg)]
    in_specs += [pl.BlockSpec((1, 1, nh, page), pg(j, 2)) for j in range(npg)]
    grid_spec = pltpu.PrefetchScalarGridSpec(
        num_scalar_prefetch=1, grid=(n_seq, n_pages // npg), in_specs=in_specs,
        out_specs=pl.BlockSpec((1, nh, hd), per_seq3),
        scratch_shapes=[pltpu.VMEM((nh, 1, LANES), F32), pltpu.VMEM((nh, 1, LANES), F32),
                        pltpu.VMEM((nh, hd, LANES), F32), pltpu.VMEM((nh, LANES), F32)])
    o = pl.pallas_call(
        functools.partial(_attn_dec_kernel, npg=npg, nh=nh),
        grid_spec=grid_spec,
        out_shape=jax.ShapeDtypeStruct((n_seq, nh, hd), F32),
        compiler_params=_cparams("arbitrary", "arbitrary"),
        name="attn_decode",
    )(page_table, col(q), col(knew), col(vnew), lfnew.reshape(n_seq, nh, 1), uincl, ones,
      *([cache_kt] * npg), *([cache_vt] * npg), *([cache_lft] * npg))
    return o.reshape(n_seq, 1, nh * hd)


def _odd_in_kernel(x_ref, g_ref, sh_ref, sc_ref, w_ref, wgd_ref, wgk_ref, bgk_ref,
                   q_ref, k_ref, v_ref, sr_ref, gl_ref, *, qk, vv, dk):
    hb = _norm_mod(x_ref[0], g_ref[...], sh_ref[0], sc_ref[0]).astype(BF16)
    q_ref[0] = _dot(hb, w_ref[:, 0:qk]) * (dk ** -0.5)
    k_ref[0] = _dot(hb, w_ref[:, qk:2 * qk])
    gd = _dot(hb, wgd_ref[...]).astype(BF16)
    gl_ref[0] = _log_sigmoid(_dot(gd, wgk_ref[...]) + bgk_ref[...]) * (1.0 / GLA_TAU)
    v_ref[0] = _dot(hb, w_ref[:, 2 * qk:2 * qk + vv]).astype(BF16)
    sr_ref[0] = _silu(_dot(hb, w_ref[:, 2 * qk + vv:2 * qk + 2 * vv]))


def _gla_seq_kernel(q_ref, k_ref, g_ref, v_ref, sr_ref, go_ref, tri_ref, o_ref, st_ref, s_sc, *, nh, dk, dv, tt):
    @pl.when(pl.program_id(1) == 0)
    def _():
        s_sc[...] = jnp.zeros_like(s_sc)

    o_ref[0] = _gla_tile(q_ref[0], k_ref[0], g_ref[0], v_ref[0], sr_ref[0], go_ref, tri_ref, s_sc, nh, dk, dv, tt)
    st_ref[0] = s_sc[...]


def _odd_weights(w_in, w_gk, b_gk, qk, vv):
    n, d, _ = w_in.shape
    rank = w_gk.shape[1]
    w_main = w_in.astype(BF16)
    w_gd = jnp.zeros((n, d, LANES), BF16).at[:, :, :rank].set(w_in[:, :, 2 * qk + 2 * vv:].astype(BF16))
    wgk = jnp.zeros((n, LANES, qk), BF16).at[:, :rank].set(w_gk.astype(BF16))
    return w_main, w_gd, wgk, b_gk.reshape(n, 1, qk)


def _odd_in(x, gain, mod, wts, li, qk, vv, dk, tm):
    g, l, d = x.shape
    oi = li // 2
    w_main, w_gd, wgk, bgk = wts
    row = lambda b, i: (b, i, 0)
    in_specs = [pl.BlockSpec((1, tm, d), row), _layer(gain, li), _mod_spec(mod, li, 0, d, 2), _mod_spec(mod, li, 1, d, 2),
                _layer(w_main, oi, 2 * qk + 2 * vv), _layer(w_gd, oi), _layer(wgk, oi), _layer(bgk, oi)]
    out_shape = [jax.ShapeDtypeStruct((g, l, qk), F32), jax.ShapeDtypeStruct((g, l, qk), F32),
                 jax.ShapeDtypeStruct((g, l, vv), BF16), jax.ShapeDtypeStruct((g, l, vv), F32),
                 jax.ShapeDtypeStruct((g, l, qk), F32)]
    out_specs = [pl.BlockSpec((1, tm, qk), row), pl.BlockSpec((1, tm, qk), row), pl.BlockSpec((1, tm, vv), row),
                 pl.BlockSpec((1, tm, vv), row), pl.BlockSpec((1, tm, qk), row)]
    return pl.pallas_call(
        functools.partial(_odd_in_kernel, qk=qk, vv=vv, dk=dk),
        grid=(g, l // tm), in_specs=in_specs, out_specs=out_specs, out_shape=out_shape,
        compiler_params=_cparams("arbitrary", "arbitrary"),
        name="odd_in",
    )(x, gain, mod, mod, w_main, w_gd, wgk, bgk)


def _gla_tile(q, k, gl, v, sr, go_ref, tri_ref, s_sc, nh, dk, dv, tt):
    c = GLA_CHUNK
    qk = nh * dk
    b_all = _sum3(_dot(tri_ref[...], jnp.concatenate(_split3(gl), axis=1)), qk)
    nchunk = tt // c
    row = lax.broadcasted_iota(jnp.int32, (tt, tt), 0)
    col = lax.broadcasted_iota(jnp.int32, (tt, tt), 1)
    blockmask = (col <= row) & (col >= (row // c) * c)
    b_last = [b_all[(n + 1) * c - 1:(n + 1) * c, :] for n in range(nchunk)]
    b_last_rows = jnp.concatenate([jnp.broadcast_to(bl, (c, qk)) for bl in b_last], axis=0)
    q_e = (q * jnp.exp(b_all)).astype(BF16)
    k_e = (k * jnp.exp(-b_all)).astype(BF16)
    k_d = (k * jnp.exp(b_last_rows - b_all)).astype(BF16)
    decay = [jnp.exp(bl) for bl in b_last]
    out = []
    for h in range(nh):
        kc = slice(h * dk, (h + 1) * dk)
        vc = slice(h * dv, (h + 1) * dv)
        v_h = v[:, vc]
        a = jnp.where(blockmask, _dot_nt(q_e[:, kc], k_e[:, kc]), 0.0)
        o_intra = _dot(a.astype(BF16), v_h)
        s_t = s_sc[h]
        o_inter = []
        for n in range(nchunk):
            rows = slice(n * c, (n + 1) * c)
            o_inter.append(_dot_nt(q_e[rows, kc], s_t.astype(BF16)))
            s_t = s_t * decay[n][:, kc] + _dot_tn(v_h[rows], k_d[rows, kc])
        s_sc[h] = s_t
        o = o_intra + jnp.concatenate(o_inter, axis=0)
        y = o * lax.rsqrt(jnp.mean(o * o, axis=-1, keepdims=True) + EPS) * go_ref[...]
        out.append((y * sr[:, vc]).astype(BF16))
    return jnp.concatenate(out, axis=1)


def _gla_seq(q, k, gl, v, sr, g_o, oi, nh, dk, dv, tt):
    g, l, qk = q.shape
    vv = nh * dv
    tri = jnp.asarray(np.kron(np.eye(tt // GLA_CHUNK), np.tril(np.ones((GLA_CHUNK, GLA_CHUNK)))), BF16)
    row = lambda b, i: (b, i, 0)
    return pl.pallas_call(
        functools.partial(_gla_seq_kernel, nh=nh, dk=dk, dv=dv, tt=tt),
        grid=(g, l // tt),
        in_specs=[pl.BlockSpec((1, tt, qk), row), pl.BlockSpec((1, tt, qk), row), pl.BlockSpec((1, tt, qk), row),
                  pl.BlockSpec((1, tt, vv), row), pl.BlockSpec((1, tt, vv), row), _layer(g_o, oi), _full((tt, tt))],
        out_specs=[pl.BlockSpec((1, tt, vv), row), pl.BlockSpec((1, nh, dv, dk), lambda b, i: (b, 0, 0, 0))],
        out_shape=[jax.ShapeDtypeStruct((g, l, vv), BF16), jax.ShapeDtypeStruct((g, nh, dv, dk), F32)],
        scratch_shapes=[pltpu.VMEM((nh, dv, dk), F32)],
        compiler_params=_cparams("arbitrary", "arbitrary"),
        name="gla_seq",
    )(q, k, gl, v, sr, g_o, tri)


def _gla_dec_kernel(q_ref, k_ref, g_ref, v_ref, sr_ref, go_ref, s0_ref, *rest, nh, dk, dv, aliased):
    o_ref, s_ref = rest[1:] if aliased else rest

    def column(r):
        return jnp.broadcast_to(r, (dk, dk)).T

    for h in range(nh):
        kc = slice(h * dk, (h + 1) * dk)
        dcol = jnp.exp(column(g_ref[0, :, kc]))
        kcol = column(k_ref[0, :, kc])
        qcol = column(q_ref[0, :, kc])
        halves = []
        for half in range(dv // dk):
            lanes = slice(half * dk, (half + 1) * dk)
            v = v_ref[0, :, h * dv + half * dk:h * dv + (half + 1) * dk].astype(F32)
            s_new = dcol * s0_ref[0, 0, h, :, lanes] + kcol * v
            s_ref[0, h, :, lanes] = s_new
            halves.append(jnp.sum(qcol * s_new, axis=0, keepdims=True))
        o = jnp.concatenate(halves, axis=1)
        y = o * lax.rsqrt(jnp.mean(o * o, axis=-1, keepdims=True) + EPS) * go_ref[...]
        vc = slice(h * dv, (h + 1) * dv)
        o_ref[0, :, vc] = y * sr_ref[0, :, vc]


def _gla_dec(q, k, gl, v, sr, g_o, state_gla, o_idx, s_prev, nh, dk, dv):
    n_seq = q.shape[0]
    qk, vv = nh * dk, nh * dv
    r3 = lambda a: a.reshape(n_seq, 1, a.shape[-1])
    per_seq = lambda b: (b, 0, 0)
    in_specs = [pl.BlockSpec((1, 1, qk), per_seq), pl.BlockSpec((1, 1, qk), per_seq),
                pl.BlockSpec((1, 1, qk), per_seq), pl.BlockSpec((1, 1, vv), per_seq),
                pl.BlockSpec((1, 1, vv), per_seq), pl.BlockSpec((1, dv), lambda b: (0, 0)),
                pl.BlockSpec((1, 1, nh, dk, dv), lambda b: (b, o_idx, 0, 0, 0))]
    args = [r3(q), r3(k), r3(gl), r3(v), r3(sr), g_o.reshape(1, dv), state_gla]
    aliases = {}
    if s_prev is not None:
        in_specs.append(pl.BlockSpec(memory_space=pl.ANY))
        args.append(s_prev)
        aliases = {len(args) - 1: 1}
    return pl.pallas_call(
        functools.partial(_gla_dec_kernel, nh=nh, dk=dk, dv=dv, aliased=s_prev is not None),
        grid=(n_seq,),
        in_specs=in_specs,
        out_specs=[pl.BlockSpec((1, 1, vv), per_seq),
                   pl.BlockSpec((1, None, nh, dk, dv), lambda b: (b, o_idx, 0, 0, 0))],
        out_shape=[jax.ShapeDtypeStruct((n_seq, 1, vv), F32), jax.ShapeDtypeStruct(state_gla.shape, F32)],
        input_output_aliases=aliases,
        compiler_params=_cparams("arbitrary"),
        name="gla_dec",
    )(*args)


def _post_kernel(x_ref, a1_ref, a2_ref, wo_ref, gtm_ref, gf_ref, shf_ref, scf_ref, gtf_ref, w1_ref, w2_ref,
                 o_ref, h_sc, *, ka):
    f = pl.program_id(2)

    @pl.when(f == 0)
    def _():
        m = _dot(a1_ref[0], wo_ref[0:ka, :]) + _dot(a2_ref[0], wo_ref[ka:2 * ka, :])
        x1 = x_ref[0] + gtm_ref[0] * m
        o_ref[0] = x1
        h_sc[...] = _norm_mod(x1, gf_ref[...], shf_ref[0], scf_ref[0]).astype(BF16)

    t = jnp.maximum(_dot(h_sc[...], w1_ref[...]), 0.0)
    o_ref[0] += gtf_ref[0] * _dot((t * t).astype(BF16), w2_ref[...])


def _post(x, a1, a1_col, a2, a2_col, w_out, gain_f, mod, w1, w2, li, tm, tf):
    g, l, d = x.shape
    ka = w_out.shape[1] // 2
    dff = w1.shape[2]
    row = lambda b, i, f: (b, i, 0)
    modc = lambda c: _mod_spec(mod, li, c, d, 3)
    return pl.pallas_call(
        functools.partial(_post_kernel, ka=ka),
        grid=(g, l // tm, dff // tf),
        in_specs=[pl.BlockSpec((1, tm, d), row),
                  pl.BlockSpec((1, tm, ka), lambda b, i, f: (b, i, a1_col)),
                  pl.BlockSpec((1, tm, ka), lambda b, i, f: (b, i, a2_col)),
                  _layer(w_out, li // 2), modc(2), _layer(gain_f, li), modc(3), modc(4), modc(5),
                  pl.BlockSpec((None, d, tf), lambda b, i, f: (li, 0, f)),
                  pl.BlockSpec((None, tf, d), lambda b, i, f: (li, f, 0))],
        out_specs=pl.BlockSpec((1, tm, d), row),
        out_shape=jax.ShapeDtypeStruct((g, l, d), F32),
        scratch_shapes=[pltpu.VMEM((tm, d), BF16)],
        compiler_params=_cparams("arbitrary", "arbitrary", "arbitrary"),
        name="post_mlp",
    )(x, a1, a2, w_out, mod, gain_f, mod, mod, mod, w1, w2)


def kernel(x_prompt, x_sample, c_prompt, c_sample, cache_k, cache_v, cache_logf, page_table, state_conv, state_gla,
           g_mix, g_ffn, w_ada, b_ada, w_in_even, b_f, g_q, g_k, w_conv, w_out_even, w_in_odd, w_gk, b_gk, g_o,
           w_out_odd, w_ff1, w_ff2):
    bp, seq, d = x_prompt.shape
    n_seq = x_sample.shape[0]
    depth = w_ada.shape[0]
    n_even, nh_fox = b_f.shape
    hd_fox = g_q.shape[1]
    cc = nh_fox * hd_fox
    _, n_odd, nh_gla, dk, dv = state_gla.shape
    qk, vv = nh_gla * dk, nh_gla * dv

    mod_all = _adaln(jnp.concatenate([c_prompt, c_sample], axis=0), w_ada, b_ada)

    cache_kt = jnp.transpose(cache_k, (0, 1, 3, 4, 2))
    cache_vt = jnp.transpose(cache_v, (0, 1, 3, 4, 2))
    cache_lft = jnp.swapaxes(cache_logf, 2, 3)

    mod_p = mod_all[:, :bp].reshape(depth, bp, 1, 6 * d)
    mod_s = mod_all[:, bp:].reshape(depth, 1, n_seq, 6 * d)
    gm = g_mix.reshape(depth, 1, d)
    gf = g_ffn.reshape(depth, 1, d)
    w1 = w_ff1.astype(BF16)
    w2 = w_ff2.astype(BF16)
    wts_even = _even_weights(w_in_even, b_f, g_q, g_k, nh_fox, hd_fox)
    wts_odd = _odd_weights(w_in_odd, w_gk, b_gk, qk, vv)
    wo_even = w_out_even.astype(BF16)
    wo_odd = w_out_odd.astype(BF16)
    go = g_o.reshape(n_odd, 1, dv)

    xp = x_prompt
    xs = x_sample.reshape(1, n_seq, d)
    kv_rows = None
    gla_s = None
    lf_rows_p, fox_s, conv_p, conv_s, gla_p = [], [], [], [], []
    for li in range(depth):
        if li % 2 == 0:
            e = li // 2
            ya, utail, qp, kp, va, kt_rows, vt_rows, lf_rows = _even_in_seq(
                xp, gm, mod_p, wts_even, w_conv, li, kv_rows, nh_fox, hd_fox, tm=TM_EVEN)
            kv_rows = (kt_rows, vt_rows)
            o = _attn_prompt(qp, kp, va, nh_fox, hd_fox, tq=TQ_ATTN)
            xp = _post(xp, ya, 0, o, 0, wo_even, gf, mod_p, w1, w2, li, tm=TM_POST, tf=TF_POST)
            lf_rows_p.append(lf_rows)
            conv_p.append(utail[:, 6:8])
            prev0 = state_conv[:, e, 0].reshape(1, n_seq, cc)
            prev1 = state_conv[:, e, 1].reshape(1, n_seq, cc)
            ya, u, qn, kn, v, lfpad = _even_in_dec(xs, gm, mod_s, wts_even, w_conv, li, prev0, prev1, nh_fox, hd_fox)
            lf_new = lfpad[0, :, :nh_fox]
            o = _attn_decode(page_table, qn[0], kn[0], v[0], lf_new, cache_kt, cache_vt, cache_lft, e,
                             nh_fox, hd_fox)
            o = o.reshape(1, n_seq, cc).astype(BF16)
            xs = _post(xs, ya, 0, o, 0, wo_even, gf, mod_s, w1, w2, li, tm=n_seq, tf=TF_POST)
            fox_s.append((kn[0], v[0], lf_new))
            conv_s.append(jnp.stack([prev1[0], u[0]], axis=1))
        else:
            oi = li // 2
            q, k, v, sr, gl = _odd_in(xp, gm, mod_p, wts_odd, li, qk, vv, dk, tm=TM_ODD)
            og, s_t = _gla_seq(q, k, gl, v, sr, go, oi, nh_gla, dk, dv, tt=TT_GLA)
            xp = _post(xp, og, 0, og, 1, wo_odd, gf, mod_p, w1, w2, li, tm=TM_POST, tf=TF_POST)
            gla_p.append(jnp.swapaxes(s_t, 2, 3))
            q, k, v, sr, gl = _odd_in(xs, gm, mod_s, wts_odd, li, qk, vv, dk, tm=n_seq)
            og, gla_s = _gla_dec(q[0], k[0], gl[0], v[0].astype(F32), sr[0], g_o[oi], state_gla, oi, gla_s,
                                 nh_gla, dk, dv)
            og = og.reshape(1, n_seq, vv).astype(BF16)
            xs = _post(xs, og, 0, og, 1, wo_odd, gf, mod_s, w1, w2, li, tm=n_seq, tf=TF_POST)

    heads = lambda a, n: a.reshape(a.shape[0], n, nh_fox, hd_fox)
    k_p = jnp.transpose(kv_rows[0], (0, 1, 4, 2, 3))
    v_p = jnp.transpose(kv_rows[1], (0, 1, 4, 2, 3))
    lf_p = jnp.stack(lf_rows_p, axis=1)
    k_s = jnp.stack([heads(r[0], 1) for r in fox_s], axis=1)
    v_s = jnp.stack([heads(r[1], 1) for r in fox_s], axis=1)
    lf_s = jnp.stack([r[2].reshape(n_seq, 1, nh_fox) for r in fox_s], axis=1)
    return (xp, xs.reshape(n_seq, 1, d), k_p, v_p, lf_p, k_s, v_s, lf_s,
            jnp.stack(conv_p, axis=1), jnp.stack(conv_s, axis=1),
            jnp.stack(gla_p, axis=1), gla_s)
```

```python
import functools

import numpy as np
import jax
import jax.numpy as jnp
from jax import lax
from jax.experimental import pallas as pl
from jax.experimental.pallas import tpu as pltpu

F32 = jnp.float32
BF16 = jnp.bfloat16

EPS = 1e-6
GLA_TAU = 16.0
GLA_CHUNK = 64
LANES = 128
VMEM_LIMIT = 56 * 1024 * 1024
NEG = -1e30
LOG2E = 1.4426950408889634
PAGES_PER_STEP = 32
SEQS_PER_GLA_STEP = 4
ONES_ROWS = 16
TM_EVEN = 512
TM_ODD = 512
TM_POST = 1024
TF_POST = 1024
TQ_ATTN = 512
TT_GLA = 256


def _cparams(*sem):
    return pltpu.CompilerParams(dimension_semantics=sem, vmem_limit_bytes=VMEM_LIMIT)


def _silu(x):
    return x / (1.0 + jnp.exp(-x))


def _log_sigmoid(z):
    return jnp.minimum(z, 0.0) - jnp.log1p(jnp.exp(-jnp.abs(z)))


def _norm_mod(x, gain, shift, scale):
    y = x * lax.rsqrt(jnp.mean(x * x, axis=-1, keepdims=True) + EPS)
    return (y * gain) * (1.0 + scale) + shift


def _split3(a):
    hi = a.astype(BF16)
    r = a - hi.astype(F32)
    mid = r.astype(BF16)
    lo = (r - mid.astype(F32)).astype(BF16)
    return hi, mid, lo


def _dot(a, b):
    return jnp.dot(a, b, preferred_element_type=F32)


def _dot_nt(a, b):
    return lax.dot_general(a, b, (((1,), (1,)), ((), ())), preferred_element_type=F32)


def _dot_tn(a, b):
    return lax.dot_general(a, b, (((0,), (0,)), ((), ())), preferred_element_type=F32)


def _sum3(c, w):
    return c[:, 0:w] + c[:, w:2 * w] + c[:, 2 * w:3 * w]


def _adaln_kernel(c_ref, w_ref, b_ref, o_ref):
    a = _silu(c_ref[...]).astype(BF16)
    o_ref[0] = _dot(a, w_ref[0].astype(BF16)) + b_ref[0]


def _adaln(c_all, w_ada, b_ada):
    depth, d, n = w_ada.shape
    r = c_all.shape[0]
    tn = n // 4
    return pl.pallas_call(
        _adaln_kernel,
        grid=(depth, n // tn),
        in_specs=[pl.BlockSpec((r, d), lambda l, j: (0, 0)),
                  pl.BlockSpec((1, d, tn), lambda l, j: (l, 0, j)),
                  pl.BlockSpec((1, 1, tn), lambda l, j: (l, 0, j))],
        out_specs=pl.BlockSpec((1, r, tn), lambda l, j: (l, 0, j)),
        out_shape=jax.ShapeDtypeStruct((depth, r, n), F32),
        compiler_params=_cparams("arbitrary", "arbitrary"),
        name="adaln",
    )(c_all, w_ada, b_ada.reshape(depth, 1, n))


def _even_project(x_ref, g_ref, sh_ref, sc_ref, w_ref, wf_ref, bf_ref, gq_ref, gk_ref, p_ref, cc, nh):
    hb = _norm_mod(x_ref[0], g_ref[...], sh_ref[0], sc_ref[0]).astype(BF16)
    gb = _dot(hb, w_ref[:, 0:cc])
    u = _dot(hb, w_ref[:, cc:2 * cc]) * _dot(hb, w_ref[:, 2 * cc:3 * cc])
    q = _dot(hb, w_ref[:, 3 * cc:4 * cc])
    k = _dot(hb, w_ref[:, 4 * cc:5 * cc])
    v = _dot(hb, w_ref[:, 5 * cc:6 * cc])
    hd = cc // nh
    qms = _dot((q * q).astype(BF16), p_ref[...]) * (1.0 / hd)
    kms = _dot((k * k).astype(BF16), p_ref[...]) * (1.0 / hd)
    qn = (q * lax.rsqrt(qms + EPS)) * gq_ref[...]
    kn = (k * lax.rsqrt(kms + EPS)) * gk_ref[...]
    fz = _dot(hb, wf_ref[...]) + bf_ref[...]
    lane = lax.broadcasted_iota(jnp.int32, fz.shape, 1)
    lf = jnp.where(lane < nh, _log_sigmoid(fz), 0.0)
    return gb, u, qn, kn, v, lf


def _even_seq_kernel(x_ref, g_ref, sh_ref, sc_ref, w_ref, wf_ref, bf_ref, gq_ref, gk_ref, wc_ref, p_ref,
                     tri_ref, eq_ref, ek_ref, oq_ref, ok_ref, *rest, nh, cc, tm, aliased):
    if aliased:
        rest = rest[2:]
    ya_ref, ut_ref, qp_ref, kp_ref, va_ref, ko_ref, vo_ref, lf_ref, ubuf, fcarry = rest
    i = pl.program_id(1)
    gb, u, qn, kn, v, lf = _even_project(x_ref, g_ref, sh_ref, sc_ref, w_ref, wf_ref, bf_ref,
                                         gq_ref, gk_ref, p_ref, cc, nh)

    @pl.when(i == 0)
    def _():
        ubuf[0:8, :] = jnp.zeros((8, cc), F32)
        fcarry[...] = jnp.zeros_like(fcarry)

    ubuf[8:8 + tm, :] = u
    conv = wc_ref[0:1, :] * ubuf[6:6 + tm, :] + wc_ref[1:2, :] * ubuf[7:7 + tm, :] + wc_ref[2:3, :] * u
    ya_ref[0] = (gb * conv).astype(BF16)
    tail = ubuf[tm:tm + 8, :]
    ubuf[0:8, :] = tail
    ut_ref[0] = tail

    ko_ref[0, 0] = kn.T.reshape(nh, cc // nh, tm)
    vo_ref[0, 0] = v.T.reshape(nh, cc // nh, tm)
    lf_ref[0] = lf[:, 0:nh]
    vb = v.astype(BF16)
    va_ref[0] = jnp.ones((tm, nh * LANES), BF16)
    for h in range(nh):
        va_ref[0, :, h * LANES:h * LANES + cc // nh] = vb[:, h * (cc // nh):(h + 1) * (cc // nh)]

    c3 = _dot(tri_ref[...], jnp.concatenate(_split3(lf), axis=1))
    f = _sum3(c3, LANES) + fcarry[...]
    fcarry[...] = f[tm - 1:tm, :]
    f3 = jnp.concatenate(_split3(f * LOG2E), axis=1)

    qp_ref[0] = (_dot(f3, eq_ref[...]) + oq_ref[...]).astype(BF16)
    kp_ref[0] = (_dot(f3, ek_ref[...]) + ok_ref[...]).astype(BF16)
    qb = qn.astype(BF16)
    kb = kn.astype(BF16)
    hd = cc // nh
    for h in range(nh):
        qp_ref[0, :, h * LANES:h * LANES + hd] = qb[:, h * hd:(h + 1) * hd]
        kp_ref[0, :, h * LANES:h * LANES + hd] = kb[:, h * hd:(h + 1) * hd]


def _even_dec_kernel(x_ref, g_ref, sh_ref, sc_ref, w_ref, wf_ref, bf_ref, gq_ref, gk_ref, wc_ref, p_ref,
                     p0_ref, p1_ref,
                     ya_ref, u_ref, q_ref, k_ref, v_ref, lf_ref, *, nh, cc):
    gb, u, qn, kn, v, lf = _even_project(x_ref, g_ref, sh_ref, sc_ref, w_ref, wf_ref, bf_ref,
                                         gq_ref, gk_ref, p_ref, cc, nh)
    conv = wc_ref[0:1, :] * p0_ref[0] + wc_ref[1:2, :] * p1_ref[0] + wc_ref[2:3, :] * u
    ya_ref[0] = (gb * conv).astype(BF16)
    u_ref[0] = u
    q_ref[0] = qn
    k_ref[0] = kn
    v_ref[0] = v
    lf_ref[0] = lf


def _even_consts(nh, hd, tm):
    cc = nh * hd
    p = np.kron(np.eye(nh), np.ones((hd, hd)))
    tri = np.tril(np.ones((tm, tm)))
    eq = np.zeros((3 * LANES, nh * LANES))
    ek = np.zeros((3 * LANES, nh * LANES))
    oq = np.zeros((1, nh * LANES))
    ok = np.zeros((1, nh * LANES))
    for h in range(nh):
        for part in range(3):
            eq[part * LANES + h, h * LANES + hd + part] = 1.0
            ek[part * LANES + h, h * LANES + hd + 3 + part] = -1.0
            oq[0, h * LANES + hd + 3 + part] = 1.0
            ok[0, h * LANES + hd + part] = 1.0
    bf = lambda a: jnp.asarray(a, BF16)
    return bf(p), bf(tri), bf(eq), bf(ek), jnp.asarray(oq, F32), jnp.asarray(ok, F32)


def _even_weights(w_in, b_f, g_q, g_k, nh, hd):
    n, d, _ = w_in.shape
    cc = nh * hd
    w_main = w_in.astype(BF16)
    w_f = jnp.zeros((n, d, LANES), BF16).at[:, :, :nh].set(w_in[:, :, 6 * cc:].astype(BF16))
    bf = jnp.zeros((n, 1, LANES), F32).at[:, 0, :nh].set(b_f)
    gq = (jnp.tile(g_q, (1, nh)) * (hd ** -0.5)).reshape(n, 1, cc)
    gk = jnp.tile(g_k, (1, nh)).reshape(n, 1, cc)
    return w_main, w_f, bf, gq, gk


def _full(shape):
    nd = len(shape)
    return pl.BlockSpec(shape, lambda *_: (0,) * nd)


def _layer(a, idx, width=None):
    nd = a.ndim
    shape = a.shape[1:] if width is None else a.shape[1:-1] + (width,)
    return pl.BlockSpec((None,) + shape, lambda *_: (idx,) + (0,) * (nd - 1))


def _mod_spec(mod, li, col, d, nargs):
    r = mod.shape[2]
    if nargs == 1:
        return pl.BlockSpec((None, 1, r, d), lambda i: (li, 0, 0, col))
    if nargs == 2:
        return pl.BlockSpec((None, 1, r, d), lambda b, i: (li, b, 0, col))
    return pl.BlockSpec((None, 1, r, d), lambda b, i, f: (li, b, 0, col))


def _even_in_seq(x, gain, mod, wts, w_conv, li, kv_prev, nh, hd, tm):
    g, l, d = x.shape
    cc = nh * hd
    e = li // 2
    n_even = w_conv.shape[0]
    w_main, w_f, bf, gq, gk = wts
    gq = gq * LOG2E
    p, tri, eq, ek, oq, ok = _even_consts(nh, hd, tm)
    row = lambda b, i: (b, i, 0)
    in_specs = [pl.BlockSpec((1, tm, d), row), _layer(gain, li), _mod_spec(mod, li, 0, d, 2), _mod_spec(mod, li, 1, d, 2),
                _layer(w_main, e, 6 * cc), _layer(w_f, e), _layer(bf, e), _layer(gq, e), _layer(gk, e), _layer(w_conv, e),
                _full(p.shape), _full(tri.shape), _full(eq.shape), _full(ek.shape), _full(oq.shape), _full(ok.shape)]
    args = [x, gain, mod, mod, w_main, w_f, bf, gq, gk, w_conv, p, tri, eq, ek, oq, ok]
    kt_shape = jax.ShapeDtypeStruct((g, n_even, nh, hd, l), F32)
    kt_spec = pl.BlockSpec((1, 1, nh, hd, tm), lambda b, i: (b, e, 0, 0, i))
    out_shape = [jax.ShapeDtypeStruct((g, l, cc), BF16),
                 jax.ShapeDtypeStruct((g, 8, cc), F32),
                 jax.ShapeDtypeStruct((g, l, nh * LANES), BF16),
                 jax.ShapeDtypeStruct((g, l, nh * LANES), BF16),
                 jax.ShapeDtypeStruct((g, l, nh * LANES), BF16),
                 kt_shape, kt_shape,
                 jax.ShapeDtypeStruct((g, l, nh), F32)]
    out_specs = [pl.BlockSpec((1, tm, cc), row), pl.BlockSpec((1, 8, cc), lambda b, i: (b, 0, 0)),
                 pl.BlockSpec((1, tm, nh * LANES), row), pl.BlockSpec((1, tm, nh * LANES), row),
                 pl.BlockSpec((1, tm, nh * LANES), row), kt_spec, kt_spec, pl.BlockSpec((1, tm, nh), row)]
    aliases = {}
    if kv_prev is not None:
        in_specs += [pl.BlockSpec(memory_space=pl.ANY)] * 2
        args += list(kv_prev)
        aliases = {len(args) - 2: 5, len(args) - 1: 6}
    return pl.pallas_call(
        functools.partial(_even_seq_kernel, nh=nh, cc=cc, tm=tm, aliased=kv_prev is not None),
        grid=(g, l // tm), in_specs=in_specs, out_specs=out_specs, out_shape=out_shape,
        scratch_shapes=[pltpu.VMEM((tm + 8, cc), F32), pltpu.VMEM((1, LANES), F32)],
        input_output_aliases=aliases,
        compiler_params=_cparams("arbitrary", "arbitrary"),
        name="even_in_seq",
    )(*args)


def _even_in_dec(x, gain, mod, wts, w_conv, li, prev0, prev1, nh, hd):
    g, l, d = x.shape
    cc = nh * hd
    e = li // 2
    w_main, w_f, bf, gq, gk = wts
    p = _even_consts(nh, hd, 8)[0]
    whole = lambda s: pl.BlockSpec(s, lambda i: (0, 0, 0))
    in_specs = [whole((1, l, d)), _layer(gain, li), _mod_spec(mod, li, 0, d, 1), _mod_spec(mod, li, 1, d, 1),
                _layer(w_main, e, 6 * cc), _layer(w_f, e), _layer(bf, e), _layer(gq, e), _layer(gk, e), _layer(w_conv, e),
                _full(p.shape), whole((1, l, cc)), whole((1, l, cc))]
    out_shape = [jax.ShapeDtypeStruct((1, l, cc), BF16)] + [jax.ShapeDtypeStruct((1, l, cc), F32)] * 4 \
        + [jax.ShapeDtypeStruct((1, l, LANES), F32)]
    out_specs = [whole((1, l, cc))] * 5 + [whole((1, l, LANES))]
    return pl.pallas_call(
        functools.partial(_even_dec_kernel, nh=nh, cc=cc),
        grid=(1,), in_specs=in_specs, out_specs=out_specs, out_shape=out_shape,
        compiler_params=_cparams("arbitrary"),
        name="even_in_dec",
    )(x, gain, mod, mod, w_main, w_f, bf, gq, gk, w_conv, p, prev0, prev1)


def _attn_kernel(it_ref, jt_ref, qp_ref, kp_ref, va_ref, o_ref, m_sc, acc_sc, s_sc, p_sc, *, nh, hd, tq):
    i = it_ref[pl.program_id(1)]
    j = jt_ref[pl.program_id(1)]
    reps = tq // LANES

    @pl.when(j == 0)
    def _():
        m_sc[...] = jnp.full(m_sc.shape, NEG, F32)
        acc_sc[...] = jnp.zeros_like(acc_sc)

    def scores(h):
        s_sc[h % 2] = _dot_nt(qp_ref[0, :, h * LANES:(h + 1) * LANES], kp_ref[0, :, h * LANES:(h + 1) * LANES])

    def step(diagonal):
        scores(0)
        for h in range(nh):
            if h + 1 < nh:
                scores(h + 1)
            s = s_sc[h % 2]
            if diagonal:
                row = lax.broadcasted_iota(jnp.int32, s.shape, 0)
                col = lax.broadcasted_iota(jnp.int32, s.shape, 1)
                s = jnp.where(col <= row, s, NEG)
            m_old = m_sc[h]
            m_new = jnp.maximum(m_old, jnp.max(s, axis=1, keepdims=True))
            alpha = jnp.exp2(m_old - m_new)
            p_sc[h % 2] = jnp.exp2(s - jnp.concatenate([m_new] * reps, axis=1)).astype(BF16)
            pv = _dot(p_sc[h % 2], va_ref[0, :, h * LANES:(h + 1) * LANES])
            acc_sc[h] = alpha * acc_sc[h] + pv
            m_sc[h] = m_new

    @pl.when(j < i)
    def _():
        step(False)

    @pl.when(j == i)
    def _():
        step(True)
        lane = lax.broadcasted_iota(jnp.int32, (tq, LANES), 1)
        half = LANES // 2
        for pair in range(nh // 2):
            a0 = acc_sc[2 * pair]
            a1 = acc_sc[2 * pair + 1]
            o0 = a0 / pltpu.roll(a0, half, axis=1)
            o1 = a1 / pltpu.roll(a1, half, axis=1)
            o_ref[0, :, pair * LANES:(pair + 1) * LANES] = jnp.where(
                lane < hd, o0, pltpu.roll(o1, half, axis=1)).astype(BF16)


def _attn_prompt(qp, kp, va, nh, hd, tq):
    g, l, _ = qp.shape
    cc = nh * hd
    assert 2 * hd == LANES, "value slots are [v | ones] halves of one 128-lane slot"
    nq = l // tq
    pairs =[(i, j) for i in range(nq) for j in range(i + 1)]
    it = jnp.asarray([p[0] for p in pairs], jnp.int32)
    jt = jnp.asarray([p[1] for p in pairs], jnp.int32)
    qrow = lambda b, p, it, jt: (b, it[p], 0)
    krow = lambda b, p, it, jt: (b, jt[p], 0)
    grid_spec = pltpu.PrefetchScalarGridSpec(
        num_scalar_prefetch=2, grid=(g, len(pairs)),
        in_specs=[pl.BlockSpec((1, tq, nh * LANES), qrow), pl.BlockSpec((1, tq, nh * LANES), krow),
                  pl.BlockSpec((1, tq, nh * LANES), krow)],
        out_specs=pl.BlockSpec((1, tq, cc), qrow),
        scratch_shapes=[pltpu.VMEM((nh, tq, LANES), F32), pltpu.VMEM((nh, tq, LANES), F32),
                        pltpu.VMEM((2, tq, tq), F32), pltpu.VMEM((2, tq, tq), BF16)])
    return pl.pallas_call(
        functools.partial(_attn_kernel, nh=nh, hd=hd, tq=tq),
        grid_spec=grid_spec,
        out_shape=jax.ShapeDtypeStruct((g, l, cc), BF16),
        compiler_params=_cparams("arbitrary", "arbitrary"),
        name="attn_prompt",
    )(it, jt, qp, kp, va)


def _attn_dec_kernel(pt_ref, qcol_ref, kncol_ref, vncol_ref, lfnew_ref, uincl_ref, ones_ref, *rest, npg, nh):
    k_refs = rest[:npg]
    v_refs = rest[npg:2 * npg]
    lf_refs = rest[2 * npg:3 * npg]
    o_ref = rest[3 * npg]
    m_sc, l_sc, acc_sc, c_sc = rest[3 * npg + 1:]
    g = pl.program_id(1)

    @pl.when(g == 0)
    def _():
        m_sc[...] = jnp.full(m_sc.shape, NEG, F32)
        l_sc[...] = jnp.zeros_like(l_sc)
        acc_sc[...] = jnp.zeros_like(acc_sc)
        c_sc[...] = jnp.zeros_like(c_sc)

    parts = []
    for j in range(npg):
        hi, mid, lo = _split3(lf_refs[j][0, 0])
        parts += [hi.astype(F32), mid.astype(F32), lo.astype(F32), jnp.zeros((nh, LANES), F32)]
    parts = jnp.concatenate(parts, axis=0).astype(BF16)
    cum = _dot(parts, uincl_ref[...])
    tot = _dot(parts, ones_ref[...])
    fs = []
    c = c_sc[...]
    for j in range(npg):
        r = 4 * nh * j
        fs.append(c + (cum[r:r + nh] + cum[r + nh:r + 2 * nh] + cum[r + 2 * nh:r + 3 * nh]))
        c = c + (tot[r:r + nh] + tot[r + nh:r + 2 * nh] + tot[r + 2 * nh:r + 3 * nh])
    c_sc[...] = c

    for h in range(nh):
        qc = qcol_ref[0, h]
        rows = [jnp.sum(qc * k_refs[j][0, 0, h], axis=0, keepdims=True) - fs[j][h:h + 1, :] for j in range(npg)]
        m_old = m_sc[h]
        m_new = jnp.maximum(m_old, jnp.max(functools.reduce(jnp.maximum, rows), axis=1, keepdims=True))
        alpha = jnp.exp(m_old - m_new)
        ps = [jnp.exp(r - m_new) for r in rows]
        l_sc[h] = alpha * l_sc[h] + jnp.sum(functools.reduce(jnp.add, ps), axis=1, keepdims=True)
        acc = alpha * acc_sc[h]
        for j in range(npg):
            acc = acc + ps[j] * v_refs[j][0, 0, h]
        acc_sc[h] = acc
        m_sc[h] = m_new

    @pl.when(g == pl.num_programs(1) - 1)
    def _():
        lane = lax.broadcasted_iota(jnp.int32, (1, LANES), 1)
        for h in range(nh):
            f_new = c_sc[h:h + 1, :] + lfnew_ref[0, h:h + 1, :]
            s_new = jnp.sum(qcol_ref[0, h] * kncol_ref[0, h], axis=0, keepdims=True) - f_new
            s_new = jnp.where(lane == 0, s_new, NEG)
            m_old = m_sc[h]
            m_new = jnp.maximum(m_old, jnp.max(s_new, axis=1, keepdims=True))
            alpha = jnp.exp(m_old - m_new)
            p_new = jnp.exp(s_new - m_new)
            l_fin = alpha * l_sc[h] + jnp.sum(p_new, axis=1, keepdims=True)
            acc = (alpha * acc_sc[h] + p_new * vncol_ref[0, h]) / l_fin
            hi, mid, lo = _split3(acc)
            ones = ones_ref[0:ONES_ROWS, :]
            o = _dot_nt(ones, hi) + _dot_nt(ones, mid) + _dot_nt(ones, lo)
            o_ref[0, h:h + 1, :] = o[0:1, :]


def _attn_decode(page_table, q, knew, vnew, lfnew, cache_kt, cache_vt, cache_lft, e, nh, hd):
    n_seq, n_pages = page_table.shape
    page = cache_kt.shape[-1]
    assert page == LANES and n_pages % PAGES_PER_STEP == 0
    npg = PAGES_PER_STEP
    col = lambda a: jnp.broadcast_to(a.reshape(n_seq, nh, hd, 1), (n_seq, nh, hd, LANES))
    uincl = jnp.asarray(np.triu(np.ones((page, page))), BF16)
    ones = jnp.ones((LANES, LANES), BF16)

    def pg(j, nd):
        return lambda b, g, pt: (pt[b, g * npg + j], e) + (0,) * nd

    per_seq4 = lambda b, g, pt: (b, 0, 0, 0)
    per_seq3 = lambda b, g, pt: (b, 0, 0)
    const2 = lambda b, g, pt: (0, 0)
    in_specs = [pl.BlockSpec((1, nh, hd, LANES), per_seq4), pl.BlockSpec((1, nh, hd, LANES), per_seq4),
                pl.BlockSpec((1, nh, hd, LANES), per_seq4), pl.BlockSpec((1, nh, 1), per_seq3),
                pl.BlockSpec((page, page), const2), pl.BlockSpec((LANES, LANES), const2)]
    in_specs += [pl.BlockSpec((1, 1, nh, hd, page), pg(j, 3)) for j in range(npg)]
    in_specs += [pl.BlockSpec((1, 1, nh, hd, page), pg(j, 3)) for j in range(npg)]
    in_specs += [pl.BlockSpec((1, 1, nh, page), pg(j, 2)) for j in range(npg)]
    grid_spec = pltpu.PrefetchScalarGridSpec(
        num_scalar_prefetch=1, grid=(n_seq, n_pages // npg), in_specs=in_specs,
        out_specs=pl.BlockSpec((1, nh, hd), per_seq3),
        scratch_shapes=[pltpu.VMEM((nh, 1, LANES), F32), pltpu.VMEM((nh, 1, LANES), F32),
                        pltpu.VMEM((nh, hd, LANES), F32), pltpu.VMEM((nh, LANES), F32)])
    o = pl.pallas_call(
        functools.partial(_attn_dec_kernel, npg=npg, nh=nh),
        grid_spec=grid_spec,
        out_shape=jax.ShapeDtypeStruct((n_seq, nh, hd), F32),
        compiler_params=_cparams("arbitrary", "arbitrary"),
        name="attn_decode",
    )(page_table, col(q), col(knew), col(vnew), lfnew.reshape(n_seq, nh, 1), uincl, ones,
      *([cache_kt] * npg), *([cache_vt] * npg), *([cache_lft] * npg))
    return o.reshape(n_seq, 1, nh * hd)


def _odd_in_kernel(x_ref, g_ref, sh_ref, sc_ref, w_ref, wgd_ref, wgk_ref, bgk_ref,
                   q_ref, k_ref, v_ref, sr_ref, gl_ref, *, qk, vv, dk):
    hb = _norm_mod(x_ref[0], g_ref[...], sh_ref[0], sc_ref[0]).astype(BF16)
    q_ref[0] = _dot(hb, w_ref[:, 0:qk]) * (dk ** -0.5)
    k_ref[0] = _dot(hb, w_ref[:, qk:2 * qk])
    gd = _dot(hb, wgd_ref[...]).astype(BF16)
    gl_ref[0] = _log_sigmoid(_dot(gd, wgk_ref[...]) + bgk_ref[...]) * (1.0 / GLA_TAU)
    v_ref[0] = _dot(hb, w_ref[:, 2 * qk:2 * qk + vv]).astype(BF16)
    sr_ref[0] = _silu(_dot(hb, w_ref[:, 2 * qk + vv:2 * qk + 2 * vv]))


def _gla_seq_kernel(q_ref, k_ref, g_ref, v_ref, sr_ref, go_ref, tri_ref, o_ref, st_ref, s_sc, *, nh, dk, dv, tt):
    @pl.when(pl.program_id(1) == 0)
    def _():
        s_sc[...] = jnp.zeros_like(s_sc)

    o_ref[0] = _gla_tile(q_ref[0], k_ref[0], g_ref[0], v_ref[0], sr_ref[0], go_ref, tri_ref, s_sc, nh, dk, dv, tt)
    st_ref[0] = s_sc[...]


def _odd_weights(w_in, w_gk, b_gk, qk, vv):
    n, d, _ = w_in.shape
    rank = w_gk.shape[1]
    w_main = w_in.astype(BF16)
    w_gd = jnp.zeros((n, d, LANES), BF16).at[:, :, :rank].set(w_in[:, :, 2 * qk + 2 * vv:].astype(BF16))
    wgk = jnp.zeros((n, LANES, qk), BF16).at[:, :rank].set(w_gk.astype(BF16))
    return w_main, w_gd, wgk, b_gk.reshape(n, 1, qk)


def _odd_in(x, gain, mod, wts, li, qk, vv, dk, tm):
    g, l, d = x.shape
    oi = li // 2
    w_main, w_gd, wgk, bgk = wts
    row = lambda b, i: (b, i, 0)
    in_specs = [pl.BlockSpec((1, tm, d), row), _layer(gain, li), _mod_spec(mod, li, 0, d, 2), _mod_spec(mod, li, 1, d, 2),
                _layer(w_main, oi, 2 * qk + 2 * vv), _layer(w_gd, oi), _layer(wgk, oi), _layer(bgk, oi)]
    out_shape = [jax.ShapeDtypeStruct((g, l, qk), F32), jax.ShapeDtypeStruct((g, l, qk), F32),
                 jax.ShapeDtypeStruct((g, l, vv), BF16), jax.ShapeDtypeStruct((g, l, vv), F32),
                 jax.ShapeDtypeStruct((g, l, qk), F32)]
    out_specs = [pl.BlockSpec((1, tm, qk), row), pl.BlockSpec((1, tm, qk), row), pl.BlockSpec((1, tm, vv), row),
                 pl.BlockSpec((1, tm, vv), row), pl.BlockSpec((1, tm, qk), row)]
    return pl.pallas_call(
        functools.partial(_odd_in_kernel, qk=qk, vv=vv, dk=dk),
        grid=(g, l // tm), in_specs=in_specs, out_specs=out_specs, out_shape=out_shape,
        compiler_params=_cparams("arbitrary", "arbitrary"),
        name="odd_in",
    )(x, gain, mod, mod, w_main, w_gd, wgk, bgk)


def _gla_tile(q, k, gl, v, sr, go_ref, tri_ref, s_sc, nh, dk, dv, tt):
    c = GLA_CHUNK
    qk = nh * dk
    b_all = _sum3(_dot(tri_ref[...], jnp.concatenate(_split3(gl), axis=1)), qk)
    nchunk = tt // c
    row = lax.broadcasted_iota(jnp.int32, (tt, tt), 0)
    col = lax.broadcasted_iota(jnp.int32, (tt, tt), 1)
    blockmask = (col <= row) & (col >= (row // c) * c)
    b_last = [b_all[(n + 1) * c - 1:(n + 1) * c, :] for n in range(nchunk)]
    b_last_rows = jnp.concatenate([jnp.broadcast_to(bl, (c, qk)) for bl in b_last], axis=0)
    q_e = (q * jnp.exp(b_all)).astype(BF16)
    k_e = (k * jnp.exp(-b_all)).astype(BF16)
    k_d = (k * jnp.exp(b_last_rows - b_all)).astype(BF16)
    decay = [jnp.exp(bl) for bl in b_last]
    out = []
    for h in range(nh):
        kc = slice(h * dk, (h + 1) * dk)
        vc = slice(h * dv, (h + 1) * dv)
        v_h = v[:, vc]
        a = jnp.where(blockmask, _dot_nt(q_e[:, kc], k_e[:, kc]), 0.0)
        o_intra = _dot(a.astype(BF16), v_h)
        s_t = s_sc[h]
        o_inter = []
        for n in range(nchunk):
            rows = slice(n * c, (n + 1) * c)
            o_inter.append(_dot_nt(q_e[rows, kc], s_t.astype(BF16)))
            s_t = s_t * decay[n][:, kc] + _dot_tn(v_h[rows], k_d[rows, kc])
        s_sc[h] = s_t
        o = o_intra + jnp.concatenate(o_inter, axis=0)
        y = o * lax.rsqrt(jnp.mean(o * o, axis=-1, keepdims=True) + EPS) * go_ref[...]
        out.append((y * sr[:, vc]).astype(BF16))
    return jnp.concatenate(out, axis=1)


def _gla_seq(q, k, gl, v, sr, g_o, oi, nh, dk, dv, tt):
    g, l, qk = q.shape
    vv = nh * dv
    tri = jnp.asarray(np.kron(np.eye(tt // GLA_CHUNK), np.tril(np.ones((GLA_CHUNK, GLA_CHUNK)))), BF16)
    row = lambda b, i: (b, i, 0)
    return pl.pallas_call(
        functools.partial(_gla_seq_kernel, nh=nh, dk=dk, dv=dv, tt=tt),
        grid=(g, l // tt),
        in_specs=[pl.BlockSpec((1, tt, qk), row), pl.BlockSpec((1, tt, qk), row), pl.BlockSpec((1, tt, qk), row),
                  pl.BlockSpec((1, tt, vv), row), pl.BlockSpec((1, tt, vv), row), _layer(g_o, oi), _full((tt, tt))],
        out_specs=[pl.BlockSpec((1, tt, vv), row), pl.BlockSpec((1, nh, dv, dk), lambda b, i: (b, 0, 0, 0))],
        out_shape=[jax.ShapeDtypeStruct((g, l, vv), BF16), jax.ShapeDtypeStruct((g, nh, dv, dk), F32)],
        scratch_shapes=[pltpu.VMEM((nh, dv, dk), F32)],
        compiler_params=_cparams("arbitrary", "arbitrary"),
        name="gla_seq",
    )(q, k, gl, v, sr, g_o, tri)


def _gla_dec_kernel(q_ref, k_ref, g_ref, v_ref, sr_ref, go_ref, s0_ref, *rest, nh, dk, dv, aliased):
    o_ref, s_ref = rest[1:] if aliased else rest

    def column(r):
        return jnp.broadcast_to(r, (dk, dk)).T

    for s in range(q_ref.shape[0]):
        for h in range(nh):
            kc = slice(h * dk, (h + 1) * dk)
            dcol = jnp.exp(column(g_ref[s, :, kc]))
            kcol = column(k_ref[s, :, kc])
            qcol = column(q_ref[s, :, kc])
            halves = []
            for half in range(dv // dk):
                lanes = slice(half * dk, (half + 1) * dk)
                v = v_ref[s, :, h * dv + half * dk:h * dv + (half + 1) * dk].astype(F32)
                s_new = dcol * s0_ref[s, 0, h, :, lanes] + kcol * v
                s_ref[s, h, :, lanes] = s_new
                halves.append(jnp.sum(qcol * s_new, axis=0, keepdims=True))
            o = jnp.concatenate(halves, axis=1)
            y = o * lax.rsqrt(jnp.mean(o * o, axis=-1, keepdims=True) + EPS) * go_ref[...]
            vc = slice(h * dv, (h + 1) * dv)
            o_ref[s, :, vc] = y * sr_ref[s, :, vc]


def _gla_dec(q, k, gl, v, sr, g_o, state_gla, o_idx, s_prev, nh, dk, dv):
    n_seq = q.shape[0]
    qk, vv = nh * dk, nh * dv
    r3 = lambda a: a.reshape(n_seq, 1, a.shape[-1])
    per_seq = lambda b: (b, 0, 0)
    sb = SEQS_PER_GLA_STEP
    assert n_seq % sb == 0
    in_specs = [pl.BlockSpec((sb, 1, qk), per_seq), pl.BlockSpec((sb, 1, qk), per_seq),
                pl.BlockSpec((sb, 1, qk), per_seq), pl.BlockSpec((sb, 1, vv), per_seq),
                pl.BlockSpec((sb, 1, vv), per_seq), pl.BlockSpec((1, dv), lambda b: (0, 0)),
                pl.BlockSpec((sb, 1, nh, dk, dv), lambda b: (b, o_idx, 0, 0, 0))]
    args = [r3(q), r3(k), r3(gl), r3(v), r3(sr), g_o.reshape(1, dv), state_gla]
    aliases = {}
    if s_prev is not None:
        in_specs.append(pl.BlockSpec(memory_space=pl.ANY))
        args.append(s_prev)
        aliases = {len(args) - 1: 1}
    return pl.pallas_call(
        functools.partial(_gla_dec_kernel, nh=nh, dk=dk, dv=dv, aliased=s_prev is not None),
        grid=(n_seq // sb,),
        in_specs=in_specs,
        out_specs=[pl.BlockSpec((sb, 1, vv), per_seq),
                   pl.BlockSpec((sb, None, nh, dk, dv), lambda b: (b, o_idx, 0, 0, 0))],
        out_shape=[jax.ShapeDtypeStruct((n_seq, 1, vv), F32), jax.ShapeDtypeStruct(state_gla.shape, F32)],
        input_output_aliases=aliases,
        compiler_params=_cparams("arbitrary"),
        name="gla_dec",
    )(*args)


def _post_kernel(x_ref, a1_ref, a2_ref, wo_ref, gtm_ref, gf_ref, shf_ref, scf_ref, gtf_ref, w1_ref, w2_ref,
                 o_ref, h_sc, *, ka):
    f = pl.program_id(2)

    @pl.when(f == 0)
    def _():
        m = _dot(a1_ref[0], wo_ref[0:ka, :]) + _dot(a2_ref[0], wo_ref[ka:2 * ka, :])
        x1 = x_ref[0] + gtm_ref[0] * m
        o_ref[0] = x1
        h_sc[...] = _norm_mod(x1, gf_ref[...], shf_ref[0], scf_ref[0]).astype(BF16)

    t = jnp.maximum(_dot(h_sc[...], w1_ref[...]), 0.0)
    o_ref[0] += gtf_ref[0] * _dot((t * t).astype(BF16), w2_ref[...])


def _post(x, a1, a1_col, a2, a2_col, w_out, gain_f, mod, w1, w2, li, tm, tf):
    g, l, d = x.shape
    ka = w_out.shape[1] // 2
    dff = w1.shape[2]
    row = lambda b, i, f: (b, i, 0)
    modc = lambda c: _mod_spec(mod, li, c, d, 3)
    return pl.pallas_call(
        functools.partial(_post_kernel, ka=ka),
        grid=(g, l // tm, dff // tf),
        in_specs=[pl.BlockSpec((1, tm, d), row),
                  pl.BlockSpec((1, tm, ka), lambda b, i, f: (b, i, a1_col)),
                  pl.BlockSpec((1, tm, ka), lambda b, i, f: (b, i, a2_col)),
                  _layer(w_out, li // 2), modc(2), _layer(gain_f, li), modc(3), modc(4), modc(5),
                  pl.BlockSpec((None, d, tf), lambda b, i, f: (li, 0, f)),
                  pl.BlockSpec((None, tf, d), lambda b, i, f: (li, f, 0))],
        out_specs=pl.BlockSpec((1, tm, d), row),
        out_shape=jax.ShapeDtypeStruct((g, l, d), F32),
        scratch_shapes=[pltpu.VMEM((tm, d), BF16)],
        compiler_params=_cparams("arbitrary", "arbitrary", "arbitrary"),
        name="post_mlp",
    )(x, a1, a2, w_out, mod, gain_f, mod, mod, mod, w1, w2)


def kernel(x_prompt, x_sample, c_prompt, c_sample, cache_k, cache_v, cache_logf, page_table, state_conv, state_gla,
           g_mix, g_ffn, w_ada, b_ada, w_in_even, b_f, g_q, g_k, w_conv, w_out_even, w_in_odd, w_gk, b_gk, g_o,
           w_out_odd, w_ff1, w_ff2):
    bp, seq, d = x_prompt.shape
    n_seq = x_sample.shape[0]
    depth = w_ada.shape[0]
    n_even, nh_fox = b_f.shape
    hd_fox = g_q.shape[1]
    cc = nh_fox * hd_fox
    _, n_odd, nh_gla, dk, dv = state_gla.shape
    qk, vv = nh_gla * dk, nh_gla * dv

    mod_all = _adaln(jnp.concatenate([c_prompt, c_sample], axis=0), w_ada, b_ada)

    cache_kt = jnp.transpose(cache_k, (0, 1, 3, 4, 2))
    cache_vt = jnp.transpose(cache_v, (0, 1, 3, 4, 2))
    cache_lft = jnp.swapaxes(cache_logf, 2, 3)

    mod_p = mod_all[:, :bp].reshape(depth, bp, 1, 6 * d)
    mod_s = mod_all[:, bp:].reshape(depth, 1, n_seq, 6 * d)
    gm = g_mix.reshape(depth, 1, d)
    gf = g_ffn.reshape(depth, 1, d)
    w1 = w_ff1.astype(BF16)
    w2 = w_ff2.astype(BF16)
    wts_even = _even_weights(w_in_even, b_f, g_q, g_k, nh_fox, hd_fox)
    wts_odd = _odd_weights(w_in_odd, w_gk, b_gk, qk, vv)
    wo_even = w_out_even.astype(BF16)
    wo_odd = w_out_odd.astype(BF16)
    go = g_o.reshape(n_odd, 1, dv)

    xp = x_prompt
    xs = x_sample.reshape(1, n_seq, d)
    kv_rows = None
    gla_s = None
    lf_rows_p, fox_s, conv_p, conv_s, gla_p = [], [], [], [], []
    for li in range(depth):
        if li % 2 == 0:
            e = li // 2
            ya, utail, qp, kp, va, kt_rows, vt_rows, lf_rows = _even_in_seq(
                xp, gm, mod_p, wts_even, w_conv, li, kv_rows, nh_fox, hd_fox, tm=TM_EVEN)
            kv_rows = (kt_rows, vt_rows)
            o = _attn_prompt(qp, kp, va, nh_fox, hd_fox, tq=TQ_ATTN)
            xp = _post(xp, ya, 0, o, 0, wo_even, gf, mod_p, w1, w2, li, tm=TM_POST, tf=TF_POST)
            lf_rows_p.append(lf_rows)
            conv_p.append(utail[:, 6:8])
            prev0 = state_conv[:, e, 0].reshape(1, n_seq, cc)
            prev1 = state_conv[:, e, 1].reshape(1, n_seq, cc)
            ya, u, qn, kn, v, lfpad = _even_in_dec(xs, gm, mod_s, wts_even, w_conv, li, prev0, prev1, nh_fox, hd_fox)
            lf_new = lfpad[0, :, :nh_fox]
            o = _attn_decode(page_table, qn[0], kn[0], v[0], lf_new, cache_kt, cache_vt, cache_lft, e,
                             nh_fox, hd_fox)
            o = o.reshape(1, n_seq, cc).astype(BF16)
            xs = _post(xs, ya, 0, o, 0, wo_even, gf, mod_s, w1, w2, li, tm=n_seq, tf=TF_POST)
            fox_s.append((kn[0], v[0], lf_new))
            conv_s.append(jnp.stack([prev1[0], u[0]], axis=1))
        else:
            oi = li // 2
            q, k, v, sr, gl = _odd_in(xp, gm, mod_p, wts_odd, li, qk, vv, dk, tm=TM_ODD)
            og, s_t = _gla_seq(q, k, gl, v, sr, go, oi, nh_gla, dk, dv, tt=TT_GLA)
            xp = _post(xp, og, 0, og, 1, wo_odd, gf, mod_p, w1, w2, li, tm=TM_POST, tf=TF_POST)
            gla_p.append(jnp.swapaxes(s_t, 2, 3))
            q, k, v, sr, gl = _odd_in(xs, gm, mod_s, wts_odd, li, qk, vv, dk, tm=n_seq)
            og, gla_s = _gla_dec(q[0], k[0], gl[0], v[0].astype(F32), sr[0], g_o[oi], state_gla, oi, gla_s,
                                 nh_gla, dk, dv)
            og = og.reshape(1, n_seq, vv).astype(BF16)
            xs = _post(xs, og, 0, og, 1, wo_odd, gf, mod_s, w1, w2, li, tm=n_seq, tf=TF_POST)

    heads = lambda a, n: a.reshape(a.shape[0], n, nh_fox, hd_fox)
    k_p = jnp.transpose(kv_rows[0], (0, 1, 4, 2, 3))
    v_p = jnp.transpose(kv_rows[1], (0, 1, 4, 2, 3))
    lf_p = jnp.stack(lf_rows_p, axis=1)
    k_s = jnp.stack([heads(r[0], 1) for r in fox_s], axis=1)
    v_s = jnp.stack([heads(r[1], 1) for r in fox_s], axis=1)
    lf_s = jnp.stack([r[2].reshape(n_seq, 1, nh_fox) for r in fox_s], axis=1)
    return (xp, xs.reshape(n_seq, 1, d), k_p, v_p, lf_p, k_s, v_s, lf_s,
            jnp.stack(conv_p, axis=1), jnp.stack(conv_s, axis=1),
            jnp.stack(gla_p, axis=1), gla_s)
```

```python
import functools

import numpy as np
import jax
import jax.numpy as jnp
from jax import lax
from jax.experimental import pallas as pl
from jax.experimental.pallas import tpu as pltpu

F32 = jnp.float32
BF16 = jnp.bfloat16

EPS = 1e-6
GLA_TAU = 16.0
GLA_CHUNK = 64
LANES = 128
VMEM_LIMIT = 56 * 1024 * 1024
NEG = -1e30
LOG2E = 1.4426950408889634
PAGES_PER_STEP = 32
SEQS_PER_GLA_STEP = 4
ONES_ROWS = 16
TM_EVEN = 512
TM_ODD = 512
TM_POST = 1024
TF_POST = 1024
TQ_ATTN = 512
TT_GLA = 256


def _cparams(*sem):
    return pltpu.CompilerParams(dimension_semantics=sem, vmem_limit_bytes=VMEM_LIMIT)


def _silu(x):
    return x / (1.0 + jnp.exp(-x))


def _log_sigmoid(z):
    return jnp.minimum(z, 0.0) - jnp.log1p(jnp.exp(-jnp.abs(z)))


def _norm_mod(x, gain, shift, scale):
    y = x * lax.rsqrt(jnp.mean(x * x, axis=-1, keepdims=True) + EPS)
    return (y * gain) * (1.0 + scale) + shift


def _split3(a):
    hi = a.astype(BF16)
    r = a - hi.astype(F32)
    mid = r.astype(BF16)
    lo = (r - mid.astype(F32)).astype(BF16)
    return hi, mid, lo


def _dot(a, b):
    return jnp.dot(a, b, preferred_element_type=F32)


def _dot_nt(a, b):
    return lax.dot_general(a, b, (((1,), (1,)), ((), ())), preferred_element_type=F32)


def _dot_tn(a, b):
    return lax.dot_general(a, b, (((0,), (0,)), ((), ())), preferred_element_type=F32)


def _sum3(c, w):
    return c[:, 0:w] + c[:, w:2 * w] + c[:, 2 * w:3 * w]


def _adaln_kernel(c_ref, w_ref, b_ref, o_ref):
    a = _silu(c_ref[...]).astype(BF16)
    o_ref[0] = _dot(a, w_ref[0].astype(BF16)) + b_ref[0]


def _adaln(c_all, w_ada, b_ada):
    depth, d, n = w_ada.shape
    r = c_all.shape[0]
    tn = n // 4
    return pl.pallas_call(
        _adaln_kernel,
        grid=(depth, n // tn),
        in_specs=[pl.BlockSpec((r, d), lambda l, j: (0, 0)),
                  pl.BlockSpec((1, d, tn), lambda l, j: (l, 0, j)),
                  pl.BlockSpec((1, 1, tn), lambda l, j: (l, 0, j))],
        out_specs=pl.BlockSpec((1, r, tn), lambda l, j: (l, 0, j)),
        out_shape=jax.ShapeDtypeStruct((depth, r, n), F32),
        compiler_params=_cparams("arbitrary", "arbitrary"),
        name="adaln",
    )(c_all, w_ada, b_ada.reshape(depth, 1, n))


def _even_project(x_ref, g_ref, sh_ref, sc_ref, w_ref, wf_ref, bf_ref, gq_ref, gk_ref, p_ref, cc, nh):
    hb = _norm_mod(x_ref[0], g_ref[...], sh_ref[0], sc_ref[0]).astype(BF16)
    gb = _dot(hb, w_ref[:, 0:cc])
    u = _dot(hb, w_ref[:, cc:2 * cc]) * _dot(hb, w_ref[:, 2 * cc:3 * cc])
    q = _dot(hb, w_ref[:, 3 * cc:4 * cc])
    k = _dot(hb, w_ref[:, 4 * cc:5 * cc])
    v = _dot(hb, w_ref[:, 5 * cc:6 * cc])
    hd = cc // nh
    qms = _dot((q * q).astype(BF16), p_ref[...]) * (1.0 / hd)
    kms = _dot((k * k).astype(BF16), p_ref[...]) * (1.0 / hd)
    qn = (q * lax.rsqrt(qms + EPS)) * gq_ref[...]
    kn = (k * lax.rsqrt(kms + EPS)) * gk_ref[...]
    fz = _dot(hb, wf_ref[...]) + bf_ref[...]
    lane = lax.broadcasted_iota(jnp.int32, fz.shape, 1)
    lf = jnp.where(lane < nh, _log_sigmoid(fz), 0.0)
    return gb, u, qn, kn, v, lf


def _even_seq_kernel(x_ref, g_ref, sh_ref, sc_ref, w_ref, wf_ref, bf_ref, gq_ref, gk_ref, wc_ref, p_ref,
                     tri_ref, eq_ref, ek_ref, oq_ref, ok_ref, *rest, nh, cc, tm, aliased):
    if aliased:
        rest = rest[2:]
    ya_ref, ut_ref, qp_ref, kp_ref, va_ref, ko_ref, vo_ref, lf_ref, ubuf, fcarry = rest
    i = pl.program_id(1)
    gb, u, qn, kn, v, lf = _even_project(x_ref, g_ref, sh_ref, sc_ref, w_ref, wf_ref, bf_ref,
                                         gq_ref, gk_ref, p_ref, cc, nh)

    @pl.when(i == 0)
    def _():
        ubuf[0:8, :] = jnp.zeros((8, cc), F32)
        fcarry[...] = jnp.zeros_like(fcarry)

    ubuf[8:8 + tm, :] = u
    conv = wc_ref[0:1, :] * ubuf[6:6 + tm, :] + wc_ref[1:2, :] * ubuf[7:7 + tm, :] + wc_ref[2:3, :] * u
    ya_ref[0] = (gb * conv).astype(BF16)
    tail = ubuf[tm:tm + 8, :]
    ubuf[0:8, :] = tail
    ut_ref[0] = tail

    ko_ref[0, 0] = kn.T.reshape(nh, cc // nh, tm)
    vo_ref[0, 0] = v.T.reshape(nh, cc // nh, tm)
    lf_ref[0] = lf[:, 0:nh]
    vb = v.astype(BF16)
    va_ref[0] = jnp.ones((tm, nh * LANES), BF16)
    for h in range(nh):
        va_ref[0, :, h * LANES:h * LANES + cc // nh] = vb[:, h * (cc // nh):(h + 1) * (cc // nh)]

    c3 = _dot(tri_ref[...], jnp.concatenate(_split3(lf), axis=1))
    f = _sum3(c3, LANES) + fcarry[...]
    fcarry[...] = f[tm - 1:tm, :]
    f3 = jnp.concatenate(_split3(f * LOG2E), axis=1)

    qp_ref[0] = (_dot(f3, eq_ref[...]) + oq_ref[...]).astype(BF16)
    kp_ref[0] = (_dot(f3, ek_ref[...]) + ok_ref[...]).astype(BF16)
    qb = qn.astype(BF16)
    kb = kn.astype(BF16)
    hd = cc // nh
    for h in range(nh):
        qp_ref[0, :, h * LANES:h * LANES + hd] = qb[:, h * hd:(h + 1) * hd]
        kp_ref[0, :, h * LANES:h * LANES + hd] = kb[:, h * hd:(h + 1) * hd]


def _even_dec_kernel(x_ref, g_ref, sh_ref, sc_ref, w_ref, wf_ref, bf_ref, gq_ref, gk_ref, wc_ref, p_ref,
                     p0_ref, p1_ref,
                     ya_ref, u_ref, q_ref, k_ref, v_ref, lf_ref, *, nh, cc):
    gb, u, qn, kn, v, lf = _even_project(x_ref, g_ref, sh_ref, sc_ref, w_ref, wf_ref, bf_ref,
                                         gq_ref, gk_ref, p_ref, cc, nh)
    conv = wc_ref[0:1, :] * p0_ref[0] + wc_ref[1:2, :] * p1_ref[0] + wc_ref[2:3, :] * u
    ya_ref[0] = (gb * conv).astype(BF16)
    u_ref[0] = u
    q_ref[0] = qn
    k_ref[0] = kn
    v_ref[0] = v
    lf_ref[0] = lf


def _even_consts(nh, hd, tm):
    cc = nh * hd
    p = np.kron(np.eye(nh), np.ones((hd, hd)))
    tri = np.tril(np.ones((tm, tm)))
    eq = np.zeros((3 * LANES, nh * LANES))
    ek = np.zeros((3 * LANES, nh * LANES))
    oq = np.zeros((1, nh * LANES))
    ok = np.zeros((1, nh * LANES))
    for h in range(nh):
        for part in range(3):
            eq[part * LANES + h, h * LANES + hd + part] = 1.0
            ek[part * LANES + h, h * LANES + hd + 3 + part] = -1.0
            oq[0, h * LANES + hd + 3 + part] = 1.0
            ok[0, h * LANES + hd + part] = 1.0
    bf = lambda a: jnp.asarray(a, BF16)
    return bf(p), bf(tri), bf(eq), bf(ek), jnp.asarray(oq, F32), jnp.asarray(ok, F32)


def _even_weights(w_in, b_f, g_q, g_k, nh, hd):
    n, d, _ = w_in.shape
    cc = nh * hd
    w_main = w_in.astype(BF16)
    w_f = jnp.zeros((n, d, LANES), BF16).at[:, :, :nh].set(w_in[:, :, 6 * cc:].astype(BF16))
    bf = jnp.zeros((n, 1, LANES), F32).at[:, 0, :nh].set(b_f)
    gq = (jnp.tile(g_q, (1, nh)) * (hd ** -0.5)).reshape(n, 1, cc)
    gk = jnp.tile(g_k, (1, nh)).reshape(n, 1, cc)
    return w_main, w_f, bf, gq, gk


def _full(shape):
    nd = len(shape)
    return pl.BlockSpec(shape, lambda *_: (0,) * nd)


def _layer(a, idx, width=None):
    nd = a.ndim
    shape = a.shape[1:] if width is None else a.shape[1:-1] + (width,)
    return pl.BlockSpec((None,) + shape, lambda *_: (idx,) + (0,) * (nd - 1))


def _mod_spec(mod, li, col, d, nargs):
    r = mod.shape[2]
    if nargs == 1:
        return pl.BlockSpec((None, 1, r, d), lambda i: (li, 0, 0, col))
    if nargs == 2:
        return pl.BlockSpec((None, 1, r, d), lambda b, i: (li, b, 0, col))
    return pl.BlockSpec((None, 1, r, d), lambda b, i, f: (li, b, 0, col))


def _even_in_seq(x, gain, mod, wts, w_conv, li, kv_prev, nh, hd, tm):
    g, l, d = x.shape
    cc = nh * hd
    e = li // 2
    n_even = w_conv.shape[0]
    w_main, w_f, bf, gq, gk = wts
    gq = gq * LOG2E
    p, tri, eq, ek, oq, ok = _even_consts(nh, hd, tm)
    row = lambda b, i: (b, i, 0)
    in_specs = [pl.BlockSpec((1, tm, d), row), _layer(gain, li), _mod_spec(mod, li, 0, d, 2), _mod_spec(mod, li, 1, d, 2),
                _layer(w_main, e, 6 * cc), _layer(w_f, e), _layer(bf, e), _layer(gq, e), _layer(gk, e), _layer(w_conv, e),
                _full(p.shape), _full(tri.shape), _full(eq.shape), _full(ek.shape), _full(oq.shape), _full(ok.shape)]
    args = [x, gain, mod, mod, w_main, w_f, bf, gq, gk, w_conv, p, tri, eq, ek, oq, ok]
    kt_shape = jax.ShapeDtypeStruct((g, n_even, nh, hd, l), F32)
    kt_spec = pl.BlockSpec((1, 1, nh, hd, tm), lambda b, i: (b, e, 0, 0, i))
    out_shape = [jax.ShapeDtypeStruct((g, l, cc), BF16),
                 jax.ShapeDtypeStruct((g, 8, cc), F32),
                 jax.ShapeDtypeStruct((g, l, nh * LANES), BF16),
                 jax.ShapeDtypeStruct((g, l, nh * LANES), BF16),
                 jax.ShapeDtypeStruct((g, l, nh * LANES), BF16),
                 kt_shape, kt_shape,
                 jax.ShapeDtypeStruct((g, l, nh), F32)]
    out_specs = [pl.BlockSpec((1, tm, cc), row), pl.BlockSpec((1, 8, cc), lambda b, i: (b, 0, 0)),
                 pl.BlockSpec((1, tm, nh * LANES), row), pl.BlockSpec((1, tm, nh * LANES), row),
                 pl.BlockSpec((1, tm, nh * LANES), row), kt_spec, kt_spec, pl.BlockSpec((1, tm, nh), row)]
    aliases = {}
    if kv_prev is not None:
        in_specs += [pl.BlockSpec(memory_space=pl.ANY)] * 2
        args += list(kv_prev)
        aliases = {len(args) - 2: 5, len(args) - 1: 6}
    return pl.pallas_call(
        functools.partial(_even_seq_kernel, nh=nh, cc=cc, tm=tm, aliased=kv_prev is not None),
        grid=(g, l // tm), in_specs=in_specs, out_specs=out_specs, out_shape=out_shape,
        scratch_shapes=[pltpu.VMEM((tm + 8, cc), F32), pltpu.VMEM((1, LANES), F32)],
        input_output_aliases=aliases,
        compiler_params=_cparams("arbitrary", "arbitrary"),
        name="even_in_seq",
    )(*args)


def _even_in_dec(x, gain, mod, wts, w_conv, li, prev0, prev1, nh, hd):
    g, l, d = x.shape
    cc = nh * hd
    e = li // 2
    w_main, w_f, bf, gq, gk = wts
    p = _even_consts(nh, hd, 8)[0]
    whole = lambda s: pl.BlockSpec(s, lambda i: (0, 0, 0))
    in_specs = [whole((1, l, d)), _layer(gain, li), _mod_spec(mod, li, 0, d, 1), _mod_spec(mod, li, 1, d, 1),
                _layer(w_main, e, 6 * cc), _layer(w_f, e), _layer(bf, e), _layer(gq, e), _layer(gk, e), _layer(w_conv, e),
                _full(p.shape), whole((1, l, cc)), whole((1, l, cc))]
    out_shape = [jax.ShapeDtypeStruct((1, l, cc), BF16)] + [jax.ShapeDtypeStruct((1, l, cc), F32)] * 4 \
        + [jax.ShapeDtypeStruct((1, l, LANES), F32)]
    out_specs = [whole((1, l, cc))] * 5 + [whole((1, l, LANES))]
    return pl.pallas_call(
        functools.partial(_even_dec_kernel, nh=nh, cc=cc),
        grid=(1,), in_specs=in_specs, out_specs=out_specs, out_shape=out_shape,
        compiler_params=_cparams("arbitrary"),
        name="even_in_dec",
    )(x, gain, mod, mod, w_main, w_f, bf, gq, gk, w_conv, p, prev0, prev1)


def _attn_kernel(it_ref, jt_ref, qp_ref, kp_ref, va_ref, o_ref, m_sc, acc_sc, s_sc, p_sc, *, nh, hd, tq):
    i = it_ref[pl.program_id(1)]
    j = jt_ref[pl.program_id(1)]
    reps = tq // LANES

    @pl.when(j == 0)
    def _():
        m_sc[...] = jnp.full(m_sc.shape, NEG, F32)
        acc_sc[...] = jnp.zeros_like(acc_sc)

    def scores(h):
        s_sc[h % 2] = _dot_nt(qp_ref[0, :, h * LANES:(h + 1) * LANES], kp_ref[0, :, h * LANES:(h + 1) * LANES])

    def step(diagonal):
        scores(0)
        for h in range(nh):
            if h + 1 < nh:
                scores(h + 1)
            s = s_sc[h % 2]
            if diagonal:
                row = lax.broadcasted_iota(jnp.int32, s.shape, 0)
                col = lax.broadcasted_iota(jnp.int32, s.shape, 1)
                s = jnp.where(col <= row, s, NEG)
            m_old = m_sc[h]
            m_new = jnp.maximum(m_old, jnp.max(s, axis=1, keepdims=True))
            alpha = jnp.exp2(m_old - m_new)
            p_sc[h % 2] = jnp.exp2(s - jnp.concatenate([m_new] * reps, axis=1)).astype(BF16)
            pv = _dot(p_sc[h % 2], va_ref[0, :, h * LANES:(h + 1) * LANES])
            acc_sc[h] = alpha * acc_sc[h] + pv
            m_sc[h] = m_new

    @pl.when(j < i)
    def _():
        step(False)

    @pl.when(j == i)
    def _():
        step(True)
        lane = lax.broadcasted_iota(jnp.int32, (tq, LANES), 1)
        half = LANES // 2
        for pair in range(nh // 2):
            a0 = acc_sc[2 * pair]
            a1 = acc_sc[2 * pair + 1]
            o0 = a0 / pltpu.roll(a0, half, axis=1)
            o1 = a1 / pltpu.roll(a1, half, axis=1)
            o_ref[0, :, pair * LANES:(pair + 1) * LANES] = jnp.where(
                lane < hd, o0, pltpu.roll(o1, half, axis=1)).astype(BF16)


def _attn_prompt(qp, kp, va, nh, hd, tq):
    g, l, _ = qp.shape
    cc = nh * hd
    assert 2 * hd == LANES, "value slots are [v | ones] halves of one 128-lane slot"
    nq = l // tq
    pairs =[(i, j) for i in range(nq) for j in range(i + 1)]
    it = jnp.asarray([p[0] for p in pairs], jnp.int32)
    jt = jnp.asarray([p[1] for p in pairs], jnp.int32)
    qrow = lambda b, p, it, jt: (b, it[p], 0)
    krow = lambda b, p, it, jt: (b, jt[p], 0)
    grid_spec = pltpu.PrefetchScalarGridSpec(
        num_scalar_prefetch=2, grid=(g, len(pairs)),
        in_specs=[pl.BlockSpec((1, tq, nh * LANES), qrow), pl.BlockSpec((1, tq, nh * LANES), krow),
                  pl.BlockSpec((1, tq, nh * LANES), krow)],
        out_specs=pl.BlockSpec((1, tq, cc), qrow),
        scratch_shapes=[pltpu.VMEM((nh, tq, LANES), F32), pltpu.VMEM((nh, tq, LANES), F32),
                        pltpu.VMEM((2, tq, tq), F32), pltpu.VMEM((2, tq, tq), BF16)])
    return pl.pallas_call(
        functools.partial(_attn_kernel, nh=nh, hd=hd, tq=tq),
        grid_spec=grid_spec,
        out_shape=jax.ShapeDtypeStruct((g, l, cc), BF16),
        compiler_params=_cparams("arbitrary", "arbitrary"),
        name="attn_prompt",
    )(it, jt, qp, kp, va)


def _attn_dec_kernel(pt_ref, qcol_ref, kncol_ref, vncol_ref, lfnew_ref, uincl_ref, ones_ref, *rest, npg, nh):
    k_refs = rest[:npg]
    v_refs = rest[npg:2 * npg]
    lf_refs = rest[2 * npg:3 * npg]
    o_ref = rest[3 * npg]
    m_sc, l_sc, acc_sc, c_sc = rest[3 * npg + 1:]
    g = pl.program_id(1)

    @pl.when(g == 0)
    def _():
        m_sc[...] = jnp.full(m_sc.shape, NEG, F32)
        l_sc[...] = jnp.zeros_like(l_sc)
        acc_sc[...] = jnp.zeros_like(acc_sc)
        c_sc[...] = jnp.zeros_like(c_sc)

    parts = []
    for j in range(npg):
        hi, mid, lo = _split3(lf_refs[j][0, 0])
        parts += [hi.astype(F32), mid.astype(F32), lo.astype(F32), jnp.zeros((nh, LANES), F32)]
    parts = jnp.concatenate(parts, axis=0).astype(BF16)
    cum = _dot(parts, uincl_ref[...])
    tot = _dot(parts, ones_ref[...])
    fs = []
    c = c_sc[...]
    for j in range(npg):
        r = 4 * nh * j
        fs.append(c + (cum[r:r + nh] + cum[r + nh:r + 2 * nh] + cum[r + 2 * nh:r + 3 * nh]))
        c = c + (tot[r:r + nh] + tot[r + nh:r + 2 * nh] + tot[r + 2 * nh:r + 3 * nh])
    c_sc[...] = c

    for h in range(nh):
        qc = qcol_ref[0, h]
        rows = [jnp.sum(qc * k_refs[j][0, 0, h], axis=0, keepdims=True) - fs[j][h:h + 1, :] for j in range(npg)]
        m_old = m_sc[h]
        m_new = jnp.maximum(m_old, jnp.max(functools.reduce(jnp.maximum, rows), axis=1, keepdims=True))
        alpha = jnp.exp(m_old - m_new)
        ps = [jnp.exp(r - m_new) for r in rows]
        l_sc[h] = alpha * l_sc[h] + jnp.sum(functools.reduce(jnp.add, ps), axis=1, keepdims=True)
        acc = alpha * acc_sc[h]
        for j in range(npg):
            acc = acc + ps[j] * v_refs[j][0, 0, h]
        acc_sc[h] = acc
        m_sc[h] = m_new

    @pl.when(g == pl.num_programs(1) - 1)
    def _():
        lane = lax.broadcasted_iota(jnp.int32, (1, LANES), 1)
        for h in range(nh):
            f_new = c_sc[h:h + 1, :] + lfnew_ref[0, h:h + 1, :]
            s_new = jnp.sum(qcol_ref[0, h] * kncol_ref[0, h], axis=0, keepdims=True) - f_new
            s_new = jnp.where(lane == 0, s_new, NEG)
            m_old = m_sc[h]
            m_new = jnp.maximum(m_old, jnp.max(s_new, axis=1, keepdims=True))
            alpha = jnp.exp(m_old - m_new)
            p_new = jnp.exp(s_new - m_new)
            l_fin = alpha * l_sc[h] + jnp.sum(p_new, axis=1, keepdims=True)
            acc = (alpha * acc_sc[h] + p_new * vncol_ref[0, h]) / l_fin
            hi, mid, lo = _split3(acc)
            ones = ones_ref[0:ONES_ROWS, :]
            o = _dot_nt(ones, hi) + _dot_nt(ones, mid) + _dot_nt(ones, lo)
            o_ref[0, h:h + 1, :] = o[0:1, :]


def _attn_decode(page_table, q, knew, vnew, lfnew, cache_kt, cache_vt, cache_lft, e, nh, hd):
    n_seq, n_pages = page_table.shape
    page = cache_kt.shape[-1]
    assert page == LANES and n_pages % PAGES_PER_STEP == 0
    npg = PAGES_PER_STEP
    col = lambda a: jnp.broadcast_to(a.reshape(n_seq, nh, hd, 1), (n_seq, nh, hd, LANES))
    uincl = jnp.asarray(np.triu(np.ones((page, page))), BF16)
    ones = jnp.ones((LANES, LANES), BF16)

    def pg(j, nd):
        return lambda b, g, pt: (pt[b, g * npg + j], e) + (0,) * nd

    per_seq4 = lambda b, g, pt: (b, 0, 0, 0)
    per_seq3 = lambda b, g, pt: (b, 0, 0)
    const2 = lambda b, g, pt: (0, 0)
    in_specs = [pl.BlockSpec((1, nh, hd, LANES), per_seq4), pl.BlockSpec((1, nh, hd, LANES), per_seq4),
                pl.BlockSpec((1, nh, hd, LANES), per_seq4), pl.BlockSpec((1, nh, 1), per_seq3),
                pl.BlockSpec((page, page), const2), pl.BlockSpec((LANES, LANES), const2)]
    in_specs += [pl.BlockSpec((1, 1, nh, hd, page), pg(j, 3)) for j in range(npg)]
    in_specs += [pl.BlockSpec((1, 1, nh, hd, page), pg(j, 3)) for j in range(npg)]
    in_specs += [pl.BlockSpec((1, 1, nh, page), pg(j, 2)) for j in range(npg)]
    grid_spec = pltpu.PrefetchScalarGridSpec(
        num_scalar_prefetch=1, grid=(n_seq, n_pages // npg), in_specs=in_specs,
        out_specs=pl.BlockSpec((1, nh, hd), per_seq3),
        scratch_shapes=[pltpu.VMEM((nh, 1, LANES), F32), pltpu.VMEM((nh, 1, LANES), F32),
                        pltpu.VMEM((nh, hd, LANES), F32), pltpu.VMEM((nh, LANES), F32)])
    o = pl.pallas_call(
        functools.partial(_attn_dec_kernel, npg=npg, nh=nh),
        grid_spec=grid_spec,
        out_shape=jax.ShapeDtypeStruct((n_seq, nh, hd), F32),
        compiler_params=_cparams("arbitrary", "arbitrary"),
        name="attn_decode",
    )(page_table, col(q), col(knew), col(vnew), lfnew.reshape(n_seq, nh, 1), uincl, ones,
      *([cache_kt] * npg), *([cache_vt] * npg), *([cache_lft] * npg))
    return o.reshape(n_seq, 1, nh * hd)


def _odd_in_kernel(x_ref, g_ref, sh_ref, sc_ref, w_ref, wgd_ref, wgk_ref, bgk_ref,
                   q_ref, k_ref, v_ref, sr_ref, gl_ref, *, qk, vv, dk):
    hb = _norm_mod(x_ref[0], g_ref[...], sh_ref[0], sc_ref[0]).astype(BF16)
    q_ref[0] = _dot(hb, w_ref[:, 0:qk]) * (dk ** -0.5)
    k_ref[0] = _dot(hb, w_ref[:, qk:2 * qk])
    gd = _dot(hb, wgd_ref[...]).astype(BF16)
    gl_ref[0] = _log_sigmoid(_dot(gd, wgk_ref[...]) + bgk_ref[...]) * (1.0 / GLA_TAU)
    v_ref[0] = _dot(hb, w_ref[:, 2 * qk:2 * qk + vv]).astype(BF16)
    sr_ref[0] = _silu(_dot(hb, w_ref[:, 2 * qk + vv:2 * qk + 2 * vv]))


def _gla_seq_kernel(q_ref, k_ref, g_ref, v_ref, sr_ref, go_ref, tri_ref, o_ref, st_ref, s_sc, *, nh, dk, dv, tt):
    @pl.when(pl.program_id(1) == 0)
    def _():
        s_sc[...] = jnp.zeros_like(s_sc)

    o_ref[0] = _gla_tile(q_ref[0], k_ref[0], g_ref[0], v_ref[0], sr_ref[0], go_ref, tri_ref, s_sc, nh, dk, dv, tt)
    st_ref[0] = s_sc[...]


def _odd_weights(w_in, w_gk, b_gk, qk, vv):
    n, d, _ = w_in.shape
    rank = w_gk.shape[1]
    w_main = w_in.astype(BF16)
    w_gd = jnp.zeros((n, d, LANES), BF16).at[:, :, :rank].set(w_in[:, :, 2 * qk + 2 * vv:].astype(BF16))
    wgk = jnp.zeros((n, LANES, qk), BF16).at[:, :rank].set(w_gk.astype(BF16))
    return w_main, w_gd, wgk, b_gk.reshape(n, 1, qk)


def _odd_in(x, gain, mod, wts, li, qk, vv, dk, tm):
    g, l, d = x.shape
    oi = li // 2
    w_main, w_gd, wgk, bgk = wts
    row = lambda b, i: (b, i, 0)
    in_specs = [pl.BlockSpec((1, tm, d), row), _layer(gain, li), _mod_spec(mod, li, 0, d, 2), _mod_spec(mod, li, 1, d, 2),
                _layer(w_main, oi, 2 * qk + 2 * vv), _layer(w_gd, oi), _layer(wgk, oi), _layer(bgk, oi)]
    out_shape = [jax.ShapeDtypeStruct((g, l, qk), F32), jax.ShapeDtypeStruct((g, l, qk), F32),
                 jax.ShapeDtypeStruct((g, l, vv), BF16), jax.ShapeDtypeStruct((g, l, vv), F32),
                 jax.ShapeDtypeStruct((g, l, qk), F32)]
    out_specs = [pl.BlockSpec((1, tm, qk), row), pl.BlockSpec((1, tm, qk), row), pl.BlockSpec((1, tm, vv), row),
                 pl.BlockSpec((1, tm, vv), row), pl.BlockSpec((1, tm, qk), row)]
    return pl.pallas_call(
        functools.partial(_odd_in_kernel, qk=qk, vv=vv, dk=dk),
        grid=(g, l // tm), in_specs=in_specs, out_specs=out_specs, out_shape=out_shape,
        compiler_params=_cparams("arbitrary", "arbitrary"),
        name="odd_in",
    )(x, gain, mod, mod, w_main, w_gd, wgk, bgk)


def _gla_tile(q, k, gl, v, sr, go_ref, tri_ref, s_sc, nh, dk, dv, tt):
    c = GLA_CHUNK
    qk = nh * dk
    b_all = _sum3(_dot(tri_ref[...], jnp.concatenate(_split3(gl), axis=1)), qk)
    nchunk = tt // c
    row = lax.broadcasted_iota(jnp.int32, (tt, tt), 0)
    col = lax.broadcasted_iota(jnp.int32, (tt, tt), 1)
    blockmask = (col <= row) & (col >= (row // c) * c)
    b_last = [b_all[(n + 1) * c - 1:(n + 1) * c, :] for n in range(nchunk)]
    b_last_rows = jnp.concatenate([jnp.broadcast_to(bl, (c, qk)) for bl in b_last], axis=0)
    q_e = (q * jnp.exp(b_all)).astype(BF16)
    k_e = (k * jnp.exp(-b_all)).astype(BF16)
    k_d = (k * jnp.exp(b_last_rows - b_all)).astype(BF16)
    decay = [jnp.exp(bl) for bl in b_last]
    out = []
    for h in range(nh):
        kc = slice(h * dk, (h + 1) * dk)
        vc = slice(h * dv, (h + 1) * dv)
        v_h = v[:, vc]
        a = jnp.where(blockmask, _dot_nt(q_e[:, kc], k_e[:, kc]), 0.0)
        o_intra = _dot(a.astype(BF16), v_h)
        s_t = s_sc[h]
        o_inter = []
        for n in range(nchunk):
            rows = slice(n * c, (n + 1) * c)
            o_inter.append(_dot_nt(q_e[rows, kc], s_t.astype(BF16)))
            s_t = s_t * decay[n][:, kc] + _dot_tn(v_h[rows], k_d[rows, kc])
        s_sc[h] = s_t
        o = o_intra + jnp.concatenate(o_inter, axis=0)
        y = o * lax.rsqrt(jnp.mean(o * o, axis=-1, keepdims=True) + EPS) * go_ref[...]
        out.append((y * sr[:, vc]).astype(BF16))
    return jnp.concatenate(out, axis=1)


def _gla_seq(q, k, gl, v, sr, g_o, oi, nh, dk, dv, tt):
    g, l, qk = q.shape
    vv = nh * dv
    tri = jnp.asarray(np.kron(np.eye(tt // GLA_CHUNK), np.tril(np.ones((GLA_CHUNK, GLA_CHUNK)))), BF16)
    row = lambda b, i: (b, i, 0)
    return pl.pallas_call(
        functools.partial(_gla_seq_kernel, nh=nh, dk=dk, dv=dv, tt=tt),
        grid=(g, l // tt),
        in_specs=[pl.BlockSpec((1, tt, qk), row), pl.BlockSpec((1, tt, qk), row), pl.BlockSpec((1, tt, qk), row),
                  pl.BlockSpec((1, tt, vv), row), pl.BlockSpec((1, tt, vv), row), _layer(g_o, oi), _full((tt, tt))],
        out_specs=[pl.BlockSpec((1, tt, vv), row), pl.BlockSpec((1, nh, dv, dk), lambda b, i: (b, 0, 0, 0))],
        out_shape=[jax.ShapeDtypeStruct((g, l, vv), BF16), jax.ShapeDtypeStruct((g, nh, dv, dk), F32)],
        scratch_shapes=[pltpu.VMEM((nh, dv, dk), F32)],
        compiler_params=_cparams("arbitrary", "arbitrary"),
        name="gla_seq",
    )(q, k, gl, v, sr, g_o, tri)


def _gla_dec_kernel(q_ref, k_ref, g_ref, v_ref, sr_ref, go_ref, s0_ref, *rest, nh, dk, dv, aliased):
    o_ref, s_ref = rest[1:] if aliased else rest

    def column(r):
        return jnp.broadcast_to(r, (dk, dk)).T

    for s in range(q_ref.shape[0]):
        for h in range(nh):
            kc = slice(h * dk, (h + 1) * dk)
            dcol = jnp.exp(column(g_ref[s, :, kc]))
            kcol = column(k_ref[s, :, kc])
            qcol = column(q_ref[s, :, kc])
            halves = []
            for half in range(dv // dk):
                lanes = slice(half * dk, (half + 1) * dk)
                v = v_ref[s, :, h * dv + half * dk:h * dv + (half + 1) * dk].astype(F32)
                s_new = dcol * s0_ref[s, 0, h, :, lanes] + kcol * v
                s_ref[s, h, :, lanes] = s_new
                halves.append(jnp.sum(qcol * s_new, axis=0, keepdims=True))
            o = jnp.concatenate(halves, axis=1)
            y = o * lax.rsqrt(jnp.mean(o * o, axis=-1, keepdims=True) + EPS) * go_ref[...]
            vc = slice(h * dv, (h + 1) * dv)
            o_ref[s, :, vc] = y * sr_ref[s, :, vc]


def _gla_dec(q, k, gl, v, sr, g_o, state_gla, o_idx, s_prev, nh, dk, dv):
    n_seq = q.shape[0]
    qk, vv = nh * dk, nh * dv
    r3 = lambda a: a.reshape(n_seq, 1, a.shape[-1])
    per_seq = lambda b: (b, 0, 0)
    sb = SEQS_PER_GLA_STEP
    assert n_seq % sb == 0
    in_specs = [pl.BlockSpec((sb, 1, qk), per_seq), pl.BlockSpec((sb, 1, qk), per_seq),
                pl.BlockSpec((sb, 1, qk), per_seq), pl.BlockSpec((sb, 1, vv), per_seq),
                pl.BlockSpec((sb, 1, vv), per_seq), pl.BlockSpec((1, dv), lambda b: (0, 0)),
                pl.BlockSpec((sb, 1, nh, dk, dv), lambda b: (b, o_idx, 0, 0, 0))]
    args = [r3(q), r3(k), r3(gl), r3(v), r3(sr), g_o.reshape(1, dv), state_gla]
    aliases = {}
    if s_prev is not None:
        in_specs.append(pl.BlockSpec(memory_space=pl.ANY))
        args.append(s_prev)
        aliases = {len(args) - 1: 1}
    return pl.pallas_call(
        functools.partial(_gla_dec_kernel, nh=nh, dk=dk, dv=dv, aliased=s_prev is not None),
        grid=(n_seq // sb,),
        in_specs=in_specs,
        out_specs=[pl.BlockSpec((sb, 1, vv), per_seq),
                   pl.BlockSpec((sb, None, nh, dk, dv), lambda b: (b, o_idx, 0, 0, 0))],
        out_shape=[jax.ShapeDtypeStruct((n_seq, 1, vv), F32), jax.ShapeDtypeStruct(state_gla.shape, F32)],
        input_output_aliases=aliases,
        compiler_params=_cparams("arbitrary"),
        name="gla_dec",
    )(*args)


def _post_kernel(x_ref, a1_ref, a2_ref, wo_ref, gtm_ref, gf_ref, shf_ref, scf_ref, gtf_ref, w1_ref, w2_ref,
                 o_ref, h_sc, *, ka):
    f = pl.program_id(2)

    @pl.when(f == 0)
    def _():
        m = _dot(a1_ref[0], wo_ref[0:ka, :]) + _dot(a2_ref[0], wo_ref[ka:2 * ka, :])
        x1 = x_ref[0] + gtm_ref[0] * m
        o_ref[0] = x1
        h_sc[...] = _norm_mod(x1, gf_ref[...], shf_ref[0], scf_ref[0]).astype(BF16)

    tf = w1_ref.shape[1]
    acc = None
    for c in range(2):
        cols = slice(c * tf // 2, (c + 1) * tf // 2)
        t = jnp.maximum(_dot(h_sc[...], w1_ref[:, cols]), 0.0)
        part = _dot((t * t).astype(BF16), w2_ref[cols, :])
        acc = part if acc is None else acc + part
    o_ref[0] += gtf_ref[0] * acc


def _post(x, a1, a1_col, a2, a2_col, w_out, gain_f, mod, w1, w2, li, tm, tf):
    g, l, d = x.shape
    ka = w_out.shape[1] // 2
    dff = w1.shape[2]
    row = lambda b, i, f: (b, i, 0)
    modc = lambda c: _mod_spec(mod, li, c, d, 3)
    return pl.pallas_call(
        functools.partial(_post_kernel, ka=ka),
        grid=(g, l // tm, dff // tf),
        in_specs=[pl.BlockSpec((1, tm, d), row),
                  pl.BlockSpec((1, tm, ka), lambda b, i, f: (b, i, a1_col)),
                  pl.BlockSpec((1, tm, ka), lambda b, i, f: (b, i, a2_col)),
                  _layer(w_out, li // 2), modc(2), _layer(gain_f, li), modc(3), modc(4), modc(5),
                  pl.BlockSpec((None, d, tf), lambda b, i, f: (li, 0, f)),
                  pl.BlockSpec((None, tf, d), lambda b, i, f: (li, f, 0))],
        out_specs=pl.BlockSpec((1, tm, d), row),
        out_shape=jax.ShapeDtypeStruct((g, l, d), F32),
        scratch_shapes=[pltpu.VMEM((tm, d), BF16)],
        compiler_params=_cparams("arbitrary", "arbitrary", "arbitrary"),
        name="post_mlp",
    )(x, a1, a2, w_out, mod, gain_f, mod, mod, mod, w1, w2)


def kernel(x_prompt, x_sample, c_prompt, c_sample, cache_k, cache_v, cache_logf, page_table, state_conv, state_gla,
           g_mix, g_ffn, w_ada, b_ada, w_in_even, b_f, g_q, g_k, w_conv, w_out_even, w_in_odd, w_gk, b_gk, g_o,
           w_out_odd, w_ff1, w_ff2):
    bp, seq, d = x_prompt.shape
    n_seq = x_sample.shape[0]
    depth = w_ada.shape[0]
    n_even, nh_fox = b_f.shape
    hd_fox = g_q.shape[1]
    cc = nh_fox * hd_fox
    _, n_odd, nh_gla, dk, dv = state_gla.shape
    qk, vv = nh_gla * dk, nh_gla * dv

    mod_all = _adaln(jnp.concatenate([c_prompt, c_sample], axis=0), w_ada, b_ada)

    cache_kt = jnp.transpose(cache_k, (0, 1, 3, 4, 2))
    cache_vt = jnp.transpose(cache_v, (0, 1, 3, 4, 2))
    cache_lft = jnp.swapaxes(cache_logf, 2, 3)

    mod_p = mod_all[:, :bp].reshape(depth, bp, 1, 6 * d)
    mod_s = mod_all[:, bp:].reshape(depth, 1, n_seq, 6 * d)
    gm = g_mix.reshape(depth, 1, d)
    gf = g_ffn.reshape(depth, 1, d)
    w1 = w_ff1.astype(BF16)
    w2 = w_ff2.astype(BF16)
    wts_even = _even_weights(w_in_even, b_f, g_q, g_k, nh_fox, hd_fox)
    wts_odd = _odd_weights(w_in_odd, w_gk, b_gk, qk, vv)
    wo_even = w_out_even.astype(BF16)
    wo_odd = w_out_odd.astype(BF16)
    go = g_o.reshape(n_odd, 1, dv)

    xp = x_prompt
    xs = x_sample.reshape(1, n_seq, d)
    kv_rows = None
    gla_s = None
    lf_rows_p, fox_s, conv_p, conv_s, gla_p = [], [], [], [], []
    for li in range(depth):
        if li % 2 == 0:
            e = li // 2
            ya, utail, qp, kp, va, kt_rows, vt_rows, lf_rows = _even_in_seq(
                xp, gm, mod_p, wts_even, w_conv, li, kv_rows, nh_fox, hd_fox, tm=TM_EVEN)
            kv_rows = (kt_rows, vt_rows)
            o = _attn_prompt(qp, kp, va, nh_fox, hd_fox, tq=TQ_ATTN)
            xp = _post(xp, ya, 0, o, 0, wo_even, gf, mod_p, w1, w2, li, tm=TM_POST, tf=TF_POST)
            lf_rows_p.append(lf_rows)
            conv_p.append(utail[:, 6:8])
            prev0 = state_conv[:, e, 0].reshape(1, n_seq, cc)
            prev1 = state_conv[:, e, 1].reshape(1, n_seq, cc)
            ya, u, qn, kn, v, lfpad = _even_in_dec(xs, gm, mod_s, wts_even, w_conv, li, prev0, prev1, nh_fox, hd_fox)
            lf_new = lfpad[0, :, :nh_fox]
            o = _attn_decode(page_table, qn[0], kn[0], v[0], lf_new, cache_kt, cache_vt, cache_lft, e,
                             nh_fox, hd_fox)
            o = o.reshape(1, n_seq, cc).astype(BF16)
            xs = _post(xs, ya, 0, o, 0, wo_even, gf, mod_s, w1, w2, li, tm=n_seq, tf=TF_POST)
            fox_s.append((kn[0], v[0], lf_new))
            conv_s.append(jnp.stack([prev1[0], u[0]], axis=1))
        else:
            oi = li // 2
            q, k, v, sr, gl = _odd_in(xp, gm, mod_p, wts_odd, li, qk, vv, dk, tm=TM_ODD)
            og, s_t = _gla_seq(q, k, gl, v, sr, go, oi, nh_gla, dk, dv, tt=TT_GLA)
            xp = _post(xp, og, 0, og, 1, wo_odd, gf, mod_p, w1, w2, li, tm=TM_POST, tf=TF_POST)
            gla_p.append(jnp.swapaxes(s_t, 2, 3))
            q, k, v, sr, gl = _odd_in(xs, gm, mod_s, wts_odd, li, qk, vv, dk, tm=n_seq)
            og, gla_s = _gla_dec(q[0], k[0], gl[0], v[0].astype(F32), sr[0], g_o[oi], state_gla, oi, gla_s,
                                 nh_gla, dk, dv)
            og = og.reshape(1, n_seq, vv).astype(BF16)
            xs = _post(xs, og, 0, og, 1, wo_odd, gf, mod_s, w1, w2, li, tm=n_seq, tf=TF_POST)

    heads = lambda a, n: a.reshape(a.shape[0], n, nh_fox, hd_fox)
    k_p = jnp.transpose(kv_rows[0], (0, 1, 4, 2, 3))
    v_p = jnp.transpose(kv_rows[1], (0, 1, 4, 2, 3))
    lf_p = jnp.stack(lf_rows_p, axis=1)
    k_s = jnp.stack([heads(r[0], 1) for r in fox_s], axis=1)
    v_s = jnp.stack([heads(r[1], 1) for r in fox_s], axis=1)
    lf_s = jnp.stack([r[2].reshape(n_seq, 1, nh_fox) for r in fox_s], axis=1)
    return (xp, xs.reshape(n_seq, 1, d), k_p, v_p, lf_p, k_s, v_s, lf_s,
            jnp.stack(conv_p, axis=1), jnp.stack(conv_s, axis=1),
            jnp.stack(gla_p, axis=1), gla_s)
```
